```python
import math
import jax, jax.numpy as jnp
from jax import lax
import numpy as np

D_MODEL = 1024
BATCH = 32
SEQ = 2048
DEPTH = 4
DEC_BATCH = 1
DEC_SEQ = 16384
PAST_LEN = 128

N_MEM = 256
GRID_W = 64
CHUNK = 128
RET_HEADS = 8
RET_DK = 128
RET_DV = 128
RET_W = RET_HEADS * RET_DV
SG_GROUPS = 4
SG_W = 1024
ATT_HEADS = 8
ATT_KV_HEADS = 2
ATT_HD = 128
ATT_W = ATT_HEADS * ATT_HD
ATT_KV_W = ATT_KV_HEADS * ATT_HD
X_HEADS = 4
X_HD = D_MODEL // X_HEADS
D_FF = -(-8 * D_MODEL // (3 * 256)) * 256
N_BRANCH = 3
ALPHA = (2 * DEPTH) ** 0.25
BETA = (8 * DEPTH) ** -0.25
ROPE_BASE = 10000.0
LN_EPS = 1e-5
RMS_EPS = 1e-6
SPLIT_POINTS = (RET_W, 2 * RET_W, 3 * RET_W, 4 * RET_W,
                4 * RET_W + SG_W, 4 * RET_W + 2 * SG_W,
                4 * RET_W + 2 * SG_W + ATT_W,
                4 * RET_W + 2 * SG_W + ATT_W + ATT_KV_W,
                4 * RET_W + 2 * SG_W + ATT_W + 2 * ATT_KV_W)
D_IN = 4 * RET_W + 2 * SG_W + ATT_W + 2 * ATT_KV_W + N_BRANCH * D_MODEL

kernel_name = "hybrid_gated_encoder"


def layer_norm(x, w, b):
    xf = x.astype(jnp.float32)
    mu = jnp.mean(xf, axis=-1, keepdims=True)
    var = jnp.mean(jnp.square(xf - mu), axis=-1, keepdims=True)
    y = (xf - mu) * lax.rsqrt(var + LN_EPS)
    return (y * w.astype(jnp.float32) + b.astype(jnp.float32)).astype(x.dtype)


def rms_norm(x, w):
    xf = x.astype(jnp.float32)
    y = xf * lax.rsqrt(jnp.mean(jnp.square(xf), axis=-1, keepdims=True) + RMS_EPS)
    return (y * w.astype(jnp.float32)).astype(x.dtype)


def rope_cos_sin(pos, dim):
    inv_freq = ROPE_BASE ** (-jnp.arange(0, dim, 2, dtype=jnp.float32) / dim)
    ang = pos.astype(jnp.float32)[:, None] * inv_freq[None, :]
    return jnp.cos(ang), jnp.sin(ang)


def apply_rope(x, cos, sin):
    x1, x2 = jnp.split(x, 2, axis=-1)
    c = cos.astype(x.dtype)
    s = sin.astype(x.dtype)
    return jnp.concatenate([x1 * c - x2 * s, x1 * s + x2 * c], axis=-1)


def axial_rope(x, cos_r, sin_r, cos_c, sin_c):
    half = x.shape[-1] // 2
    xr = apply_rope(x[..., :half], cos_r[:, None], sin_r[:, None])
    xc = apply_rope(x[..., half:], cos_c[:, None], sin_c[:, None])
    return jnp.concatenate([xr, xc], axis=-1)


def retention_direction(q, k, v, log_gamma, strict):
    B, H, S, dk = q.shape
    dv = v.shape[-1]
    n = S // CHUNK
    qc = q.reshape(B, H, n, CHUNK, dk)
    kc = k.reshape(B, H, n, CHUNK, dk)
    vc = v.reshape(B, H, n, CHUNK, dv)
    idx = jnp.arange(CHUNK, dtype=jnp.float32)
    diff = idx[:, None] - idx[None, :]
    lg = log_gamma.astype(jnp.float32)
    allowed = (diff > 0) if strict else (diff >= 0)
    dmat = jnp.where(allowed[None], jnp.exp(lg[:, None, None] * jnp.maximum(diff, 0.0)[None]), 0.0)
    dmat = dmat.astype(q.dtype)
    scores = jnp.einsum('bhncd,bhnmd->bhncm', qc, kc) * dmat[None, :, None]
    intra = jnp.einsum('bhncm,bhnme->bhnce', scores, vc)
    zeta = jnp.exp(lg[:, None] * (CHUNK - 1 - idx)[None]).astype(q.dtype)
    xi = jnp.exp(lg[:, None] * (idx + 1.0)[None]).astype(q.dtype)
    kv = jnp.einsum('bhncd,bhnce->nbhde', kc * zeta[None, :, None, :, None], vc)
    chunk_decay = jnp.exp(lg * CHUNK).astype(q.dtype)[None, :, None, None]

    def step(state, kv_i):
        return state * chunk_decay + kv_i, state

    _, prev = lax.scan(step, jnp.zeros((B, H, dk, dv), kv.dtype), kv)
    inter = jnp.einsum('bhncd,nbhde->bhnce', qc * xi[None, :, None, :, None], prev)
    return (intra + inter).reshape(B, H, S, dv)


def bidirectional_retention(q, k, v, log_gamma_f, log_gamma_b):
    fwd = retention_direction(q, k, v, log_gamma_f, False)
    flip = lambda t: jnp.flip(t, axis=2)
    bwd = flip(retention_direction(flip(q), flip(k), flip(v), log_gamma_b, True))
    return fwd + bwd


def head_group_norm(o, w):
    B, H, S, dv = o.shape
    of = o.astype(jnp.float32)
    mu = jnp.mean(of, axis=-1, keepdims=True)
    var = jnp.mean(jnp.square(of - mu), axis=-1, keepdims=True)
    y = ((of - mu) * lax.rsqrt(var + LN_EPS)).transpose(0, 2, 1, 3).reshape(B, S, H * dv)
    return (y * w.astype(jnp.float32)).astype(o.dtype)


def gqa_block_attention(q, k, v):
    B, S, H, hd = q.shape
    G = H // ATT_KV_HEADS
    qb = q.reshape(B, S // CHUNK, CHUNK, ATT_KV_HEADS, G, hd).transpose(1, 0, 3, 4, 2, 5)
    kt = k.transpose(0, 2, 1, 3)
    vt = v.transpose(0, 2, 1, 3)
    scale = hd ** -0.5

    def block(qi):
        s = jnp.einsum('bkgqd,bksd->bkgqs', qi, kt).astype(jnp.float32) * scale
        p = jax.nn.softmax(s, axis=-1).astype(vt.dtype)
        return jnp.einsum('bkgqs,bksd->bkgqd', p, vt)

    out = lax.map(block, qb)
    return out.transpose(1, 0, 4, 2, 3, 5).reshape(B, S, H * hd)


def token_mixer(x, l, p, rope):
    B, S, _ = x.shape
    cos_t, sin_t, cos_r, sin_r, cos_c, sin_c = rope
    h = x @ p['w_in'][l]
    rq, rk, rv, rg, su, sv, aq, ak, av, gate_logits = jnp.split(h, SPLIT_POINTS, axis=-1)

    heads = lambda t: t.reshape(B, S, RET_HEADS, -1).transpose(0, 2, 1, 3)
    rq = apply_rope(heads(rq), cos_t, sin_t)
    rk = apply_rope(heads(rk), cos_t, sin_t) * (RET_DK ** -0.5)
    ro = bidirectional_retention(rq, rk, heads(rv),
                                 jax.nn.log_sigmoid(p['ret_decay_f'][l].astype(jnp.float32)),
                                 jax.nn.log_sigmoid(p['ret_decay_b'][l].astype(jnp.float32)))
    ro = head_group_norm(ro, p['ret_gn_w'][l])
    ret_out = (jax.nn.silu(rg) * ro) @ p['ret_wo'][l]

    su = jax.nn.gelu(su)
    sv = layer_norm(jax.nn.gelu(sv), p['sg_ln_w'][l], p['sg_ln_b'][l])
    svc = sv.reshape(B, S // CHUNK, CHUNK, SG_GROUPS, SG_W // SG_GROUPS)
    mixed = jnp.einsum('gcm,bnmgd->bncgd', p['sg_ws'][l], svc) + p['sg_b'][l].T[:, :, None]
    sg_out = (su * mixed.reshape(B, S, SG_W)) @ p['sg_wo'][l]

    aq = rms_norm(aq.reshape(B, S, ATT_HEADS, ATT_HD), p['att_qn_w'][l])
    ak = rms_norm(ak.reshape(B, S, ATT_KV_HEADS, ATT_HD), p['att_kn_w'][l])
    av = av.reshape(B, S, ATT_KV_HEADS, ATT_HD)
    aq = axial_rope(aq, cos_r, sin_r, cos_c, sin_c)
    ak = axial_rope(ak, cos_r, sin_r, cos_c, sin_c)
    att_out = gqa_block_attention(aq, ak, av) @ p['att_wo'][l]

    gates = jax.nn.sigmoid(gate_logits + p['b_gate'][l]).reshape(B, S, N_BRANCH, D_MODEL)
    merged = gates[:, :, 0] * ret_out + gates[:, :, 1] * sg_out + gates[:, :, 2] * att_out
    return merged @ p['w_out'][l]


def cross_attention(x, mem, wq, wkv, wo):
    B, S, _ = x.shape
    M = mem.shape[1]
    q = (x @ wq).reshape(B, S, X_HEADS, X_HD)
    k, v = jnp.split(mem @ wkv, 2, axis=-1)
    k = k.reshape(B, M, X_HEADS, X_HD)
    v = v.reshape(B, M, X_HEADS, X_HD)
    s = jnp.einsum('bshd,bmhd->bhsm', q, k).astype(jnp.float32) * (X_HD ** -0.5)
    pr = jax.nn.softmax(s, axis=-1).astype(v.dtype)
    o = jnp.einsum('bhsm,bmhd->bshd', pr, v).reshape(B, S, D_MODEL)
    return o @ wo


def swiglu(x, w_in, w_out):
    a, b = jnp.split(x @ w_in, 2, axis=-1)
    return (jax.nn.silu(a) * b) @ w_out


def encoder(x, mem, p):
    S = x.shape[1]
    rows = S // GRID_W
    grid_r, grid_c = jnp.meshgrid(jnp.arange(rows), jnp.arange(GRID_W), indexing='ij')
    row = grid_r.reshape(S)
    col = grid_c.reshape(S)
    t = jnp.arange(S)
    cos_t, sin_t = rope_cos_sin(t, RET_DK)
    cos_r, sin_r = rope_cos_sin(row, ATT_HD // 2)
    cos_c, sin_c = rope_cos_sin(col, ATT_HD // 2)
    rope = (cos_t, sin_t, cos_r, sin_r, cos_c, sin_c)
    x = layer_norm(x, p['in_ln_w'], p['in_ln_b'])
    for l in range(DEPTH):
        x = layer_norm(ALPHA * x + token_mixer(x, l, p, rope), p['ln_w'][l, 0], p['ln_b'][l, 0])
        x = layer_norm(ALPHA * x + cross_attention(x, mem, p['xa_wq'][l], p['xa_wkv'][l], p['xa_wo'][l]),
                       p['ln_w'][l, 1], p['ln_b'][l, 1])
        x = layer_norm(ALPHA * x + swiglu(x, p['ffn_w_in'][l], p['ffn_w_out'][l]),
                       p['ln_w'][l, 2], p['ln_b'][l, 2])
    return x


def setup_inputs(seed: int = 0) -> dict:
    key = jax.random.key(seed)
    ks = jax.random.split(key, 32)
    nrm = lambda k, shape, scale: jax.random.normal(k, shape, jnp.float32) * scale
    base_logit = jnp.log(jnp.exp2(5.0 + jnp.arange(RET_HEADS, dtype=jnp.float32)) - 1.0)
    return {
        "x_prompt": nrm(ks[0], (BATCH, SEQ, D_MODEL), 1.0),
        "x_sample": nrm(ks[1], (DEC_BATCH, DEC_SEQ, D_MODEL), 1.0),
        "mem_prompt": nrm(ks[2], (BATCH, N_MEM, D_MODEL), 1.0),
        "mem_sample": nrm(ks[3], (DEC_BATCH, N_MEM, D_MODEL), 1.0),
        "in_ln_w": 1.0 + nrm(ks[4], (D_MODEL,), 0.02),
        "in_ln_b": nrm(ks[5], (D_MODEL,), 0.02),
        "w_in": nrm(ks[6], (DEPTH, D_MODEL, D_IN), D_MODEL ** -0.5),
        "b_gate": nrm(ks[7], (DEPTH, N_BRANCH * D_MODEL), 0.02),
        "ret_decay_f": base_logit[None] + nrm(ks[8], (DEPTH, RET_HEADS), 0.1),
        "ret_decay_b": base_logit[None] + nrm(ks[9], (DEPTH, RET_HEADS), 0.1),
        "ret_gn_w": 1.0 + nrm(ks[10], (DEPTH, RET_W), 0.02),
        "ret_wo": nrm(ks[11], (DEPTH, RET_W, D_MODEL), BETA * RET_W ** -0.5),
        "sg_ln_w": 1.0 + nrm(ks[12], (DEPTH, SG_W), 0.02),
        "sg_ln_b": nrm(ks[13], (DEPTH, SG_W), 0.02),
        "sg_ws": nrm(ks[14], (DEPTH, SG_GROUPS, CHUNK, CHUNK), CHUNK ** -0.5),
        "sg_b": 1.0 + nrm(ks[15], (DEPTH, SG_GROUPS, CHUNK), 0.02),
        "sg_wo": nrm(ks[16], (DEPTH, SG_W, D_MODEL), BETA * SG_W ** -0.5),
        "att_qn_w": 1.0 + nrm(ks[17], (DEPTH, ATT_HD), 0.02),
        "att_kn_w": 1.0 + nrm(ks[18], (DEPTH, ATT_HD), 0.02),
        "att_wo": nrm(ks[19], (DEPTH, ATT_W, D_MODEL), BETA * ATT_W ** -0.5),
        "w_out": nrm(ks[20], (DEPTH, D_MODEL, D_MODEL), BETA * D_MODEL ** -0.5),
        "ln_w": 1.0 + nrm(ks[21], (DEPTH, 3, D_MODEL), 0.02),
        "ln_b": nrm(ks[22], (DEPTH, 3, D_MODEL), 0.02),
        "xa_wq": nrm(ks[23], (DEPTH, D_MODEL, D_MODEL), D_MODEL ** -0.5),
        "xa_wkv": nrm(ks[24], (DEPTH, D_MODEL, 2 * D_MODEL), D_MODEL ** -0.5),
        "xa_wo": nrm(ks[25], (DEPTH, D_MODEL, D_MODEL), BETA * D_MODEL ** -0.5),
        "ffn_w_in": nrm(ks[26], (DEPTH, D_MODEL, 2 * D_FF), D_MODEL ** -0.5),
        "ffn_w_out": nrm(ks[27], (DEPTH, D_FF, D_MODEL), BETA * D_FF ** -0.5),
    }


def reference(x_prompt, x_sample, mem_prompt, mem_sample, in_ln_w, in_ln_b, w_in, b_gate,
              ret_decay_f, ret_decay_b, ret_gn_w, ret_wo, sg_ln_w, sg_ln_b, sg_ws, sg_b, sg_wo,
              att_qn_w, att_kn_w, att_wo, w_out, ln_w, ln_b, xa_wq, xa_wkv, xa_wo,
              ffn_w_in, ffn_w_out):
    p = dict(in_ln_w=in_ln_w, in_ln_b=in_ln_b, w_in=w_in, b_gate=b_gate,
             ret_decay_f=ret_decay_f, ret_decay_b=ret_decay_b, ret_gn_w=ret_gn_w, ret_wo=ret_wo,
             sg_ln_w=sg_ln_w, sg_ln_b=sg_ln_b, sg_ws=sg_ws, sg_b=sg_b, sg_wo=sg_wo,
             att_qn_w=att_qn_w, att_kn_w=att_kn_w, att_wo=att_wo, w_out=w_out,
             ln_w=ln_w, ln_b=ln_b, xa_wq=xa_wq, xa_wkv=xa_wkv, xa_wo=xa_wo,
             ffn_w_in=ffn_w_in, ffn_w_out=ffn_w_out)
    y_prompt = encoder(x_prompt, mem_prompt, p)
    y_sample = encoder(x_sample, mem_sample, p)
    return (y_prompt, y_sample)
```

```python
import functools
import math

import jax
import jax.numpy as jnp
from jax import lax
from jax.experimental import pallas as pl
from jax.experimental.pallas import tpu as pltpu

F32 = jnp.float32
BF16 = jnp.bfloat16

D_MODEL = 1024
DEPTH = 4
N_MEM = 256
GRID_W = 64
CHUNK = 128
RET_HEADS = 8
RET_DK = 128
SG_GROUPS = 4
SG_GW = 256
ATT_HEADS = 8
ATT_KV_HEADS = 2
ATT_GROUP = ATT_HEADS // ATT_KV_HEADS
ATT_HD = 128
X_HEADS = 4
X_HD = D_MODEL // X_HEADS
D_FF = 2816
FF_CHUNK = 256
ALPHA = (2 * DEPTH) ** 0.25
ROPE_BASE = 10000.0
LN_EPS = 1e-5
RMS_EPS = 1e-6
LOG2E = 1.4426950408889634

D_IN = 10752
COL_RQ, COL_RK, COL_RV, COL_RG = 0, 1024, 2048, 3072
COL_SU, COL_SV, COL_AQ = 4096, 5120, 6144
COL_GATE = 7168
COL_AK, COL_AV = 10240, 10496

VMEM_LIMIT = 52 * 1024 * 1024


def _cparams(sem):
    return pltpu.CompilerParams(dimension_semantics=sem, vmem_limit_bytes=VMEM_LIMIT)


def _const_spec(shape):
    nd = len(shape)
    return pl.BlockSpec(shape, lambda *_: (0,) * nd, pipeline_mode=pl.Buffered(1))


def _layer_norm_rows(z, w, b):
    mu = jnp.mean(z, axis=-1, keepdims=True)
    d = z - mu
    var = jnp.mean(d * d, axis=-1, keepdims=True)
    return d * lax.rsqrt(var + LN_EPS) * w + b


def _gelu_tanh(x):
    return 0.5 * x * (1.0 + jnp.tanh(0.7978845608028654 * (x + 0.044715 * (x * x * x))))


def _sigmoid(x):
    return 1.0 / (1.0 + jnp.exp(-x))


def _in_ln_kernel(x_ref, w_ref, b_ref, xo_ref, xb_ref):
    y = _layer_norm_rows(x_ref[...], w_ref[...], b_ref[...])
    xo_ref[...] = y
    xb_ref[...] = y.astype(BF16)


def _in_ln(x2, w, b, tm):
    T = x2.shape[0]
    row = pl.BlockSpec((tm, D_MODEL), lambda i: (i, 0))
    return pl.pallas_call(
        _in_ln_kernel,
        grid=(T // tm,),
        in_specs=[row, _const_spec((1, D_MODEL)), _const_spec((1, D_MODEL))],
        out_specs=[row, row],
        out_shape=[jax.ShapeDtypeStruct((T, D_MODEL), F32), jax.ShapeDtypeStruct((T, D_MODEL), BF16)],
        compiler_params=_cparams(("parallel",)),
        name="in_ln",
    )(x2, w, b)


def _mm_kernel(x_ref, w_ref, o_ref):
    o_ref[...] = jnp.dot(x_ref[...], w_ref[...], preferred_element_type=F32).astype(o_ref.dtype)


def _matmul(x, w, tm, tn, name):
    M, K = x.shape
    N = w.shape[1]
    return pl.pallas_call(
        _mm_kernel,
        grid=(N // tn, M // tm),
        in_specs=[pl.BlockSpec((tm, K), lambda j, i: (i, 0)), pl.BlockSpec((K, tn), lambda j, i: (0, j))],
        out_specs=pl.BlockSpec((tm, tn), lambda j, i: (i, j)),
        out_shape=jax.ShapeDtypeStruct((M, N), BF16),
        compiler_params=_cparams(("parallel", "parallel")),
        name=name,
    )(x, w)


def _rope(x, cos, sin_signed, partner):
    return x * cos + partner * sin_signed


def _prep_kernel(rq_ref, rk_ref, aq_ref, ak_ref, rc_ref, rs_ref, ac_ref, as_ref, qnw_ref, knw_ref,
                 rqo_ref, rkt_ref, aqo_ref, akt_ref, *, ts):
    rc = rc_ref[...]
    rs = rs_ref[...]
    ac = ac_ref[...]
    as_ = as_ref[...]
    lane = lax.broadcasted_iota(jnp.int32, (ts, ATT_HD), 1)
    low_half = (lane % 64) < 32

    def ret_rope(x):
        return _rope(x, rc, rs, pltpu.roll(x, 64, axis=1))

    def axial_rope(x):
        partner = jnp.where(low_half, pltpu.roll(x, 96, axis=1), pltpu.roll(x, 32, axis=1))
        return _rope(x, ac, as_, partner)

    def rms(x, w):
        return x * lax.rsqrt(jnp.mean(x * x, axis=-1, keepdims=True) + RMS_EPS) * w

    k_scale = RET_DK ** -0.5
    for h in range(RET_HEADS):
        sl = slice(h * RET_DK, (h + 1) * RET_DK)
        rqo_ref[0, :, sl] = ret_rope(rq_ref[0, :, sl].astype(F32)).astype(BF16)
        kt = (ret_rope(rk_ref[0, :, sl].astype(F32)) * k_scale).T
        for c in range(ts // CHUNK):
            rkt_ref[0, h, c] = kt[:, c * CHUNK:(c + 1) * CHUNK].astype(BF16)

    qw = qnw_ref[...] * (ATT_HD ** -0.5 * LOG2E)
    kw = knw_ref[...]
    for h in range(ATT_HEADS):
        sl = slice(h * ATT_HD, (h + 1) * ATT_HD)
        aqo_ref[0, :, sl] = axial_rope(rms(aq_ref[0, :, sl].astype(F32), qw)).astype(BF16)
    for h in range(ATT_KV_HEADS):
        sl = slice(h * ATT_HD, (h + 1) * ATT_HD)
        akt_ref[0, h] = axial_rope(rms(ak_ref[0, :, sl].astype(F32), kw)).T.astype(BF16)


def _prep(h3, tabs, qnw, knw, ts):
    B, S, _ = h3.shape
    rc, rs, ac, as_ = tabs
    tab = pl.BlockSpec((ts, ATT_HD), lambda b, s: (s, 0))
    wide = lambda blk: pl.BlockSpec((1, ts, 1024), lambda b, s: (b, s, blk))
    return pl.pallas_call(
        functools.partial(_prep_kernel, ts=ts),
        grid=(B, S // ts),
        in_specs=[wide(COL_RQ // 1024), wide(COL_RK // 1024), wide(COL_AQ // 1024),
                  pl.BlockSpec((1, ts, 256), lambda b, s: (b, s, COL_AK // 256)),
                  tab, tab, tab, tab, _const_spec((1, ATT_HD)), _const_spec((1, ATT_HD))],
        out_specs=[wide(0),
                   pl.BlockSpec((1, RET_HEADS, ts // CHUNK, RET_DK, CHUNK), lambda b, s: (b, 0, s, 0, 0)),
                   wide(0),
                   pl.BlockSpec((1, ATT_KV_HEADS, ATT_HD, ts), lambda b, s: (b, 0, 0, s))],
        out_shape=[jax.ShapeDtypeStruct((B, S, 1024), BF16),
                   jax.ShapeDtypeStruct((B, RET_HEADS, S // CHUNK, RET_DK, CHUNK), BF16),
                   jax.ShapeDtypeStruct((B, S, 1024), BF16),
                   jax.ShapeDtypeStruct((B, ATT_KV_HEADS, ATT_HD, S), BF16)],
        compiler_params=_cparams(("parallel", "parallel")),
        name="prep_qk",
    )(h3, h3, h3, h3, rc, rs, ac, as_, qnw, knw)


def _ret_kernel(q_ref, kt_ref, v_ref, g_ref, dec_ref, gnw_ref, o_ref, acc_ref, *, n_chunks):
    dtot = dec_ref[0, 0]
    xif, xib = dec_ref[0, 1], dec_ref[0, 2]
    zf, zb = dec_ref[0, 3], dec_ref[0, 4]
    cdf, cdb = dec_ref[0, 5], dec_ref[0, 6]
    gnw = gnw_ref[...]

    def load(c):
        r = pl.multiple_of(c * CHUNK, CHUNK)
        rows = pl.ds(r, CHUNK)
        return rows, q_ref[0, rows, :], kt_ref[0, 0, c], v_ref[0, rows, :]

    def fwd(c, state):
        rows, q, kt, v = load(c)
        s = jnp.dot(q, kt, preferred_element_type=F32)
        o = jnp.dot((s * dtot).astype(BF16), v, preferred_element_type=F32)
        qx = (q.astype(F32) * xif).astype(BF16)
        o = o + jnp.dot(qx, state.astype(BF16), preferred_element_type=F32)
        acc_ref[rows, :] = o
        kz = (kt.astype(F32) * zf).astype(BF16)
        return state * cdf + jnp.dot(kz, v, preferred_element_type=F32)

    zero = jnp.zeros((RET_DK, RET_DK), F32)
    lax.fori_loop(0, n_chunks, fwd, zero)

    def bwd(i, state):
        c = n_chunks - 1 - i
        rows, q, kt, v = load(c)
        qx = (q.astype(F32) * xib).astype(BF16)
        tot = acc_ref[rows, :] + jnp.dot(qx, state.astype(BF16), preferred_element_type=F32)
        mu = jnp.mean(tot, axis=-1, keepdims=True)
        d = tot - mu
        var = jnp.mean(d * d, axis=-1, keepdims=True)
        y = d * lax.rsqrt(var + LN_EPS) * gnw
        g = g_ref[0, rows, :].astype(F32)
        o_ref[0, rows, :] = (y * (g * _sigmoid(g))).astype(BF16)
        kz = (kt.astype(F32) * zb).astype(BF16)
        return state * cdb + jnp.dot(kz, v, preferred_element_type=F32)

    lax.fori_loop(0, n_chunks, bwd, zero)


def _retention(rq, rkt, h3, dec, gnw):
    B, S, _ = rq.shape
    n = S // CHUNK
    head = lambda base: pl.BlockSpec((1, S, RET_DK), lambda b, h: (b, 0, base + h))
    return pl.pallas_call(
        functools.partial(_ret_kernel, n_chunks=n),
        grid=(B, RET_HEADS),
        in_specs=[head(0),
                  pl.BlockSpec((1, 1, n, RET_DK, CHUNK), lambda b, h: (b, h, 0, 0, 0)),
                  head(COL_RV // RET_DK), head(COL_RG // RET_DK),
                  pl.BlockSpec((1, 7, CHUNK, CHUNK), lambda b, h: (h, 0, 0, 0)),
                  pl.BlockSpec((1, RET_DK), lambda b, h: (0, h))],
        out_specs=head(0),
        out_shape=jax.ShapeDtypeStruct((B, S, 1024), BF16),
        scratch_shapes=[pltpu.VMEM((S, RET_DK), F32)],
        compiler_params=_cparams(("parallel", "parallel")),
        name="retention",
    )(rq, rkt, h3, h3, dec, gnw)


def _sg_kernel(u_ref, v_ref, lnw_ref, lnb_ref, ws_ref, bias_ref, o_ref, *, ts):
    lnw = lnw_ref[...]
    lnb = lnb_ref[...]
    for c in range(ts // CHUNK):
        rows = slice(c * CHUNK, (c + 1) * CHUNK)
        vn = _layer_norm_rows(_gelu_tanh(v_ref[0, rows, :].astype(F32)), lnw, lnb).astype(BF16)
        u = _gelu_tanh(u_ref[0, rows, :].astype(F32))
        for g in range(SG_GROUPS):
            cols = slice(g * SG_GW, (g + 1) * SG_GW)
            mixed = jnp.dot(ws_ref[g], vn[:, cols], preferred_element_type=F32) + bias_ref[:, cols]
            o_ref[0, rows, cols] = (u[:, cols] * mixed).astype(BF16)


def _spatial_gate(h3, lnw, lnb, ws, bias, ts):
    B, S, _ = h3.shape
    wide = lambda blk: pl.BlockSpec((1, ts, 1024), lambda b, s: (b, s, blk))
    return pl.pallas_call(
        functools.partial(_sg_kernel, ts=ts),
        grid=(B, S // ts),
        in_specs=[wide(COL_SU // 1024), wide(COL_SV // 1024), _const_spec((1, 1024)), _const_spec((1, 1024)),
                  _const_spec((SG_GROUPS, CHUNK, CHUNK)), _const_spec((CHUNK, 1024))],
        out_specs=wide(0),
        out_shape=jax.ShapeDtypeStruct((B, S, 1024), BF16),
        compiler_params=_cparams(("parallel", "parallel")),
        name="spatial_gate",
    )(h3, h3, lnw, lnb, ws, bias)


def _attn_kernel(q_ref, kt_ref, v_ref, o_ref, m_ref, l_ref, acc_ref, *, tk):
    ki = pl.program_id(3)

    @pl.when(ki == 0)
    def _():
        m_ref[...] = jnp.full(m_ref.shape, -jnp.inf, F32)
        l_ref[...] = jnp.zeros(l_ref.shape, F32)
        acc_ref[...] = jnp.zeros(acc_ref.shape, F32)

    kt = kt_ref[0, 0]
    v = v_ref[0]
    for g in range(ATT_GROUP):
        q = q_ref[0, :, g * ATT_HD:(g + 1) * ATT_HD]
        s = jnp.dot(q, kt, preferred_element_type=F32)
        m_prev = m_ref[g]
        m_next = jnp.maximum(m_prev, jnp.max(s, axis=1, keepdims=True))
        p = jnp.exp2(s - jnp.tile(m_next, (1, tk // ATT_HD)))
        alpha = jnp.exp2(m_prev - m_next)
        l_ref[g] = alpha * l_ref[g] + jnp.sum(p, axis=1, keepdims=True)
        acc_ref[g] = alpha * acc_ref[g] + jnp.dot(p.astype(BF16), v, preferred_element_type=F32)
        m_ref[g] = m_next

    @pl.when(ki == pl.num_programs(3) - 1)
    def _():
        for g in range(ATT_GROUP):
            o_ref[0, :, g * ATT_HD:(g + 1) * ATT_HD] = (acc_ref[g] / l_ref[g]).astype(BF16)


def _attention(aq, akt, h3, tq, tk):
    B, S, _ = aq.shape
    gw = ATT_GROUP * ATT_HD
    return pl.pallas_call(
        functools.partial(_attn_kernel, tk=tk),
        grid=(B, ATT_KV_HEADS, S // tq, S // tk),
        in_specs=[pl.BlockSpec((1, tq, gw), lambda b, k, i, j: (b, i, k)),
                  pl.BlockSpec((1, 1, ATT_HD, tk), lambda b, k, i, j: (b, k, 0, j)),
                  pl.BlockSpec((1, tk, ATT_HD), lambda b, k, i, j: (b, j, COL_AV // ATT_HD + k))],
        out_specs=pl.BlockSpec((1, tq, gw), lambda b, k, i, j: (b, i, k)),
        out_shape=jax.ShapeDtypeStruct((B, S, 1024), BF16),
        scratch_shapes=[pltpu.VMEM((ATT_GROUP, tq, ATT_HD), F32)] * 3,
        compiler_params=_cparams(("parallel", "parallel", "parallel", "arbitrary")),
        name="gqa_attention",
    )(aq, akt, h3)


def _merge_kernel(ro_ref, sg_ref, at_ref, g0_ref, g1_ref, g2_ref, bg_ref, x_ref,
                  wr_ref, ws_ref, wa_ref, wo_ref, lnw_ref, lnb_ref, xo_ref, xb_ref):
    def branch(a_ref, w_ref, gl_ref, i):
        gate = _sigmoid(gl_ref[...].astype(F32) + bg_ref[:, i * D_MODEL:(i + 1) * D_MODEL])
        return gate * jnp.dot(a_ref[...], w_ref[...], preferred_element_type=F32)

    merged = branch(ro_ref, wr_ref, g0_ref, 0) + branch(sg_ref, ws_ref, g1_ref, 1) + branch(at_ref, wa_ref, g2_ref, 2)
    y = jnp.dot(merged.astype(BF16), wo_ref[...], preferred_element_type=F32)
    out = _layer_norm_rows(ALPHA * x_ref[...] + y, lnw_ref[...], lnb_ref[...])
    xo_ref[...] = out
    xb_ref[...] = out.astype(BF16)


def _merge(ro, sg, at, h2, bg, x, wr, ws, wa, wo, lnw, lnb, tm):
    T = x.shape[0]
    row = pl.BlockSpec((tm, D_MODEL), lambda i: (i, 0))
    gate = lambda k: pl.BlockSpec((tm, D_MODEL), lambda i: (i, COL_GATE // D_MODEL + k))
    wspec = _const_spec((D_MODEL, D_MODEL))
    vec = _const_spec((1, D_MODEL))
    return pl.pallas_call(
        _merge_kernel,
        grid=(T // tm,),
        in_specs=[row, row, row, gate(0), gate(1), gate(2), _const_spec((1, 3 * D_MODEL)), row,
                  wspec, wspec, wspec, wspec, vec, vec],
        out_specs=[row, row],
        out_shape=[jax.ShapeDtypeStruct((T, D_MODEL), F32), jax.ShapeDtypeStruct((T, D_MODEL), BF16)],
        compiler_params=_cparams(("parallel",)),
        name="merge_out_ln",
    )(ro, sg, at, h2, h2, h2, bg, x, wr, ws, wa, wo, lnw, lnb)


def _cross_kernel(x_ref, xb_ref, kv_ref, wq_ref, wo_ref, lnw_ref, lnb_ref, xo_ref, xbo_ref, o_scr):
    q = jnp.dot(xb_ref[0], wq_ref[...], preferred_element_type=F32).astype(BF16)
    scale = X_HD ** -0.5
    for h in range(X_HEADS):
        cols = slice(h * X_HD, (h + 1) * X_HD)
        k = kv_ref[0, :, cols]
        v = kv_ref[0, :, D_MODEL + h * X_HD:D_MODEL + (h + 1) * X_HD]
        s = lax.dot_general(q[:, cols], k, (((1,), (1,)), ((), ())), preferred_element_type=F32) * scale
        p = jnp.exp(s - jnp.max(s, axis=-1, keepdims=True))
        o = jnp.dot(p.astype(BF16), v, preferred_element_type=F32) / jnp.sum(p, axis=-1, keepdims=True)
        o_scr[:, cols] = o.astype(BF16)
    y = jnp.dot(o_scr[...], wo_ref[...], preferred_element_type=F32)
    out = _layer_norm_rows(ALPHA * x_ref[0] + y, lnw_ref[...], lnb_ref[...])
    xo_ref[0] = out
    xbo_ref[0] = out.astype(BF16)


def _cross(x3, xb3, kv3, wq, wo, lnw, lnb, tm):
    B, S, _ = x3.shape
    row = pl.BlockSpec((1, tm, D_MODEL), lambda b, i: (b, i, 0))
    wspec = _const_spec((D_MODEL, D_MODEL))
    vec = _const_spec((1, D_MODEL))
    return pl.pallas_call(
        _cross_kernel,
        grid=(B, S // tm),
        in_specs=[row, row, pl.BlockSpec((1, N_MEM, 2 * D_MODEL), lambda b, i: (b, 0, 0)), wspec, wspec, vec, vec],
        out_specs=[row, row],
        out_shape=[jax.ShapeDtypeStruct((B, S, D_MODEL), F32), jax.ShapeDtypeStruct((B, S, D_MODEL), BF16)],
        scratch_shapes=[pltpu.VMEM((tm, D_MODEL), BF16)],
        compiler_params=_cparams(("parallel", "parallel")),
        name="cross_attn_ln",
    )(x3, xb3, kv3, wq, wo, lnw, lnb)


def _ffn_kernel(x_ref, xb_ref, wa_ref, wb_ref, wo_ref, lnw_ref, lnb_ref, xo_ref, xbo_ref, h_scr):
    xb = xb_ref[...]
    for j in range(D_FF // FF_CHUNK):
        cols = slice(j * FF_CHUNK, (j + 1) * FF_CHUNK)
        a = jnp.dot(xb, wa_ref[:, cols], preferred_element_type=F32)
        b = jnp.dot(xb, wb_ref[:, cols], preferred_element_type=F32)
        h_scr[:, cols] = (a * _sigmoid(a) * b).astype(BF16)
    y = jnp.dot(h_scr[...], wo_ref[...], preferred_element_type=F32)
    out = _layer_norm_rows(ALPHA * x_ref[...] + y, lnw_ref[...], lnb_ref[...])
    xo_ref[...] = out
    xbo_ref[...] = out.astype(BF16)


def _ffn(x, xb, wa, wb, wo, lnw, lnb, tm):
    T = x.shape[0]
    row = pl.BlockSpec((tm, D_MODEL), lambda i: (i, 0))
    vec = _const_spec((1, D_MODEL))
    return pl.pallas_call(
        _ffn_kernel,
        grid=(T // tm,),
        in_specs=[row, row, _const_spec((D_MODEL, D_FF)), _const_spec((D_MODEL, D_FF)),
                  _const_spec((D_FF, D_MODEL)), vec, vec],
        out_specs=[row, row],
        out_shape=[jax.ShapeDtypeStruct((T, D_MODEL), F32), jax.ShapeDtypeStruct((T, D_MODEL), BF16)],
        scratch_shapes=[pltpu.VMEM((tm, D_FF), BF16)],
        compiler_params=_cparams(("parallel",)),
        name="swiglu_ln",
    )(x, xb, wa, wb, wo, lnw, lnb)


def _rope_tables(S):
    t = jnp.arange(S)

    def cos_sin(pos, dim):
        inv_freq = ROPE_BASE ** (-jnp.arange(0, dim, 2, dtype=F32) / dim)
        ang = pos.astype(F32)[:, None] * inv_freq[None, :]
        return jnp.cos(ang), jnp.sin(ang)

    ct, st = cos_sin(t, RET_DK)
    cr, sr = cos_sin(t // GRID_W, ATT_HD // 2)
    cc, sc = cos_sin(t % GRID_W, ATT_HD // 2)
    return (jnp.concatenate([ct, ct], -1), jnp.concatenate([-st, st], -1),
            jnp.concatenate([cr, cr, cc, cc], -1), jnp.concatenate([-sr, sr, -sc, sc], -1))


def _decay_tables(decay_f, decay_b):
    lgf = jax.nn.log_sigmoid(decay_f.astype(F32))[:, None, None]
    lgb = jax.nn.log_sigmoid(decay_b.astype(F32))[:, None, None]
    idx = jnp.arange(CHUNK, dtype=F32)
    diff = idx[:, None] - idx[None, :]
    dtot = jnp.where(diff >= 0, jnp.exp(lgf * jnp.maximum(diff, 0.0)), jnp.exp(lgb * jnp.maximum(-diff, 0.0)))
    ones = jnp.ones((CHUNK, CHUNK), F32)
    row = idx[None, :, None] * ones
    lane = idx[None, None, :] * ones
    xif = jnp.exp(lgf * (row + 1.0))
    xib = jnp.exp(lgb * (CHUNK - row))
    zf = jnp.exp(lgf * (CHUNK - 1.0 - lane))
    zb = jnp.exp(lgb * lane)
    cdf = jnp.exp(lgf * CHUNK) * ones
    cdb = jnp.exp(lgb * CHUNK) * ones
    return jnp.stack([dtot, xif, xib, zf, zb, cdf, cdb], axis=1)


def _tile(n, pref):
    t = min(n, pref)
    assert n % t == 0, (n, t)
    return t


def _encoder(x, mem, p):
    B, S, _ = x.shape
    T = B * S
    tm = _tile(T, 512)
    ts = _tile(S, 512)
    tabs = _rope_tables(S)
    xf, xb = _in_ln(x.reshape(T, D_MODEL), p["in_ln_w"], p["in_ln_b"], tm)
    memb = mem.astype(BF16).reshape(B * N_MEM, D_MODEL)
    for l in range(DEPTH):
        h2 = _matmul(xb, p["w_in"][l], _tile(T, 1024), 1536, "proj_in")
        h3 = h2.reshape(B, S, D_IN)
        rq, rkt, aq, akt = _prep(h3, tabs, p["att_qn_w"][l], p["att_kn_w"][l], ts)
        ro = _retention(rq, rkt, h3, p["dec"][l], p["ret_gn_w"][l])
        sg = _spatial_gate(h3, p["sg_ln_w"][l], p["sg_ln_b"][l], p["sg_ws"][l], p["sg_bias"][l], ts)
        at = _attention(aq, akt, h3, _tile(S, 256), _tile(S, 1024))
        xf, xb = _merge(ro.reshape(T, D_MODEL), sg.reshape(T, D_MODEL), at.reshape(T, D_MODEL), h2,
                        p["b_gate"][l], xf, p["ret_wo"][l], p["sg_wo"][l], p["att_wo"][l], p["w_out"][l],
                        p["ln_w"][l, 0], p["ln_b"][l, 0], tm)
        kv = _matmul(memb, p["xa_wkv"][l], _tile(B * N_MEM, 1024), 1024, "proj_kv")
        tmx = _tile(S, 512)
        xf3, xb3 = _cross(xf.reshape(B, S, D_MODEL), xb.reshape(B, S, D_MODEL), kv.reshape(B, N_MEM, 2 * D_MODEL),
                          p["xa_wq"][l], p["xa_wo"][l], p["ln_w"][l, 1], p["ln_b"][l, 1], tmx)
        xf, xb = _ffn(xf3.reshape(T, D_MODEL), xb3.reshape(T, D_MODEL), p["ffn_wa"][l], p["ffn_wb"][l],
                      p["ffn_w_out"][l], p["ln_w"][l, 2], p["ln_b"][l, 2], tm)
    return xf.reshape(B, S, D_MODEL)


def _prepare_params(in_ln_w, in_ln_b, w_in, b_gate, ret_decay_f, ret_decay_b, ret_gn_w, ret_wo, sg_ln_w, sg_ln_b,
                    sg_ws, sg_b, sg_wo, att_qn_w, att_kn_w, att_wo, w_out, ln_w, ln_b, xa_wq, xa_wkv, xa_wo,
                    ffn_w_in, ffn_w_out):
    vec = lambda a: a.astype(F32).reshape(a.shape[:-1] + (1, a.shape[-1]))
    n_gate = 3 * D_MODEL
    old_gate = D_IN - n_gate
    w_in_perm = jnp.concatenate([w_in[:, :, :COL_GATE], w_in[:, :, old_gate:], w_in[:, :, COL_GATE:old_gate]], axis=-1)
    sg_bias = jnp.repeat(jnp.swapaxes(sg_b.astype(F32), 1, 2), SG_GW, axis=-1)
    return dict(
        in_ln_w=vec(in_ln_w), in_ln_b=vec(in_ln_b),
        w_in=w_in_perm.astype(BF16), b_gate=vec(b_gate),
        dec=jnp.stack([_decay_tables(ret_decay_f[l], ret_decay_b[l]) for l in range(DEPTH)]),
        ret_gn_w=vec(ret_gn_w), ret_wo=ret_wo.astype(BF16),
        sg_ln_w=vec(sg_ln_w), sg_ln_b=vec(sg_ln_b), sg_ws=sg_ws.astype(BF16), sg_bias=sg_bias,
        sg_wo=sg_wo.astype(BF16),
        att_qn_w=vec(att_qn_w), att_kn_w=vec(att_kn_w), att_wo=att_wo.astype(BF16),
        w_out=w_out.astype(BF16), ln_w=vec(ln_w), ln_b=vec(ln_b),
        xa_wq=xa_wq.astype(BF16), xa_wkv=xa_wkv.astype(BF16), xa_wo=xa_wo.astype(BF16),
        ffn_wa=ffn_w_in[:, :, :D_FF].astype(BF16), ffn_wb=ffn_w_in[:, :, D_FF:].astype(BF16),
        ffn_w_out=ffn_w_out.astype(BF16),
    )


def kernel(x_prompt, x_sample, mem_prompt, mem_sample, in_ln_w, in_ln_b, w_in, b_gate, ret_decay_f, ret_decay_b,
           ret_gn_w, ret_wo, sg_ln_w, sg_ln_b, sg_ws, sg_b, sg_wo, att_qn_w, att_kn_w, att_wo, w_out, ln_w, ln_b,
           xa_wq, xa_wkv, xa_wo, ffn_w_in, ffn_w_out):
    p = _prepare_params(in_ln_w, in_ln_b, w_in, b_gate, ret_decay_f, ret_decay_b, ret_gn_w, ret_wo, sg_ln_w,
                        sg_ln_b, sg_ws, sg_b, sg_wo, att_qn_w, att_kn_w, att_wo, w_out, ln_w, ln_b, xa_wq, xa_wkv,
                        xa_wo, ffn_w_in, ffn_w_out)
    return (_encoder(x_prompt, mem_prompt, p), _encoder(x_sample, mem_sample, p))
```

```python
import functools
import math

import jax
import jax.numpy as jnp
from jax import lax
from jax.experimental import pallas as pl
from jax.experimental.pallas import tpu as pltpu

F32 = jnp.float32
BF16 = jnp.bfloat16

D_MODEL = 1024
DEPTH = 4
N_MEM = 256
GRID_W = 64
CHUNK = 128
RET_HEADS = 8
RET_DK = 128
SG_GROUPS = 4
SG_GW = 256
ATT_HEADS = 8
ATT_KV_HEADS = 2
ATT_GROUP = ATT_HEADS // ATT_KV_HEADS
ATT_HD = 128
X_HEADS = 4
X_HD = D_MODEL // X_HEADS
D_FF = 2816
FF_CHUNK = 256
RET_UNROLL = 8
RET_DOUBLE_BUFFER_MAX_BYTES = 1024 * 1024
ALPHA = (2 * DEPTH) ** 0.25
ROPE_BASE = 10000.0
LN_EPS = 1e-5
RMS_EPS = 1e-6
LOG2E = 1.4426950408889634
SCORE_BOUND_COEF = ATT_HD ** 0.5 * LOG2E
SAFE_SCORE_BOUND = 64.0

D_IN = 10752
COL_RQ, COL_RK, COL_RV, COL_RG = 0, 1024, 2048, 3072
COL_SU, COL_SV, COL_AQ = 4096, 5120, 6144
COL_GATE = 7168
COL_AK, COL_AV = 10240, 10496

VMEM_LIMIT = 52 * 1024 * 1024


def _cparams(sem):
    return pltpu.CompilerParams(dimension_semantics=sem, vmem_limit_bytes=VMEM_LIMIT)


def _const_spec(shape):
    nd = len(shape)
    return pl.BlockSpec(shape, lambda *_: (0,) * nd, pipeline_mode=pl.Buffered(1))


def _layer_norm_rows(z, w, b):
    mu = jnp.mean(z, axis=-1, keepdims=True)
    d = z - mu
    var = jnp.mean(d * d, axis=-1, keepdims=True)
    return d * lax.rsqrt(var + LN_EPS) * w + b


def _gelu_tanh(x):
    return 0.5 * x * (1.0 + jnp.tanh(0.7978845608028654 * (x + 0.044715 * (x * x * x))))


def _sigmoid(x):
    return 1.0 / (1.0 + jnp.exp(-x))


def _in_ln_kernel(x_ref, w_ref, b_ref, xo_ref, xb_ref):
    y = _layer_norm_rows(x_ref[...], w_ref[...], b_ref[...])
    xo_ref[...] = y
    xb_ref[...] = y.astype(BF16)


def _in_ln(x2, w, b, tm):
    T = x2.shape[0]
    row = pl.BlockSpec((tm, D_MODEL), lambda i: (i, 0))
    return pl.pallas_call(
        _in_ln_kernel,
        grid=(T // tm,),
        in_specs=[row, _const_spec((1, D_MODEL)), _const_spec((1, D_MODEL))],
        out_specs=[row, row],
        out_shape=[jax.ShapeDtypeStruct((T, D_MODEL), F32), jax.ShapeDtypeStruct((T, D_MODEL), BF16)],
        compiler_params=_cparams(("parallel",)),
        name="in_ln",
    )(x2, w, b)


def _mm_kernel(x_ref, w_ref, o_ref):
    o_ref[...] = jnp.dot(x_ref[...], w_ref[...], preferred_element_type=F32).astype(o_ref.dtype)


def _matmul(x, w, tm, tn, name):
    M, K = x.shape
    N = w.shape[1]
    return pl.pallas_call(
        _mm_kernel,
        grid=(N // tn, M // tm),
        in_specs=[pl.BlockSpec((tm, K), lambda j, i: (i, 0)), pl.BlockSpec((K, tn), lambda j, i: (0, j))],
        out_specs=pl.BlockSpec((tm, tn), lambda j, i: (i, j)),
        out_shape=jax.ShapeDtypeStruct((M, N), BF16),
        compiler_params=_cparams(("parallel", "parallel")),
        name=name,
    )(x, w)


def _rope(x, cos, sin_signed, partner):
    return x * cos + partner * sin_signed


def _prep_kernel(rq_ref, rk_ref, aq_ref, ak_ref, rc_ref, rs_ref, ac_ref, as_ref, qnw_ref, knw_ref,
                 rqo_ref, rkt_ref, aqo_ref, akt_ref, *, ts):
    rc = rc_ref[...]
    rs = rs_ref[...]
    ac = ac_ref[...]
    as_ = as_ref[...]
    lane = lax.broadcasted_iota(jnp.int32, (ts, ATT_HD), 1)
    low_half = (lane % 64) < 32

    def ret_rope(x):
        return _rope(x, rc, rs, pltpu.roll(x, 64, axis=1))

    def axial_rope(x):
        partner = jnp.where(low_half, pltpu.roll(x, 96, axis=1), pltpu.roll(x, 32, axis=1))
        return _rope(x, ac, as_, partner)

    def rms(x, w):
        return x * lax.rsqrt(jnp.mean(x * x, axis=-1, keepdims=True) + RMS_EPS) * w

    k_scale = RET_DK ** -0.5
    for h in range(RET_HEADS):
        sl = slice(h * RET_DK, (h + 1) * RET_DK)
        rqo_ref[0, :, sl] = ret_rope(rq_ref[0, :, sl].astype(F32)).astype(BF16)
        kt = (ret_rope(rk_ref[0, :, sl].astype(F32)) * k_scale).T
        for c in range(ts // CHUNK):
            rkt_ref[0, h, c] = kt[:, c * CHUNK:(c + 1) * CHUNK].astype(BF16)

    qw = qnw_ref[...] * (ATT_HD ** -0.5 * LOG2E)
    kw = knw_ref[...]
    for h in range(ATT_HEADS):
        sl = slice(h * ATT_HD, (h + 1) * ATT_HD)
        aqo_ref[0, h] = axial_rope(rms(aq_ref[0, :, sl].astype(F32), qw)).astype(BF16)
    for h in range(ATT_KV_HEADS):
        sl = slice(h * ATT_HD, (h + 1) * ATT_HD)
        akt_ref[0, h] = axial_rope(rms(ak_ref[0, :, sl].astype(F32), kw)).T.astype(BF16)


def _prep(h3, tabs, qnw, knw, ts):
    B, S, _ = h3.shape
    rc, rs, ac, as_ = tabs
    tab = pl.BlockSpec((ts, ATT_HD), lambda b, s: (s, 0))
    wide = lambda blk: pl.BlockSpec((1, ts, 1024), lambda b, s: (b, s, blk))
    return pl.pallas_call(
        functools.partial(_prep_kernel, ts=ts),
        grid=(B, S // ts),
        in_specs=[wide(COL_RQ // 1024), wide(COL_RK // 1024), wide(COL_AQ // 1024),
                  pl.BlockSpec((1, ts, 256), lambda b, s: (b, s, COL_AK // 256)),
                  tab, tab, tab, tab, _const_spec((1, ATT_HD)), _const_spec((1, ATT_HD))],
        out_specs=[wide(0),
                   pl.BlockSpec((1, RET_HEADS, ts // CHUNK, RET_DK, CHUNK), lambda b, s: (b, 0, s, 0, 0)),
                   pl.BlockSpec((1, ATT_HEADS, ts, ATT_HD), lambda b, s: (b, 0, s, 0)),
                   pl.BlockSpec((1, ATT_KV_HEADS, ATT_HD, ts), lambda b, s: (b, 0, 0, s))],
        out_shape=[jax.ShapeDtypeStruct((B, S, 1024), BF16),
                   jax.ShapeDtypeStruct((B, RET_HEADS, S // CHUNK, RET_DK, CHUNK), BF16),
                   jax.ShapeDtypeStruct((B, ATT_HEADS, S, ATT_HD), BF16),
                   jax.ShapeDtypeStruct((B, ATT_KV_HEADS, ATT_HD, S), BF16)],
        compiler_params=_cparams(("parallel", "parallel")),
        name="prep_qk",
    )(h3, h3, h3, h3, rc, rs, ac, as_, qnw, knw)


def _ret_kernel(q_ref, kt_ref, v_ref, g_ref, dec_ref, gnw_ref, o_ref, st_ref, p_ref, tot_ref, *, n_chunks, unroll):
    dtot = dec_ref[0, 0]
    xif, xib = dec_ref[0, 1], dec_ref[0, 2]
    zf, zb = dec_ref[0, 3], dec_ref[0, 4]
    cdf, cdb = dec_ref[0, 5], dec_ref[0, 6]
    gnw = gnw_ref[...]

    def rows_of(c):
        return pl.ds(pl.multiple_of(c * CHUNK, CHUNK), CHUNK)

    def scan(i, carry):
        sf, sb = carry
        cf, cb = i, n_chunks - 1 - i
        st_ref[cf, :, :RET_DK] = sf.astype(BF16)
        st_ref[cb, :, RET_DK:] = sb.astype(BF16)
        kzf = (kt_ref[0, 0, cf].astype(F32) * zf).astype(BF16)
        kzb = (kt_ref[0, 0, cb].astype(F32) * zb).astype(BF16)
        sf = sf * cdf + jnp.dot(kzf, v_ref[0, rows_of(cf), :], preferred_element_type=F32)
        sb = sb * cdb + jnp.dot(kzb, v_ref[0, rows_of(cb), :], preferred_element_type=F32)
        return sf, sb

    zero = jnp.zeros((RET_DK, RET_DK), F32)
    lax.fori_loop(0, n_chunks, scan, (zero, zero), unroll=unroll)

    def decayed_scores(c, carry):
        s = jnp.dot(q_ref[0, rows_of(c), :], kt_ref[0, 0, c], preferred_element_type=F32)
        p_ref[c] = (s * dtot).astype(BF16)
        return carry

    lax.fori_loop(0, n_chunks, decayed_scores, 0, unroll=unroll)

    def mix(c, carry):
        rows = rows_of(c)
        inter = jnp.dot(q_ref[0, rows, :], st_ref[c], preferred_element_type=F32)
        tot_ref[rows, :] = (jnp.dot(p_ref[c], v_ref[0, rows, :], preferred_element_type=F32)
                            + inter[:, :RET_DK] * xif + inter[:, RET_DK:] * xib)
        return carry

    lax.fori_loop(0, n_chunks, mix, 0, unroll=unroll)

    def norm_gate(c, carry):
        rows = rows_of(c)
        tot = tot_ref[rows, :]
        mu = jnp.mean(tot, axis=-1, keepdims=True)
        var = jnp.maximum(jnp.mean(tot * tot, axis=-1, keepdims=True) - mu * mu, 0.0)
        y = (tot - mu) * lax.rsqrt(var + LN_EPS) * gnw
        g = g_ref[0, rows, :].astype(F32)
        o_ref[0, rows, :] = (y * (g * _sigmoid(g))).astype(BF16)
        return carry

    lax.fori_loop(0, n_chunks, norm_gate, 0, unroll=unroll)


def _retention(rq, rkt, h3, dec, gnw):
    B, S, _ = rq.shape
    n = S // CHUNK
    mode = pl.Buffered(1) if S * RET_DK * 2 > RET_DOUBLE_BUFFER_MAX_BYTES else None
    head = lambda base, pm=None: pl.BlockSpec((1, S, RET_DK), lambda b, h: (b, 0, base + h), pipeline_mode=pm)
    return pl.pallas_call(
        functools.partial(_ret_kernel, n_chunks=n, unroll=min(RET_UNROLL, n)),
        grid=(B, RET_HEADS),
        in_specs=[head(0, mode),
                  pl.BlockSpec((1, 1, n, RET_DK, CHUNK), lambda b, h: (b, h, 0, 0, 0), pipeline_mode=mode),
                  head(COL_RV // RET_DK, mode), head(COL_RG // RET_DK, mode),
                  pl.BlockSpec((1, 7, CHUNK, CHUNK), lambda b, h: (h, 0, 0, 0)),
                  pl.BlockSpec((1, RET_DK), lambda b, h: (0, h))],
        out_specs=head(0),
        out_shape=jax.ShapeDtypeStruct((B, S, 1024), BF16),
        scratch_shapes=[pltpu.VMEM((n, RET_DK, 2 * RET_DK), BF16), pltpu.VMEM((n, CHUNK, CHUNK), BF16),
                        pltpu.VMEM((S, RET_DK), F32)],
        compiler_params=_cparams(("parallel", "parallel")),
        name="retention",
    )(rq, rkt, h3, h3, dec, gnw)


def _sg_kernel(u_ref, v_ref, lnw_ref, lnb_ref, ws_ref, bias_ref, o_ref, *, ts):
    lnw = lnw_ref[...]
    lnb = lnb_ref[...]
    for c in range(ts // CHUNK):
        rows = slice(c * CHUNK, (c + 1) * CHUNK)
        vn = _layer_norm_rows(_gelu_tanh(v_ref[0, rows, :].astype(F32)), lnw, lnb).astype(BF16)
        u = _gelu_tanh(u_ref[0, rows, :].astype(F32))
        for g in range(SG_GROUPS):
            cols = slice(g * SG_GW, (g + 1) * SG_GW)
            mixed = jnp.dot(ws_ref[g], vn[:, cols], preferred_element_type=F32) + bias_ref[:, cols]
            o_ref[0, rows, cols] = (u[:, cols] * mixed).astype(BF16)


def _spatial_gate(h3, lnw, lnb, ws, bias, ts):
    B, S, _ = h3.shape
    wide = lambda blk: pl.BlockSpec((1, ts, 1024), lambda b, s: (b, s, blk))
    return pl.pallas_call(
        functools.partial(_sg_kernel, ts=ts),
        grid=(B, S // ts),
        in_specs=[wide(COL_SU // 1024), wide(COL_SV // 1024), _const_spec((1, 1024)), _const_spec((1, 1024)),
                  _const_spec((SG_GROUPS, CHUNK, CHUNK)), _const_spec((CHUNK, 1024))],
        out_specs=wide(0),
        out_shape=jax.ShapeDtypeStruct((B, S, 1024), BF16),
        compiler_params=_cparams(("parallel", "parallel")),
        name="spatial_gate",
    )(h3, h3, lnw, lnb, ws, bias)


def _attn_kernel(bounded_ref, q_ref, kt_ref, v_ref, o_ref, m_ref, acc_ref, *, tq, tk):
    ki = pl.program_id(3)

    @pl.when(ki == 0)
    def _():
        m_ref[...] = jnp.full(m_ref.shape, -jnp.inf, F32)
        acc_ref[...] = jnp.zeros(acc_ref.shape, F32)

    q = q_ref[0].reshape(ATT_GROUP * tq, ATT_HD)
    kt = kt_ref[0, 0]
    v1 = jnp.concatenate([v_ref[0], jnp.ones((tk, ATT_HD), BF16)], axis=1)
    bounded = bounded_ref[0] == 1

    @pl.when(bounded)
    def _():
        p = jnp.exp2(jnp.dot(q, kt, preferred_element_type=F32))
        acc_ref[...] += jnp.dot(p.astype(BF16), v1, preferred_element_type=F32)

    @pl.when(jnp.logical_not(bounded))
    def _():
        s = jnp.dot(q, kt, preferred_element_type=F32)
        m_prev = m_ref[...]
        m_next = jnp.maximum(m_prev, jnp.max(s, axis=1, keepdims=True))
        p = jnp.exp2(s - jnp.tile(m_next, (1, tk // ATT_HD)))
        alpha = jnp.exp2(m_prev - m_next)
        acc_ref[...] = jnp.tile(alpha, (1, 2)) * acc_ref[...] + jnp.dot(p.astype(BF16), v1, preferred_element_type=F32)
        m_ref[...] = m_next

    @pl.when(ki == pl.num_programs(3) - 1)
    def _():
        o = acc_ref[:, :ATT_HD] / acc_ref[:, ATT_HD:]
        for g in range(ATT_GROUP):
            o_ref[0, :, g * ATT_HD:(g + 1) * ATT_HD] = o[g * tq:(g + 1) * tq].astype(BF16)


def _attention(bounded, aq, akt, h3, tq, tk):
    B, _, S, _ = aq.shape
    gw = ATT_GROUP * ATT_HD
    return pl.pallas_call(
        functools.partial(_attn_kernel, tq=tq, tk=tk),
        grid=(B, ATT_KV_HEADS, S // tq, S // tk),
        in_specs=[pl.BlockSpec(memory_space=pltpu.SMEM),
                  pl.BlockSpec((1, ATT_GROUP, tq, ATT_HD), lambda b, k, i, j: (b, k, i, 0)),
                  pl.BlockSpec((1, 1, ATT_HD, tk), lambda b, k, i, j: (b, k, 0, j)),
                  pl.BlockSpec((1, tk, ATT_HD), lambda b, k, i, j: (b, j, COL_AV // ATT_HD + k))],
        out_specs=pl.BlockSpec((1, tq, gw), lambda b, k, i, j: (b, i, k)),
        out_shape=jax.ShapeDtypeStruct((B, S, 1024), BF16),
        scratch_shapes=[pltpu.VMEM((ATT_GROUP * tq, ATT_HD), F32), pltpu.VMEM((ATT_GROUP * tq, 2 * ATT_HD), F32)],
        compiler_params=_cparams(("parallel", "parallel", "parallel", "arbitrary")),
        name="gqa_attention",
    )(bounded, aq, akt, h3)


def _merge_kernel(ro_ref, sg_ref, at_ref, g0_ref, g1_ref, g2_ref, bg_ref, x_ref,
                  wr_ref, ws_ref, wa_ref, wo_ref, lnw_ref, lnb_ref, xo_ref, xb_ref):
    def branch(a_ref, w_ref, gl_ref, i):
        gate = _sigmoid(gl_ref[...].astype(F32) + bg_ref[:, i * D_MODEL:(i + 1) * D_MODEL])
        return gate * jnp.dot(a_ref[...], w_ref[...], preferred_element_type=F32)

    merged = branch(ro_ref, wr_ref, g0_ref, 0) + branch(sg_ref, ws_ref, g1_ref, 1) + branch(at_ref, wa_ref, g2_ref, 2)
    y = jnp.dot(merged.astype(BF16), wo_ref[...], preferred_element_type=F32)
    out = _layer_norm_rows(ALPHA * x_ref[...] + y, lnw_ref[...], lnb_ref[...])
    xo_ref[...] = out
    xb_ref[...] = out.astype(BF16)


def _merge(ro, sg, at, h2, bg, x, wr, ws, wa, wo, lnw, lnb, tm):
    T = x.shape[0]
    row = pl.BlockSpec((tm, D_MODEL), lambda i: (i, 0))
    gate = lambda k: pl.BlockSpec((tm, D_MODEL), lambda i: (i, COL_GATE // D_MODEL + k))
    wspec = _const_spec((D_MODEL, D_MODEL))
    vec = _const_spec((1, D_MODEL))
    return pl.pallas_call(
        _merge_kernel,
        grid=(T // tm,),
        in_specs=[row, row, row, gate(0), gate(1), gate(2), _const_spec((1, 3 * D_MODEL)), row,
                  wspec, wspec, wspec, wspec, vec, vec],
        out_specs=[row, row],
        out_shape=[jax.ShapeDtypeStruct((T, D_MODEL), F32), jax.ShapeDtypeStruct((T, D_MODEL), BF16)],
        compiler_params=_cparams(("parallel",)),
        name="merge_out_ln",
    )(ro, sg, at, h2, h2, h2, bg, x, wr, ws, wa, wo, lnw, lnb)


def _cross_kernel(x_ref, xb_ref, kv_ref, wq_ref, wo_ref, lnw_ref, lnb_ref, xo_ref, xbo_ref, o_scr):
    q = jnp.dot(xb_ref[0], wq_ref[...], preferred_element_type=F32).astype(BF16)
    scale = X_HD ** -0.5
    for h in range(X_HEADS):
        cols = slice(h * X_HD, (h + 1) * X_HD)
        k = kv_ref[0, :, cols]
        v = kv_ref[0, :, D_MODEL + h * X_HD:D_MODEL + (h + 1) * X_HD]
        s = lax.dot_general(q[:, cols], k, (((1,), (1,)), ((), ())), preferred_element_type=F32) * scale
        p = jnp.exp(s - jnp.max(s, axis=-1, keepdims=True))
        o = jnp.dot(p.astype(BF16), v, preferred_element_type=F32) / jnp.sum(p, axis=-1, keepdims=True)
        o_scr[:, cols] = o.astype(BF16)
    y = jnp.dot(o_scr[...], wo_ref[...], preferred_element_type=F32)
    out = _layer_norm_rows(ALPHA * x_ref[0] + y, lnw_ref[...], lnb_ref[...])
    xo_ref[0] = out
    xbo_ref[0] = out.astype(BF16)


def _cross(x3, xb3, kv3, wq, wo, lnw, lnb, tm):
    B, S, _ = x3.shape
    row = pl.BlockSpec((1, tm, D_MODEL), lambda b, i: (b, i, 0))
    wspec = _const_spec((D_MODEL, D_MODEL))
    vec = _const_spec((1, D_MODEL))
    return pl.pallas_call(
        _cross_kernel,
        grid=(B, S // tm),
        in_specs=[row, row, pl.BlockSpec((1, N_MEM, 2 * D_MODEL), lambda b, i: (b, 0, 0)), wspec, wspec, vec, vec],
        out_specs=[row, row],
        out_shape=[jax.ShapeDtypeStruct((B, S, D_MODEL), F32), jax.ShapeDtypeStruct((B, S, D_MODEL), BF16)],
        scratch_shapes=[pltpu.VMEM((tm, D_MODEL), BF16)],
        compiler_params=_cparams(("parallel", "parallel")),
        name="cross_attn_ln",
    )(x3, xb3, kv3, wq, wo, lnw, lnb)


def _ffn_kernel(x_ref, xb_ref, wa_ref, wb_ref, wo_ref, lnw_ref, lnb_ref, xo_ref, xbo_ref, h_scr):
    xb = xb_ref[...]
    for j in range(D_FF // FF_CHUNK):
        cols = slice(j * FF_CHUNK, (j + 1) * FF_CHUNK)
        a = jnp.dot(xb, wa_ref[:, cols], preferred_element_type=F32)
        b = jnp.dot(xb, wb_ref[:, cols], preferred_element_type=F32)
        h_scr[:, cols] = (a * _sigmoid(a) * b).astype(BF16)
    y = jnp.dot(h_scr[...], wo_ref[...], preferred_element_type=F32)
    out = _layer_norm_rows(ALPHA * x_ref[...] + y, lnw_ref[...], lnb_ref[...])
    xo_ref[...] = out
    xbo_ref[...] = out.astype(BF16)


def _ffn(x, xb, wa, wb, wo, lnw, lnb, tm):
    T = x.shape[0]
    row = pl.BlockSpec((tm, D_MODEL), lambda i: (i, 0))
    vec = _const_spec((1, D_MODEL))
    return pl.pallas_call(
        _ffn_kernel,
        grid=(T // tm,),
        in_specs=[row, row, _const_spec((D_MODEL, D_FF)), _const_spec((D_MODEL, D_FF)),
                  _const_spec((D_FF, D_MODEL)), vec, vec],
        out_specs=[row, row],
        out_shape=[jax.ShapeDtypeStruct((T, D_MODEL), F32), jax.ShapeDtypeStruct((T, D_MODEL), BF16)],
        scratch_shapes=[pltpu.VMEM((tm, D_FF), BF16)],
        compiler_params=_cparams(("parallel",)),
        name="swiglu_ln",
    )(x, xb, wa, wb, wo, lnw, lnb)


def _rope_tables(S):
    t = jnp.arange(S)

    def cos_sin(pos, dim):
        inv_freq = ROPE_BASE ** (-jnp.arange(0, dim, 2, dtype=F32) / dim)
        ang = pos.astype(F32)[:, None] * inv_freq[None, :]
        return jnp.cos(ang), jnp.sin(ang)

    ct, st = cos_sin(t, RET_DK)
    cr, sr = cos_sin(t // GRID_W, ATT_HD // 2)
    cc, sc = cos_sin(t % GRID_W, ATT_HD // 2)
    return (jnp.concatenate([ct, ct], -1), jnp.concatenate([-st, st], -1),
            jnp.concatenate([cr, cr, cc, cc], -1), jnp.concatenate([-sr, sr, -sc, sc], -1))


def _decay_tables(decay_f, decay_b):
    lgf = jax.nn.log_sigmoid(decay_f.astype(F32))[:, None, None]
    lgb = jax.nn.log_sigmoid(decay_b.astype(F32))[:, None, None]
    idx = jnp.arange(CHUNK, dtype=F32)
    diff = idx[:, None] - idx[None, :]
    dtot = jnp.where(diff >= 0, jnp.exp(lgf * jnp.maximum(diff, 0.0)), jnp.exp(lgb * jnp.maximum(-diff, 0.0)))
    ones = jnp.ones((CHUNK, CHUNK), F32)
    row = idx[None, :, None] * ones
    lane = idx[None, None, :] * ones
    xif = jnp.exp(lgf * (row + 1.0))
    xib = jnp.exp(lgb * (CHUNK - row))
    zf = jnp.exp(lgf * (CHUNK - 1.0 - lane))
    zb = jnp.exp(lgb * lane)
    cdf = jnp.exp(lgf * CHUNK) * ones
    cdb = jnp.exp(lgb * CHUNK) * ones
    return jnp.stack([dtot, xif, xib, zf, zb, cdf, cdb], axis=1)


def _tile(n, pref):
    t = min(n, pref)
    assert n % t == 0, (n, t)
    return t


def _encoder(x, mem, p):
    B, S, _ = x.shape
    T = B * S
    tm = _tile(T, 512)
    ts = _tile(S, 512)
    tabs = _rope_tables(S)
    xf, xb = _in_ln(x.reshape(T, D_MODEL), p["in_ln_w"], p["in_ln_b"], tm)
    memb = mem.astype(BF16).reshape(B * N_MEM, D_MODEL)
    for l in range(DEPTH):
        h2 = _matmul(xb, p["w_in"][l], _tile(T, 1024), 1536, "proj_in")
        h3 = h2.reshape(B, S, D_IN)
        rq, rkt, aq, akt = _prep(h3, tabs, p["att_qn_w"][l], p["att_kn_w"][l], ts)
        ro = _retention(rq, rkt, h3, p["dec"][l], p["ret_gn_w"][l])
        sg = _spatial_gate(h3, p["sg_ln_w"][l], p["sg_ln_b"][l], p["sg_ws"][l], p["sg_bias"][l], ts)
        at = _attention(p["att_bounded"][l], aq, akt, h3, _tile(S, 512), _tile(S, 1024))
        xf, xb = _merge(ro.reshape(T, D_MODEL), sg.reshape(T, D_MODEL), at.reshape(T, D_MODEL), h2,
                        p["b_gate"][l], xf, p["ret_wo"][l], p["sg_wo"][l], p["att_wo"][l], p["w_out"][l],
                        p["ln_w"][l, 0], p["ln_b"][l, 0], tm)
        kv = _matmul(memb, p["xa_wkv"][l], _tile(B * N_MEM, 1024), 1024, "proj_kv")
        tmx = _tile(S, 512)
        xf3, xb3 = _cross(xf.reshape(B, S, D_MODEL), xb.reshape(B, S, D_MODEL), kv.reshape(B, N_MEM, 2 * D_MODEL),
                          p["xa_wq"][l], p["xa_wo"][l], p["ln_w"][l, 1], p["ln_b"][l, 1], tmx)
        xf, xb = _ffn(xf3.reshape(T, D_MODEL), xb3.reshape(T, D_MODEL), p["ffn_wa"][l], p["ffn_wb"][l],
                      p["ffn_w_out"][l], p["ln_w"][l, 2], p["ln_b"][l, 2], tm)
    return xf.reshape(B, S, D_MODEL)


def _prepare_params(in_ln_w, in_ln_b, w_in, b_gate, ret_decay_f, ret_decay_b, ret_gn_w, ret_wo, sg_ln_w, sg_ln_b,
                    sg_ws, sg_b, sg_wo, att_qn_w, att_kn_w, att_wo, w_out, ln_w, ln_b, xa_wq, xa_wkv, xa_wo,
                    ffn_w_in, ffn_w_out):
    vec = lambda a: a.astype(F32).reshape(a.shape[:-1] + (1, a.shape[-1]))
    n_gate = 3 * D_MODEL
    old_gate = D_IN - n_gate
    w_in_perm = jnp.concatenate([w_in[:, :, :COL_GATE], w_in[:, :, old_gate:], w_in[:, :, COL_GATE:old_gate]], axis=-1)
    sg_bias = jnp.repeat(jnp.swapaxes(sg_b.astype(F32), 1, 2), SG_GW, axis=-1)
    score_bound = (SCORE_BOUND_COEF * jnp.max(jnp.abs(att_qn_w.astype(F32)), axis=-1)
                   * jnp.max(jnp.abs(att_kn_w.astype(F32)), axis=-1))
    return dict(
        att_bounded=(score_bound <= SAFE_SCORE_BOUND).astype(jnp.int32).reshape(DEPTH, 1),
        in_ln_w=vec(in_ln_w), in_ln_b=vec(in_ln_b),
        w_in=w_in_perm.astype(BF16), b_gate=vec(b_gate),
        dec=jnp.stack([_decay_tables(ret_decay_f[l], ret_decay_b[l]) for l in range(DEPTH)]),
        ret_gn_w=vec(ret_gn_w), ret_wo=ret_wo.astype(BF16),
        sg_ln_w=vec(sg_ln_w), sg_ln_b=vec(sg_ln_b), sg_ws=sg_ws.astype(BF16), sg_bias=sg_bias,
        sg_wo=sg_wo.astype(BF16),
        att_qn_w=vec(att_qn_w), att_kn_w=vec(att_kn_w), att_wo=att_wo.astype(BF16),
        w_out=w_out.astype(BF16), ln_w=vec(ln_w), ln_b=vec(ln_b),
        xa_wq=xa_wq.astype(BF16), xa_wkv=xa_wkv.astype(BF16), xa_wo=xa_wo.astype(BF16),
        ffn_wa=ffn_w_in[:, :, :D_FF].astype(BF16), ffn_wb=ffn_w_in[:, :, D_FF:].astype(BF16),
        ffn_w_out=ffn_w_out.astype(BF16),
    )


def kernel(x_prompt, x_sample, mem_prompt, mem_sample, in_ln_w, in_ln_b, w_in, b_gate, ret_decay_f, ret_decay_b,
           ret_gn_w, ret_wo, sg_ln_w, sg_ln_b, sg_ws, sg_b, sg_wo, att_qn_w, att_kn_w, att_wo, w_out, ln_w, ln_b,
           xa_wq, xa_wkv, xa_wo, ffn_w_in, ffn_w_out):
    p = _prepare_params(in_ln_w, in_ln_b, w_in, b_gate, ret_decay_f, ret_decay_b, ret_gn_w, ret_wo, sg_ln_w,
                        sg_ln_b, sg_ws, sg_b, sg_wo, att_qn_w, att_kn_w, att_wo, w_out, ln_w, ln_b, xa_wq, xa_wkv,
                        xa_wo, ffn_w_in, ffn_w_out)
    return (_encoder(x_prompt, mem_prompt, p), _encoder(x_sample, mem_sample, p))
```

```python
import functools

import jax
import jax.numpy as jnp
from jax import lax
from jax.experimental import pallas as pl
from jax.experimental.pallas import tpu as pltpu

F32 = jnp.float32
BF16 = jnp.bfloat16

D_MODEL = 1024
DEPTH = 4
N_MEM = 256
GRID_W = 64
CHUNK = 128
RET_HEADS = 8
RET_DK = 128
RET_W = RET_HEADS * RET_DK
SG_GROUPS = 4
SG_GW = 256
SG_W = SG_GROUPS * SG_GW
ATT_HEADS = 8
ATT_KV_HEADS = 2
ATT_GROUP = ATT_HEADS // ATT_KV_HEADS
ATT_HD = 128
ATT_W = ATT_HEADS * ATT_HD
ATT_KV_W = ATT_KV_HEADS * ATT_HD
X_HEADS = 4
X_HD = D_MODEL // X_HEADS
D_FF = 2816
N_BRANCH = 3
ALPHA = (2 * DEPTH) ** 0.25
ROPE_BASE = 10000.0
LN_EPS = 1e-5
RMS_EPS = 1e-6
LOG2E = 1.4426950408889634

COLS_RET = (0, 2 * RET_W)
COLS_MID = (COLS_RET[1], COLS_RET[1] + 2 * RET_W + 2 * SG_W)
COLS_ATT = (COLS_MID[1], COLS_MID[1] + ATT_W + 2 * ATT_KV_W)
COLS_GATE = (COLS_ATT[1], COLS_ATT[1] + N_BRANCH * D_MODEL)

MXU_W = 256
FF_CHUNK = MXU_W
TOKEN_TILE = 512
ATT_TQ = 512
ATT_TK = 1024
ATT_SINGLE_KV_MAX = 2048
RET_UNROLL = 8
RET_DOUBLE_BUFFER_MAX_BYTES = 1024 * 1024
SCORE_BOUND_COEF = ATT_HD ** 0.5 * LOG2E
SAFE_SCORE_BOUND = 64.0
VMEM_LIMIT = 52 * 1024 * 1024


def _cparams(sem):
    return pltpu.CompilerParams(dimension_semantics=sem, vmem_limit_bytes=VMEM_LIMIT)


def _const_spec(shape):
    nd = len(shape)
    return pl.BlockSpec(shape, lambda *_: (0,) * nd, pipeline_mode=pl.Buffered(1))


def _tile(n, pref):
    t = min(n, pref)
    assert n % t == 0, (n, t)
    return t


def _layer_norm_rows(z, w, b):
    mu = jnp.mean(z, axis=-1, keepdims=True)
    d = z - mu
    var = jnp.mean(d * d, axis=-1, keepdims=True)
    return d * lax.rsqrt(var + LN_EPS) * w + b


def _gelu_tanh(x):
    return 0.5 * x * (1.0 + jnp.tanh(0.7978845608028654 * (x + 0.044715 * (x * x * x))))


def _sigmoid(x):
    return 1.0 / (1.0 + jnp.exp(-x))


def _dot(a, b):
    return jnp.dot(a, b, preferred_element_type=F32)


def _in_ln_kernel(x_ref, w_ref, b_ref, xo_ref, xb_ref):
    y = _layer_norm_rows(x_ref[...], w_ref[...], b_ref[...])
    xo_ref[...] = y
    xb_ref[...] = y.astype(BF16)


def _in_ln(x2, w, b, tm):
    T = x2.shape[0]
    row = pl.BlockSpec((tm, D_MODEL), lambda i: (i, 0))
    return pl.pallas_call(
        _in_ln_kernel,
        grid=(T // tm,),
        in_specs=[row, _const_spec((1, D_MODEL)), _const_spec((1, D_MODEL))],
        out_specs=[row, row],
        out_shape=[jax.ShapeDtypeStruct((T, D_MODEL), F32), jax.ShapeDtypeStruct((T, D_MODEL), BF16)],
        compiler_params=_cparams(("parallel",)),
        name="in_ln",
    )(x2, w, b)


def _mm_kernel(x_ref, w_ref, o_ref):
    o_ref[...] = _dot(x_ref[...], w_ref[...]).astype(o_ref.dtype)


def _matmul(x, w, tm, tn, name):
    M, K = x.shape
    N = w.shape[1]
    return pl.pallas_call(
        _mm_kernel,
        grid=(N // tn, M // tm),
        in_specs=[pl.BlockSpec((tm, K), lambda j, i: (i, 0)), pl.BlockSpec((K, tn), lambda j, i: (0, j))],
        out_specs=pl.BlockSpec((tm, tn), lambda j, i: (i, j)),
        out_shape=jax.ShapeDtypeStruct((M, N), BF16),
        compiler_params=_cparams(("parallel", "parallel")),
        name=name,
    )(x, w)


def _rope(x, cos, sin_signed, partner):
    return x * cos + partner * sin_signed


def _proj_ret_kernel(x_ref, w_ref, rc_ref, rs_ref, rq_ref, rkt_ref, *, tm):
    x = x_ref[0]
    rc = rc_ref[...]
    rs = rs_ref[...]
    k_scale = RET_DK ** -0.5

    def rope(t):
        return _rope(t, rc, rs, pltpu.roll(t, RET_DK // 2, axis=1))

    heads_per_slab = MXU_W // RET_DK
    for slab in range(RET_W // MXU_W):
        q2 = _dot(x, w_ref[:, slab * MXU_W:(slab + 1) * MXU_W])
        k2 = _dot(x, w_ref[:, RET_W + slab * MXU_W:RET_W + (slab + 1) * MXU_W])
        for j in range(heads_per_slab):
            h = slab * heads_per_slab + j
            sl = slice(j * RET_DK, (j + 1) * RET_DK)
            rq_ref[0, :, h * RET_DK:(h + 1) * RET_DK] = rope(q2[:, sl]).astype(BF16)
            kt = (rope(k2[:, sl]) * k_scale).T
            for c in range(tm // CHUNK):
                rkt_ref[0, h, c] = kt[:, c * CHUNK:(c + 1) * CHUNK].astype(BF16)


def _proj_ret(xb3, w, rc, rs, tm):
    B, S, _ = xb3.shape
    tab = pl.BlockSpec((tm, RET_DK), lambda b, s: (s, 0))
    return pl.pallas_call(
        functools.partial(_proj_ret_kernel, tm=tm),
        grid=(B, S // tm),
        in_specs=[pl.BlockSpec((1, tm, D_MODEL), lambda b, s: (b, s, 0)), _const_spec((D_MODEL, 2 * RET_W)), tab, tab],
        out_specs=[pl.BlockSpec((1, tm, RET_W), lambda b, s: (b, s, 0)),
                   pl.BlockSpec((1, RET_HEADS, tm // CHUNK, RET_DK, CHUNK), lambda b, s: (b, 0, s, 0, 0))],
        out_shape=[jax.ShapeDtypeStruct((B, S, RET_W), BF16),
                   jax.ShapeDtypeStruct((B, RET_HEADS, S // CHUNK, RET_DK, CHUNK), BF16)],
        compiler_params=_cparams(("parallel", "parallel")),
        name="proj_ret",
    )(xb3, w, rc, rs)


def _proj_mid_kernel(x_ref, w_ref, lnw_ref, lnb_ref, ws_ref, bias_ref, rv_ref, rgs_ref, sg_ref, vn_ref, *, tm):
    x = x_ref[0]
    rv_ref[0] = _dot(x, w_ref[:, 0:RET_W]).astype(BF16)
    g = _dot(x, w_ref[:, RET_W:2 * RET_W])
    rgs_ref[0] = (g * _sigmoid(g)).astype(BF16)
    sv = _gelu_tanh(_dot(x, w_ref[:, 2 * RET_W + SG_W:2 * RET_W + 2 * SG_W]))
    vn_ref[...] = _layer_norm_rows(sv, lnw_ref[...], lnb_ref[...]).astype(BF16)
    for grp in range(SG_GROUPS):
        cols = slice(grp * SG_GW, (grp + 1) * SG_GW)
        u = _gelu_tanh(_dot(x, w_ref[:, 2 * RET_W + grp * SG_GW:2 * RET_W + (grp + 1) * SG_GW]))
        for c in range(tm // CHUNK):
            rows = slice(c * CHUNK, (c + 1) * CHUNK)
            mixed = _dot(ws_ref[grp], vn_ref[rows, cols]) + bias_ref[:, cols]
            sg_ref[0, rows, cols] = (u[rows, :] * mixed).astype(BF16)


def _proj_mid(xb3, w, lnw, lnb, ws, bias, tm):
    B, S, _ = xb3.shape
    row = pl.BlockSpec((1, tm, D_MODEL), lambda b, s: (b, s, 0))
    out = jax.ShapeDtypeStruct((B, S, D_MODEL), BF16)
    return pl.pallas_call(
        functools.partial(_proj_mid_kernel, tm=tm),
        grid=(B, S // tm),
        in_specs=[row, _const_spec((D_MODEL, 2 * RET_W + 2 * SG_W)), _const_spec((1, SG_W)), _const_spec((1, SG_W)),
                  _const_spec((SG_GROUPS, CHUNK, CHUNK)), _const_spec((CHUNK, SG_W))],
        out_specs=[row, row, row],
        out_shape=[out, out, out],
        scratch_shapes=[pltpu.VMEM((tm, SG_W), BF16)],
        compiler_params=_cparams(("parallel", "parallel")),
        name="proj_mid",
    )(xb3, w, lnw, lnb, ws, bias)


def _proj_att_kernel(x_ref, w_ref, ac_ref, as_ref, qnw_ref, knw_ref, aq_ref, akt_ref, av_ref, *, tm):
    x = x_ref[0]
    q_scale = ATT_HD ** -0.5 * LOG2E
    q_cos = ac_ref[...] * (qnw_ref[0:1, :] * q_scale)
    q_sin = as_ref[...] * (qnw_ref[1:2, :] * q_scale)
    k_cos = ac_ref[...] * knw_ref[0:1, :]
    k_sin = as_ref[...] * knw_ref[1:2, :]

    def norm_rope(t, cos, sin):
        r = lax.rsqrt(jnp.mean(t * t, axis=-1, keepdims=True) + RMS_EPS)
        return r * (t * cos + pltpu.roll(t, ATT_HD // 2, axis=1) * sin)

    heads_per_slab = MXU_W // ATT_HD
    for slab in range(ATT_W // MXU_W):
        q2 = _dot(x, w_ref[:, slab * MXU_W:(slab + 1) * MXU_W])
        for j in range(heads_per_slab):
            for c in range(tm // CHUNK):
                rows = slice(c * CHUNK, (c + 1) * CHUNK)
                t = q2[rows, j * ATT_HD:(j + 1) * ATT_HD]
                aq_ref[0, slab * heads_per_slab + j, rows, :] = norm_rope(t, q_cos[rows], q_sin[rows]).astype(BF16)
    k2 = _dot(x, w_ref[:, ATT_W:ATT_W + ATT_KV_W])
    for j in range(ATT_KV_HEADS):
        for c in range(tm // CHUNK):
            rows = slice(c * CHUNK, (c + 1) * CHUNK)
            t = k2[rows, j * ATT_HD:(j + 1) * ATT_HD]
            akt_ref[0, j, :, rows] = norm_rope(t, k_cos[rows], k_sin[rows]).T.astype(BF16)
    av_ref[0] = _dot(x, w_ref[:, ATT_W + ATT_KV_W:ATT_W + 2 * ATT_KV_W]).astype(BF16)


def _proj_att(xb3, w, ac, as_, qnw, knw, tm):
    B, S, _ = xb3.shape
    tab = pl.BlockSpec((tm, ATT_HD), lambda b, s: (s, 0))
    return pl.pallas_call(
        functools.partial(_proj_att_kernel, tm=tm),
        grid=(B, S // tm),
        in_specs=[pl.BlockSpec((1, tm, D_MODEL), lambda b, s: (b, s, 0)), _const_spec((D_MODEL, ATT_W + 2 * ATT_KV_W)),
                  tab, tab, _const_spec((2, ATT_HD)), _const_spec((2, ATT_HD))],
        out_specs=[pl.BlockSpec((1, ATT_HEADS, tm, ATT_HD), lambda b, s: (b, 0, s, 0)),
                   pl.BlockSpec((1, ATT_KV_HEADS, ATT_HD, tm), lambda b, s: (b, 0, 0, s)),
                   pl.BlockSpec((1, tm, ATT_KV_W), lambda b, s: (b, s, 0))],
        out_shape=[jax.ShapeDtypeStruct((B, ATT_HEADS, S, ATT_HD), BF16),
                   jax.ShapeDtypeStruct((B, ATT_KV_HEADS, ATT_HD, S), BF16),
                   jax.ShapeDtypeStruct((B, S, ATT_KV_W), BF16)],
        compiler_params=_cparams(("parallel", "parallel")),
        name="proj_att",
    )(xb3, w, ac, as_, qnw, knw)


def _proj_gate_kernel(x_ref, w_ref, b_ref, o_ref):
    x = x_ref[...]
    for i in range(N_BRANCH):
        cols = slice(i * D_MODEL, (i + 1) * D_MODEL)
        o_ref[:, cols] = _sigmoid(_dot(x, w_ref[:, cols]) + b_ref[:, cols]).astype(BF16)


def _proj_gate(xb, w, b, tm):
    T = xb.shape[0]
    n = N_BRANCH * D_MODEL
    return pl.pallas_call(
        _proj_gate_kernel,
        grid=(T // tm,),
        in_specs=[pl.BlockSpec((tm, D_MODEL), lambda i: (i, 0)), _const_spec((D_MODEL, n)), _const_spec((1, n))],
        out_specs=pl.BlockSpec((tm, n), lambda i: (i, 0)),
        out_shape=jax.ShapeDtypeStruct((T, n), BF16),
        compiler_params=_cparams(("parallel",)),
        name="proj_gate",
    )(xb, w, b)


def _ret_kernel(q_ref, kt_ref, v_ref, g_ref, dec_ref, gnw_ref, o_ref, st_ref, p_ref, tot_ref, *, n_chunks, unroll):
    dtot = dec_ref[0, 0]
    xif, xib = dec_ref[0, 1], dec_ref[0, 2]
    zf, zb = dec_ref[0, 3], dec_ref[0, 4]
    cdf, cdb = dec_ref[0, 5], dec_ref[0, 6]
    gnw = gnw_ref[...]

    def rows_of(c):
        return pl.ds(pl.multiple_of(c * CHUNK, CHUNK), CHUNK)

    def scan(i, carry):
        sf, sb = carry
        cf, cb = i, n_chunks - 1 - i
        st_ref[cf, :, :RET_DK] = sf.astype(BF16)
        st_ref[cb, :, RET_DK:] = sb.astype(BF16)
        kzf = (kt_ref[0, 0, cf].astype(F32) * zf).astype(BF16)
        kzb = (kt_ref[0, 0, cb].astype(F32) * zb).astype(BF16)
        sf = sf * cdf + _dot(kzf, v_ref[0, rows_of(cf), :])
        sb = sb * cdb + _dot(kzb, v_ref[0, rows_of(cb), :])
        return sf, sb

    zero = jnp.zeros((RET_DK, RET_DK), F32)
    lax.fori_loop(0, n_chunks, scan, (zero, zero), unroll=unroll)

    def decayed_scores(c, carry):
        s = _dot(q_ref[0, rows_of(c), :], kt_ref[0, 0, c])
        p_ref[c] = (s * dtot).astype(BF16)
        return carry

    lax.fori_loop(0, n_chunks, decayed_scores, 0, unroll=unroll)

    def mix(c, carry):
        rows = rows_of(c)
        inter = _dot(q_ref[0, rows, :], st_ref[c])
        tot_ref[rows, :] = (_dot(p_ref[c], v_ref[0, rows, :]) + inter[:, :RET_DK] * xif + inter[:, RET_DK:] * xib)
        return carry

    lax.fori_loop(0, n_chunks, mix, 0, unroll=unroll)

    def norm_gate(c, carry):
        rows = rows_of(c)
        tot = tot_ref[rows, :]
        mu = jnp.mean(tot, axis=-1, keepdims=True)
        var = jnp.maximum(jnp.mean(tot * tot, axis=-1, keepdims=True) - mu * mu, 0.0)
        y = (tot - mu) * lax.rsqrt(var + LN_EPS) * gnw
        o_ref[0, rows, :] = (y * g_ref[0, rows, :].astype(F32)).astype(BF16)
        return carry

    lax.fori_loop(0, n_chunks, norm_gate, 0, unroll=unroll)


def _retention(rq, rkt, rv, rgs, dec, gnw):
    B, S, _ = rq.shape
    n = S // CHUNK
    mode = pl.Buffered(1) if S * RET_DK * 2 > RET_DOUBLE_BUFFER_MAX_BYTES else None
    head = lambda pm=None: pl.BlockSpec((1, S, RET_DK), lambda b, h: (b, 0, h), pipeline_mode=pm)
    return pl.pallas_call(
        functools.partial(_ret_kernel, n_chunks=n, unroll=min(RET_UNROLL, n)),
        grid=(B, RET_HEADS),
        in_specs=[head(mode),
                  pl.BlockSpec((1, 1, n, RET_DK, CHUNK), lambda b, h: (b, h, 0, 0, 0), pipeline_mode=mode),
                  head(mode), head(mode),
                  pl.BlockSpec((1, 7, CHUNK, CHUNK), lambda b, h: (h, 0, 0, 0)),
                  pl.BlockSpec((1, RET_DK), lambda b, h: (0, h))],
        out_specs=head(),
        out_shape=jax.ShapeDtypeStruct((B, S, RET_W), BF16),
        scratch_shapes=[pltpu.VMEM((n, RET_DK, 2 * RET_DK), BF16), pltpu.VMEM((n, CHUNK, CHUNK), BF16),
                        pltpu.VMEM((S, RET_DK), F32)],
        compiler_params=_cparams(("parallel", "parallel")),
        name="retention",
    )(rq, rkt, rv, rgs, dec, gnw)


def _attn_kernel(bounded_ref, q_ref, kt_ref, v_ref, o_ref, *scratch, tq, tk, single_kv):
    q = q_ref[0].reshape(ATT_GROUP * tq, ATT_HD)
    kt = kt_ref[0, 0]
    v1 = jnp.concatenate([v_ref[0], jnp.ones((tk, ATT_HD), BF16)], axis=1)
    bounded = bounded_ref[0] == 1

    def write(pv):
        o = pv[:, :ATT_HD] / pv[:, ATT_HD:]
        for g in range(ATT_GROUP):
            o_ref[0, :, g * ATT_HD:(g + 1) * ATT_HD] = o[g * tq:(g + 1) * tq].astype(BF16)

    if single_kv:
        @pl.when(bounded)
        def _():
            write(_dot(jnp.exp2(_dot(q, kt)).astype(BF16), v1))

        @pl.when(jnp.logical_not(bounded))
        def _():
            s = _dot(q, kt)
            write(_dot(jnp.exp2(s - jnp.max(s, axis=1, keepdims=True)).astype(BF16), v1))

        return

    m_ref, acc_ref = scratch
    ki = pl.program_id(3)

    @pl.when(ki == 0)
    def _():
        m_ref[...] = jnp.full(m_ref.shape, -jnp.inf, F32)
        acc_ref[...] = jnp.zeros(acc_ref.shape, F32)

    @pl.when(bounded)
    def _():
        acc_ref[...] += _dot(jnp.exp2(_dot(q, kt)).astype(BF16), v1)

    @pl.when(jnp.logical_not(bounded))
    def _():
        s = _dot(q, kt)
        m_prev = m_ref[...]
        m_next = jnp.maximum(m_prev, jnp.max(s, axis=1, keepdims=True))
        p = jnp.exp2(s - jnp.tile(m_next, (1, tk // ATT_HD)))
        alpha = jnp.exp2(m_prev - m_next)
        acc_ref[...] = jnp.tile(alpha, (1, 2)) * acc_ref[...] + _dot(p.astype(BF16), v1)
        m_ref[...] = m_next

    @pl.when(ki == pl.num_programs(3) - 1)
    def _():
        write(acc_ref[...])


def _attention(bounded, aq, akt, av):
    B, _, S, _ = aq.shape
    tq = _tile(S, ATT_TQ)
    single_kv = S <= ATT_SINGLE_KV_MAX
    tk = S if single_kv else _tile(S, ATT_TK)
    rows = ATT_GROUP * tq
    scratch = [] if single_kv else [pltpu.VMEM((rows, ATT_HD), F32), pltpu.VMEM((rows, 2 * ATT_HD), F32)]
    return pl.pallas_call(
        functools.partial(_attn_kernel, tq=tq, tk=tk, single_kv=single_kv),
        grid=(B, ATT_KV_HEADS, S // tq, S // tk),
        in_specs=[pl.BlockSpec(memory_space=pltpu.SMEM),
                  pl.BlockSpec((1, ATT_GROUP, tq, ATT_HD), lambda b, k, i, j: (b, k, i, 0)),
                  pl.BlockSpec((1, 1, ATT_HD, tk), lambda b, k, i, j: (b, k, 0, j)),
                  pl.BlockSpec((1, tk, ATT_HD), lambda b, k, i, j: (b, j, k))],
        out_specs=pl.BlockSpec((1, tq, ATT_GROUP * ATT_HD), lambda b, k, i, j: (b, i, k)),
        out_shape=jax.ShapeDtypeStruct((B, S, ATT_W), BF16),
        scratch_shapes=scratch,
        compiler_params=_cparams(("parallel", "parallel", "parallel", "arbitrary")),
        name="gqa_attention",
    )(bounded, aq, akt, av)


def _merge_kernel(ro_ref, sg_ref, at_ref, gate_ref, x_ref, wr_ref, ws_ref, wa_ref, wo_ref, lnw_ref, lnb_ref,
                  xo_ref, xb_ref):
    def branch(a_ref, w_ref, i):
        return gate_ref[:, i * D_MODEL:(i + 1) * D_MODEL].astype(F32) * _dot(a_ref[...], w_ref[...])

    merged = branch(ro_ref, wr_ref, 0) + branch(sg_ref, ws_ref, 1) + branch(at_ref, wa_ref, 2)
    y = _dot(merged.astype(BF16), wo_ref[...])
    out = _layer_norm_rows(ALPHA * x_ref[...] + y, lnw_ref[...], lnb_ref[...])
    xo_ref[...] = out
    xb_ref[...] = out.astype(BF16)


def _merge(ro, sg, at, gates, x, wr, ws, wa, wo, lnw, lnb, tm):
    T = x.shape[0]
    row = pl.BlockSpec((tm, D_MODEL), lambda i: (i, 0))
    wspec = _const_spec((D_MODEL, D_MODEL))
    vec = _const_spec((1, D_MODEL))
    return pl.pallas_call(
        _merge_kernel,
        grid=(T // tm,),
        in_specs=[row, row, row, pl.BlockSpec((tm, N_BRANCH * D_MODEL), lambda i: (i, 0)), row,
                  wspec, wspec, wspec, wspec, vec, vec],
        out_specs=[row, row],
        out_shape=[jax.ShapeDtypeStruct((T, D_MODEL), F32), jax.ShapeDtypeStruct((T, D_MODEL), BF16)],
        compiler_params=_cparams(("parallel",)),
        name="merge_out_ln",
    )(ro, sg, at, gates, x, wr, ws, wa, wo, lnw, lnb)


def _cross_kernel(x_ref, xb_ref, kv_ref, wq_ref, wo_ref, lnw_ref, lnb_ref, xo_ref, xbo_ref, o_scr):
    q = _dot(xb_ref[0], wq_ref[...]).astype(BF16)
    scale = X_HD ** -0.5
    for h in range(X_HEADS):
        cols = slice(h * X_HD, (h + 1) * X_HD)
        k = kv_ref[0, :, cols]
        v = kv_ref[0, :, D_MODEL + h * X_HD:D_MODEL + (h + 1) * X_HD]
        s = lax.dot_general(q[:, cols], k, (((1,), (1,)), ((), ())), preferred_element_type=F32) * scale
        p = jnp.exp(s - jnp.max(s, axis=-1, keepdims=True))
        o = _dot(p.astype(BF16), v) / jnp.sum(p, axis=-1, keepdims=True)
        o_scr[:, cols] = o.astype(BF16)
    y = _dot(o_scr[...], wo_ref[...])
    out = _layer_norm_rows(ALPHA * x_ref[0] + y, lnw_ref[...], lnb_ref[...])
    xo_ref[0] = out
    xbo_ref[0] = out.astype(BF16)


def _cross(x3, xb3, kv3, wq, wo, lnw, lnb, tm):
    B, S, _ = x3.shape
    row = pl.BlockSpec((1, tm, D_MODEL), lambda b, i: (b, i, 0))
    wspec = _const_spec((D_MODEL, D_MODEL))
    vec = _const_spec((1, D_MODEL))
    return pl.pallas_call(
        _cross_kernel,
        grid=(B, S // tm),
        in_specs=[row, row, pl.BlockSpec((1, N_MEM, 2 * D_MODEL), lambda b, i: (b, 0, 0)), wspec, wspec, vec, vec],
        out_specs=[row, row],
        out_shape=[jax.ShapeDtypeStruct((B, S, D_MODEL), F32), jax.ShapeDtypeStruct((B, S, D_MODEL), BF16)],
        scratch_shapes=[pltpu.VMEM((tm, D_MODEL), BF16)],
        compiler_params=_cparams(("parallel", "parallel")),
        name="cross_attn_ln",
    )(x3, xb3, kv3, wq, wo, lnw, lnb)


def _ffn_kernel(x_ref, xb_ref, wa_ref, wb_ref, wo_ref, lnw_ref, lnb_ref, xo_ref, xbo_ref, h_scr):
    xb = xb_ref[...]
    for j in range(D_FF // FF_CHUNK):
        cols = slice(j * FF_CHUNK, (j + 1) * FF_CHUNK)
        a = _dot(xb, wa_ref[:, cols])
        b = _dot(xb, wb_ref[:, cols])
        h_scr[:, cols] = (a * _sigmoid(a) * b).astype(BF16)
    y = _dot(h_scr[...], wo_ref[...])
    out = _layer_norm_rows(ALPHA * x_ref[...] + y, lnw_ref[...], lnb_ref[...])
    xo_ref[...] = out
    xbo_ref[...] = out.astype(BF16)


def _ffn(x, xb, wa, wb, wo, lnw, lnb, tm):
    T = x.shape[0]
    row = pl.BlockSpec((tm, D_MODEL), lambda i: (i, 0))
    vec = _const_spec((1, D_MODEL))
    return pl.pallas_call(
        _ffn_kernel,
        grid=(T // tm,),
        in_specs=[row, row, _const_spec((D_MODEL, D_FF)), _const_spec((D_MODEL, D_FF)),
                  _const_spec((D_FF, D_MODEL)), vec, vec],
        out_specs=[row, row],
        out_shape=[jax.ShapeDtypeStruct((T, D_MODEL), F32), jax.ShapeDtypeStruct((T, D_MODEL), BF16)],
        scratch_shapes=[pltpu.VMEM((tm, D_FF), BF16)],
        compiler_params=_cparams(("parallel",)),
        name="swiglu_ln",
    )(x, xb, wa, wb, wo, lnw, lnb)


def _rope_tables(S):
    t = jnp.arange(S)

    def cos_sin(pos, dim):
        inv_freq = ROPE_BASE ** (-jnp.arange(0, dim, 2, dtype=F32) / dim)
        ang = pos.astype(F32)[:, None] * inv_freq[None, :]
        return jnp.cos(ang), jnp.sin(ang)

    ct, st = cos_sin(t, RET_DK)
    cr, sr = cos_sin(t // GRID_W, ATT_HD // 2)
    cc, sc = cos_sin(t % GRID_W, ATT_HD // 2)
    return (jnp.concatenate([ct, ct], -1), jnp.concatenate([-st, st], -1),
            jnp.concatenate([cr, cc, cr, cc], -1), jnp.concatenate([-sr, -sc, sr, sc], -1))


def _axial_dim_order():
    quarter = ATT_HD // 4
    blocks = [jnp.arange(i * quarter, (i + 1) * quarter) for i in (0, 2, 1, 3)]
    return jnp.concatenate(blocks)


def _decay_tables(decay_f, decay_b):
    lgf = jax.nn.log_sigmoid(decay_f.astype(F32))[:, None, None]
    lgb = jax.nn.log_sigmoid(decay_b.astype(F32))[:, None, None]
    idx = jnp.arange(CHUNK, dtype=F32)
    diff = idx[:, None] - idx[None, :]
    dtot = jnp.where(diff >= 0, jnp.exp(lgf * jnp.maximum(diff, 0.0)), jnp.exp(lgb * jnp.maximum(-diff, 0.0)))
    ones = jnp.ones((CHUNK, CHUNK), F32)
    row = idx[None, :, None] * ones
    lane = idx[None, None, :] * ones
    xif = jnp.exp(lgf * (row + 1.0))
    xib = jnp.exp(lgb * (CHUNK - row))
    zf = jnp.exp(lgf * (CHUNK - 1.0 - lane))
    zb = jnp.exp(lgb * lane)
    cdf = jnp.exp(lgf * CHUNK) * ones
    cdb = jnp.exp(lgb * CHUNK) * ones
    return jnp.stack([dtot, xif, xib, zf, zb, cdf, cdb], axis=1)


def _encoder(x, mem, p):
    B, S, _ = x.shape
    T = B * S
    tm = _tile(T, TOKEN_TILE)
    ts = _tile(S, TOKEN_TILE)
    rc, rs, ac, as_ = _rope_tables(S)
    xf, xb = _in_ln(x.reshape(T, D_MODEL), p["in_ln_w"], p["in_ln_b"], tm)
    memb = mem.astype(BF16).reshape(B * N_MEM, D_MODEL)
    flat = lambda a: a.reshape(T, D_MODEL)
    seq = lambda a: a.reshape(B, S, D_MODEL)
    for l in range(DEPTH):
        xb3 = seq(xb)
        rq, rkt = _proj_ret(xb3, p["w_ret"][l], rc, rs, ts)
        rv, rgs, sg = _proj_mid(xb3, p["w_mid"][l], p["sg_ln_w"][l], p["sg_ln_b"][l], p["sg_ws"][l], p["sg_bias"][l], ts)
        aq, akt, av = _proj_att(xb3, p["w_att"][l], ac, as_, p["att_qn_w"][l], p["att_kn_w"][l], ts)
        gates = _proj_gate(xb, p["w_gate"][l], p["b_gate"][l], tm)
        ro = _retention(rq, rkt, rv, rgs, p["dec"][l], p["ret_gn_w"][l])
        at = _attention(p["att_bounded"][l], aq, akt, av)
        xf, xb = _merge(flat(ro), flat(sg), flat(at), gates, xf, p["ret_wo"][l], p["sg_wo"][l], p["att_wo"][l],
                        p["w_out"][l], p["ln_w"][l, 0], p["ln_b"][l, 0], tm)
        kv = _matmul(memb, p["xa_wkv"][l], _tile(B * N_MEM, 1024), 1024, "proj_kv")
        xf3, xb3 = _cross(seq(xf), seq(xb), kv.reshape(B, N_MEM, 2 * D_MODEL), p["xa_wq"][l], p["xa_wo"][l],
                          p["ln_w"][l, 1], p["ln_b"][l, 1], ts)
        xf, xb = _ffn(flat(xf3), flat(xb3), p["ffn_wa"][l], p["ffn_wb"][l], p["ffn_w_out"][l],
                      p["ln_w"][l, 2], p["ln_b"][l, 2], tm)
    return seq(xf)


def _prepare_params(in_ln_w, in_ln_b, w_in, b_gate, ret_decay_f, ret_decay_b, ret_gn_w, ret_wo, sg_ln_w, sg_ln_b,
                    sg_ws, sg_b, sg_wo, att_qn_w, att_kn_w, att_wo, w_out, ln_w, ln_b, xa_wq, xa_wkv, xa_wo,
                    ffn_w_in, ffn_w_out):
    vec = lambda a: a.astype(F32).reshape(a.shape[:-1] + (1, a.shape[-1]))
    group = lambda c: w_in[:, :, c[0]:c[1]].astype(BF16)
    sg_bias = jnp.repeat(jnp.swapaxes(sg_b.astype(F32), 1, 2), SG_GW, axis=-1)
    score_bound = (SCORE_BOUND_COEF * jnp.max(jnp.abs(att_qn_w.astype(F32)), axis=-1)
                   * jnp.max(jnp.abs(att_kn_w.astype(F32)), axis=-1))
    order = _axial_dim_order()
    w_att = group(COLS_ATT)
    n_qk = ATT_W + ATT_KV_W
    w_qk = w_att[:, :, :n_qk].reshape(DEPTH, D_MODEL, n_qk // ATT_HD, ATT_HD)[..., order].reshape(DEPTH, D_MODEL, n_qk)
    w_att = jnp.concatenate([w_qk, w_att[:, :, n_qk:]], axis=-1)
    with_partner = lambda w: jnp.stack([w[:, order], jnp.roll(w[:, order], ATT_HD // 2, axis=-1)], axis=1).astype(F32)
    att_qn_w = with_partner(att_qn_w)
    att_kn_w = with_partner(att_kn_w)
    return dict(
        att_bounded=(score_bound <= SAFE_SCORE_BOUND).astype(jnp.int32).reshape(DEPTH, 1),
        in_ln_w=vec(in_ln_w), in_ln_b=vec(in_ln_b),
        w_ret=group(COLS_RET), w_mid=group(COLS_MID), w_att=w_att, w_gate=group(COLS_GATE),
        b_gate=vec(b_gate),
        dec=jnp.stack([_decay_tables(ret_decay_f[l], ret_decay_b[l]) for l in range(DEPTH)]),
        ret_gn_w=vec(ret_gn_w), ret_wo=ret_wo.astype(BF16),
        sg_ln_w=vec(sg_ln_w), sg_ln_b=vec(sg_ln_b), sg_ws=sg_ws.astype(BF16), sg_bias=sg_bias,
        sg_wo=sg_wo.astype(BF16),
        att_qn_w=att_qn_w, att_kn_w=att_kn_w, att_wo=att_wo.astype(BF16),
        w_out=w_out.astype(BF16), ln_w=vec(ln_w), ln_b=vec(ln_b),
        xa_wq=xa_wq.astype(BF16), xa_wkv=xa_wkv.astype(BF16), xa_wo=xa_wo.astype(BF16),
        ffn_wa=ffn_w_in[:, :, :D_FF].astype(BF16), ffn_wb=ffn_w_in[:, :, D_FF:].astype(BF16),
        ffn_w_out=ffn_w_out.astype(BF16),
    )


def kernel(x_prompt, x_sample, mem_prompt, mem_sample, in_ln_w, in_ln_b, w_in, b_gate, ret_decay_f, ret_decay_b,
           ret_gn_w, ret_wo, sg_ln_w, sg_ln_b, sg_ws, sg_b, sg_wo, att_qn_w, att_kn_w, att_wo, w_out, ln_w, ln_b,
           xa_wq, xa_wkv, xa_wo, ffn_w_in, ffn_w_out):
    p = _prepare_params(in_ln_w, in_ln_b, w_in, b_gate, ret_decay_f, ret_decay_b, ret_gn_w, ret_wo, sg_ln_w,
                        sg_ln_b, sg_ws, sg_b, sg_wo, att_qn_w, att_kn_w, att_wo, w_out, ln_w, ln_b, xa_wq, xa_wkv,
                        xa_wo, ffn_w_in, ffn_w_out)
    return (_encoder(x_prompt, mem_prompt, p), _encoder(x_sample, mem_sample, p))
```

```python
import functools

import jax
import jax.numpy as jnp
from jax import lax
from jax.experimental import pallas as pl
from jax.experimental.pallas import tpu as pltpu

F32 = jnp.float32
BF16 = jnp.bfloat16

D_MODEL = 1024
DEPTH = 4
N_MEM = 256
GRID_W = 64
CHUNK = 128
RET_HEADS = 8
RET_DK = 128
RET_W = RET_HEADS * RET_DK
SG_GROUPS = 4
SG_GW = 256
SG_W = SG_GROUPS * SG_GW
ATT_HEADS = 8
ATT_KV_HEADS = 2
ATT_GROUP = ATT_HEADS // ATT_KV_HEADS
ATT_HD = 128
ATT_W = ATT_HEADS * ATT_HD
ATT_KV_W = ATT_KV_HEADS * ATT_HD
X_HEADS = 4
X_HD = D_MODEL // X_HEADS
D_FF = 2816
N_BRANCH = 3
ALPHA = (2 * DEPTH) ** 0.25
ROPE_BASE = 10000.0
LN_EPS = 1e-5
RMS_EPS = 1e-6
LOG2E = 1.4426950408889634

COLS_RET = (0, 2 * RET_W)
COLS_MID = (COLS_RET[1], COLS_RET[1] + 2 * RET_W + 2 * SG_W)
COLS_ATT = (COLS_MID[1], COLS_MID[1] + ATT_W + 2 * ATT_KV_W)
COLS_GATE = (COLS_ATT[1], COLS_ATT[1] + N_BRANCH * D_MODEL)

MXU_W = 256
FF_CHUNK = MXU_W
SLAB = 512
TOKEN_TILE = 1024
MERGE_TILE = 512
ATT_TQ = 512
ATT_TK = 2048
ATT_SINGLE_KV_MAX = 2048
RET_UNROLL = 8
RET_DOUBLE_BUFFER_MAX_BYTES = 1024 * 1024
SCORE_BOUND_COEF = ATT_HD ** 0.5 * LOG2E
SAFE_SCORE_BOUND = 64.0
VMEM_LIMIT = 52 * 1024 * 1024


def _cparams(sem):
    return pltpu.CompilerParams(dimension_semantics=sem, vmem_limit_bytes=VMEM_LIMIT)


def _const_spec(shape):
    nd = len(shape)
    return pl.BlockSpec(shape, lambda *_: (0,) * nd, pipeline_mode=pl.Buffered(1))


def _tile(n, pref):
    t = min(n, pref)
    assert n % t == 0, (n, t)
    return t


def _layer_norm_rows(z, w, b):
    mu = jnp.mean(z, axis=-1, keepdims=True)
    d = z - mu
    var = jnp.mean(d * d, axis=-1, keepdims=True)
    return d * lax.rsqrt(var + LN_EPS) * w + b


def _gelu_tanh(x):
    return 0.5 * x * (1.0 + jnp.tanh(0.7978845608028654 * (x + 0.044715 * (x * x * x))))


def _sigmoid(x):
    return 1.0 / (1.0 + jnp.exp(-x))


def _dot(a, b):
    return jnp.dot(a, b, preferred_element_type=F32)


def _in_ln_kernel(x_ref, w_ref, b_ref, xo_ref, xb_ref):
    y = _layer_norm_rows(x_ref[...], w_ref[...], b_ref[...])
    xo_ref[...] = y
    xb_ref[...] = y.astype(BF16)


def _in_ln(x2, w, b, tm):
    T = x2.shape[0]
    row = pl.BlockSpec((tm, D_MODEL), lambda i: (i, 0))
    return pl.pallas_call(
        _in_ln_kernel,
        grid=(T // tm,),
        in_specs=[row, _const_spec((1, D_MODEL)), _const_spec((1, D_MODEL))],
        out_specs=[row, row],
        out_shape=[jax.ShapeDtypeStruct((T, D_MODEL), F32), jax.ShapeDtypeStruct((T, D_MODEL), BF16)],
        compiler_params=_cparams(("parallel",)),
        name="in_ln",
    )(x2, w, b)


def _mm_kernel(x_ref, w_ref, o_ref):
    o_ref[...] = _dot(x_ref[...], w_ref[...]).astype(o_ref.dtype)


def _matmul(x, w, tm, tn, name):
    M, K = x.shape
    N = w.shape[1]
    return pl.pallas_call(
        _mm_kernel,
        grid=(N // tn, M // tm),
        in_specs=[pl.BlockSpec((tm, K), lambda j, i: (i, 0)), pl.BlockSpec((K, tn), lambda j, i: (0, j))],
        out_specs=pl.BlockSpec((tm, tn), lambda j, i: (i, j)),
        out_shape=jax.ShapeDtypeStruct((M, N), BF16),
        compiler_params=_cparams(("parallel", "parallel")),
        name=name,
    )(x, w)


def _rope(x, cos, sin_signed, partner):
    return x * cos + partner * sin_signed


def _proj_ret_kernel(x_ref, w_ref, rc_ref, rs_ref, rq_ref, rkt_ref, *, tm):
    x = x_ref[0]
    rc = rc_ref[...]
    rs = rs_ref[...]
    k_scale = RET_DK ** -0.5

    def rope(t):
        return _rope(t, rc, rs, pltpu.roll(t, RET_DK // 2, axis=1))

    heads_per_slab = MXU_W // RET_DK
    for slab in range(RET_W // MXU_W):
        q2 = _dot(x, w_ref[:, slab * MXU_W:(slab + 1) * MXU_W])
        k2 = _dot(x, w_ref[:, RET_W + slab * MXU_W:RET_W + (slab + 1) * MXU_W])
        for j in range(heads_per_slab):
            h = slab * heads_per_slab + j
            sl = slice(j * RET_DK, (j + 1) * RET_DK)
            rq_ref[0, :, h * RET_DK:(h + 1) * RET_DK] = rope(q2[:, sl]).astype(BF16)
            kt = (rope(k2[:, sl]) * k_scale).T
            for c in range(tm // CHUNK):
                rkt_ref[0, h, c] = kt[:, c * CHUNK:(c + 1) * CHUNK].astype(BF16)


def _proj_ret(xb3, w, rc, rs, tm):
    B, S, _ = xb3.shape
    tab = pl.BlockSpec((tm, RET_DK), lambda b, s: (s, 0))
    return pl.pallas_call(
        functools.partial(_proj_ret_kernel, tm=tm),
        grid=(B, S // tm),
        in_specs=[pl.BlockSpec((1, tm, D_MODEL), lambda b, s: (b, s, 0)), _const_spec((D_MODEL, 2 * RET_W)), tab, tab],
        out_specs=[pl.BlockSpec((1, tm, RET_W), lambda b, s: (b, s, 0)),
                   pl.BlockSpec((1, RET_HEADS, tm // CHUNK, RET_DK, CHUNK), lambda b, s: (b, 0, s, 0, 0))],
        out_shape=[jax.ShapeDtypeStruct((B, S, RET_W), BF16),
                   jax.ShapeDtypeStruct((B, RET_HEADS, S // CHUNK, RET_DK, CHUNK), BF16)],
        compiler_params=_cparams(("parallel", "parallel")),
        name="proj_ret",
    )(xb3, w, rc, rs)


def _proj_mid_kernel(x_ref, w_ref, lnw_ref, lnb_ref, ws_ref, bias_ref, rv_ref, rgs_ref, sg_ref, vn_ref, *, tm):
    x = x_ref[0]
    for i in range(RET_W // SLAB):
        cols = slice(i * SLAB, (i + 1) * SLAB)
        rv_ref[0, :, cols] = _dot(x, w_ref[:, cols]).astype(BF16)
        g = _dot(x, w_ref[:, RET_W + i * SLAB:RET_W + (i + 1) * SLAB])
        rgs_ref[0, :, cols] = (g * _sigmoid(g)).astype(BF16)
    sv = _gelu_tanh(_dot(x, w_ref[:, 2 * RET_W + SG_W:2 * RET_W + 2 * SG_W]))
    vn_ref[...] = _layer_norm_rows(sv, lnw_ref[...], lnb_ref[...]).astype(BF16)
    for grp in range(SG_GROUPS):
        cols = slice(grp * SG_GW, (grp + 1) * SG_GW)
        u = _gelu_tanh(_dot(x, w_ref[:, 2 * RET_W + grp * SG_GW:2 * RET_W + (grp + 1) * SG_GW]))
        for c in range(tm // CHUNK):
            rows = slice(c * CHUNK, (c + 1) * CHUNK)
            mixed = _dot(ws_ref[grp], vn_ref[rows, cols]) + bias_ref[:, cols]
            sg_ref[0, rows, cols] = (u[rows, :] * mixed).astype(BF16)


def _proj_mid(xb3, w, lnw, lnb, ws, bias, tm):
    B, S, _ = xb3.shape
    row = pl.BlockSpec((1, tm, D_MODEL), lambda b, s: (b, s, 0))
    out = jax.ShapeDtypeStruct((B, S, D_MODEL), BF16)
    return pl.pallas_call(
        functools.partial(_proj_mid_kernel, tm=tm),
        grid=(B, S // tm),
        in_specs=[row, _const_spec((D_MODEL, 2 * RET_W + 2 * SG_W)), _const_spec((1, SG_W)), _const_spec((1, SG_W)),
                  _const_spec((SG_GROUPS, CHUNK, CHUNK)), _const_spec((CHUNK, SG_W))],
        out_specs=[row, row, row],
        out_shape=[out, out, out],
        scratch_shapes=[pltpu.VMEM((tm, SG_W), BF16)],
        compiler_params=_cparams(("parallel", "parallel")),
        name="proj_mid",
    )(xb3, w, lnw, lnb, ws, bias)


def _proj_att_kernel(x_ref, w_ref, ac_ref, as_ref, qnw_ref, knw_ref, aq_ref, akt_ref, av_ref, *, tm, tq):
    x = x_ref[0]
    q_scale = ATT_HD ** -0.5 * LOG2E
    q_cos = ac_ref[...] * (qnw_ref[0:1, :] * q_scale)
    q_sin = as_ref[...] * (qnw_ref[1:2, :] * q_scale)
    k_cos = ac_ref[...] * knw_ref[0:1, :]
    k_sin = as_ref[...] * knw_ref[1:2, :]

    def norm_rope(t, cos, sin):
        r = lax.rsqrt(jnp.mean(t * t, axis=-1, keepdims=True) + RMS_EPS)
        return r * (t * cos + pltpu.roll(t, ATT_HD // 2, axis=1) * sin)

    heads_per_slab = MXU_W // ATT_HD
    for slab in range(ATT_W // MXU_W):
        q2 = _dot(x, w_ref[:, slab * MXU_W:(slab + 1) * MXU_W])
        for j in range(heads_per_slab):
            kv_head, g = divmod(slab * heads_per_slab + j, ATT_GROUP)
            for c in range(tm // CHUNK):
                rows = slice(c * CHUNK, (c + 1) * CHUNK)
                t = q2[rows, j * ATT_HD:(j + 1) * ATT_HD]
                q_tile, r0 = divmod(c * CHUNK, tq)
                aq_ref[0, kv_head, q_tile, g * tq + r0:g * tq + r0 + CHUNK, :] = (
                    norm_rope(t, q_cos[rows], q_sin[rows]).astype(BF16))
    k2 = _dot(x, w_ref[:, ATT_W:ATT_W + ATT_KV_W])
    for j in range(ATT_KV_HEADS):
        for c in range(tm // CHUNK):
            rows = slice(c * CHUNK, (c + 1) * CHUNK)
            t = k2[rows, j * ATT_HD:(j + 1) * ATT_HD]
            akt_ref[0, j, :, rows] = norm_rope(t, k_cos[rows], k_sin[rows]).T.astype(BF16)
    av_ref[0] = _dot(x, w_ref[:, ATT_W + ATT_KV_W:ATT_W + 2 * ATT_KV_W]).astype(BF16)


def _proj_att(xb3, w, ac, as_, qnw, knw, tm, tq):
    B, S, _ = xb3.shape
    assert tm % tq == 0, (tm, tq)
    tab = pl.BlockSpec((tm, ATT_HD), lambda b, s: (s, 0))
    return pl.pallas_call(
        functools.partial(_proj_att_kernel, tm=tm, tq=tq),
        grid=(B, S // tm),
        in_specs=[pl.BlockSpec((1, tm, D_MODEL), lambda b, s: (b, s, 0)), _const_spec((D_MODEL, ATT_W + 2 * ATT_KV_W)),
                  tab, tab, _const_spec((2, ATT_HD)), _const_spec((2, ATT_HD))],
        out_specs=[pl.BlockSpec((1, ATT_KV_HEADS, tm // tq, ATT_GROUP * tq, ATT_HD), lambda b, s: (b, 0, s, 0, 0)),
                   pl.BlockSpec((1, ATT_KV_HEADS, ATT_HD, tm), lambda b, s: (b, 0, 0, s)),
                   pl.BlockSpec((1, tm, ATT_KV_W), lambda b, s: (b, s, 0))],
        out_shape=[jax.ShapeDtypeStruct((B, ATT_KV_HEADS, S // tq, ATT_GROUP * tq, ATT_HD), BF16),
                   jax.ShapeDtypeStruct((B, ATT_KV_HEADS, ATT_HD, S), BF16),
                   jax.ShapeDtypeStruct((B, S, ATT_KV_W), BF16)],
        compiler_params=_cparams(("parallel", "parallel")),
        name="proj_att",
    )(xb3, w, ac, as_, qnw, knw)


def _proj_gate_kernel(x_ref, w_ref, b_ref, o_ref):
    x = x_ref[...]
    for i in range(N_BRANCH * D_MODEL // SLAB):
        cols = slice(i * SLAB, (i + 1) * SLAB)
        o_ref[:, cols] = _sigmoid(_dot(x, w_ref[:, cols]) + b_ref[:, cols]).astype(BF16)


def _proj_gate(xb, w, b, tm):
    T = xb.shape[0]
    n = N_BRANCH * D_MODEL
    return pl.pallas_call(
        _proj_gate_kernel,
        grid=(T // tm,),
        in_specs=[pl.BlockSpec((tm, D_MODEL), lambda i: (i, 0)), _const_spec((D_MODEL, n)), _const_spec((1, n))],
        out_specs=pl.BlockSpec((tm, n), lambda i: (i, 0)),
        out_shape=jax.ShapeDtypeStruct((T, n), BF16),
        compiler_params=_cparams(("parallel",)),
        name="proj_gate",
    )(xb, w, b)


def _ret_kernel(q_ref, kt_ref, v_ref, g_ref, dec_ref, gnw_ref, o_ref, st_ref, p_ref, tot_ref, *, n_chunks, unroll):
    dtot = dec_ref[0, 0]
    xif, xib = dec_ref[0, 1], dec_ref[0, 2]
    zf, zb = dec_ref[0, 3], dec_ref[0, 4]
    cdf, cdb = dec_ref[0, 5], dec_ref[0, 6]
    gnw = gnw_ref[...]

    def rows_of(c):
        return pl.ds(pl.multiple_of(c * CHUNK, CHUNK), CHUNK)

    def scan(i, carry):
        sf, sb = carry
        cf, cb = i, n_chunks - 1 - i
        st_ref[cf, :, :RET_DK] = sf.astype(BF16)
        st_ref[cb, :, RET_DK:] = sb.astype(BF16)
        kzf = (kt_ref[0, 0, cf].astype(F32) * zf).astype(BF16)
        kzb = (kt_ref[0, 0, cb].astype(F32) * zb).astype(BF16)
        sf = sf * cdf + _dot(kzf, v_ref[0, rows_of(cf), :])
        sb = sb * cdb + _dot(kzb, v_ref[0, rows_of(cb), :])
        return sf, sb

    zero = jnp.zeros((RET_DK, RET_DK), F32)
    lax.fori_loop(0, n_chunks, scan, (zero, zero), unroll=unroll)

    def decayed_scores(c, carry):
        s = _dot(q_ref[0, rows_of(c), :], kt_ref[0, 0, c])
        p_ref[c] = (s * dtot).astype(BF16)
        return carry

    lax.fori_loop(0, n_chunks, decayed_scores, 0, unroll=unroll)

    def mix(c, carry):
        rows = rows_of(c)
        inter = _dot(q_ref[0, rows, :], st_ref[c])
        tot_ref[rows, :] = (_dot(p_ref[c], v_ref[0, rows, :]) + inter[:, :RET_DK] * xif + inter[:, RET_DK:] * xib)
        return carry

    lax.fori_loop(0, n_chunks, mix, 0, unroll=unroll)

    def norm_gate(c, carry):
        rows = rows_of(c)
        tot = tot_ref[rows, :]
        mu = jnp.mean(tot, axis=-1, keepdims=True)
        var = jnp.maximum(jnp.mean(tot * tot, axis=-1, keepdims=True) - mu * mu, 0.0)
        y = (tot - mu) * lax.rsqrt(var + LN_EPS) * gnw
        o_ref[0, rows, :] = (y * g_ref[0, rows, :].astype(F32)).astype(BF16)
        return carry

    lax.fori_loop(0, n_chunks, norm_gate, 0, unroll=unroll)


def _retention(rq, rkt, rv, rgs, dec, gnw):
    B, S, _ = rq.shape
    n = S // CHUNK
    mode = pl.Buffered(1) if S * RET_DK * 2 > RET_DOUBLE_BUFFER_MAX_BYTES else None
    head = lambda pm=None: pl.BlockSpec((1, S, RET_DK), lambda b, h: (b, 0, h), pipeline_mode=pm)
    return pl.pallas_call(
        functools.partial(_ret_kernel, n_chunks=n, unroll=min(RET_UNROLL, n)),
        grid=(B, RET_HEADS),
        in_specs=[head(mode),
                  pl.BlockSpec((1, 1, n, RET_DK, CHUNK), lambda b, h: (b, h, 0, 0, 0), pipeline_mode=mode),
                  head(mode), head(mode),
                  pl.BlockSpec((1, 7, CHUNK, CHUNK), lambda b, h: (h, 0, 0, 0)),
                  pl.BlockSpec((1, RET_DK), lambda b, h: (0, h))],
        out_specs=head(),
        out_shape=jax.ShapeDtypeStruct((B, S, RET_W), BF16),
        scratch_shapes=[pltpu.VMEM((n, RET_DK, 2 * RET_DK), BF16), pltpu.VMEM((n, CHUNK, CHUNK), BF16),
                        pltpu.VMEM((S, RET_DK), F32)],
        compiler_params=_cparams(("parallel", "parallel")),
        name="retention",
    )(rq, rkt, rv, rgs, dec, gnw)


def _attn_kernel(bounded_ref, q_ref, kt_ref, v_ref, o_ref, *scratch, tq, tk, single_kv):
    q = q_ref[0, 0, 0]
    kt = kt_ref[0, 0]
    v1 = jnp.concatenate([v_ref[0], jnp.ones((tk, ATT_HD), BF16)], axis=1)
    bounded = bounded_ref[0] == 1

    def write(pv):
        o = pv[:, :ATT_HD] / pv[:, ATT_HD:]
        for g in range(ATT_GROUP):
            o_ref[0, :, g * ATT_HD:(g + 1) * ATT_HD] = o[g * tq:(g + 1) * tq].astype(BF16)

    if single_kv:
        @pl.when(bounded)
        def _():
            write(_dot(jnp.exp2(_dot(q, kt)).astype(BF16), v1))

        @pl.when(jnp.logical_not(bounded))
        def _():
            s = _dot(q, kt)
            write(_dot(jnp.exp2(s - jnp.max(s, axis=1, keepdims=True)).astype(BF16), v1))

        return

    m_ref, acc_ref = scratch
    ki = pl.program_id(3)

    @pl.when(ki == 0)
    def _():
        m_ref[...] = jnp.full(m_ref.shape, -jnp.inf, F32)
        acc_ref[...] = jnp.zeros(acc_ref.shape, F32)

    @pl.when(bounded)
    def _():
        acc_ref[...] += _dot(jnp.exp2(_dot(q, kt)).astype(BF16), v1)

    @pl.when(jnp.logical_not(bounded))
    def _():
        s = _dot(q, kt)
        m_prev = m_ref[...]
        m_next = jnp.maximum(m_prev, jnp.max(s, axis=1, keepdims=True))
        p = jnp.exp2(s - jnp.tile(m_next, (1, tk // ATT_HD)))
        alpha = jnp.exp2(m_prev - m_next)
        acc_ref[...] = jnp.tile(alpha, (1, 2)) * acc_ref[...] + _dot(p.astype(BF16), v1)
        m_ref[...] = m_next

    @pl.when(ki == pl.num_programs(3) - 1)
    def _():
        write(acc_ref[...])


def _attention(bounded, aq, akt, av):
    B, S, _ = av.shape
    tq = aq.shape[3] // ATT_GROUP
    single_kv = S <= ATT_SINGLE_KV_MAX
    tk = S if single_kv else _tile(S, ATT_TK)
    rows = ATT_GROUP * tq
    scratch = [] if single_kv else [pltpu.VMEM((rows, ATT_HD), F32), pltpu.VMEM((rows, 2 * ATT_HD), F32)]
    return pl.pallas_call(
        functools.partial(_attn_kernel, tq=tq, tk=tk, single_kv=single_kv),
        grid=(B, ATT_KV_HEADS, S // tq, S // tk),
        in_specs=[pl.BlockSpec(memory_space=pltpu.SMEM),
                  pl.BlockSpec((1, 1, 1, ATT_GROUP * tq, ATT_HD), lambda b, k, i, j: (b, k, i, 0, 0)),
                  pl.BlockSpec((1, 1, ATT_HD, tk), lambda b, k, i, j: (b, k, 0, j)),
                  pl.BlockSpec((1, tk, ATT_HD), lambda b, k, i, j: (b, j, k))],
        out_specs=pl.BlockSpec((1, tq, ATT_GROUP * ATT_HD), lambda b, k, i, j: (b, i, k)),
        out_shape=jax.ShapeDtypeStruct((B, S, ATT_W), BF16),
        scratch_shapes=scratch,
        compiler_params=_cparams(("parallel", "parallel", "parallel", "arbitrary")),
        name="gqa_attention",
    )(bounded, aq, akt, av)


def _merge_kernel(ro_ref, sg_ref, at_ref, gate_ref, x_ref, wr_ref, ws_ref, wa_ref, wo_ref, lnw_ref, lnb_ref,
                  xo_ref, xb_ref):
    def branch(a_ref, w_ref, i):
        return gate_ref[:, i * D_MODEL:(i + 1) * D_MODEL].astype(F32) * _dot(a_ref[...], w_ref[...])

    merged = branch(ro_ref, wr_ref, 0) + branch(sg_ref, ws_ref, 1) + branch(at_ref, wa_ref, 2)
    y = _dot(merged.astype(BF16), wo_ref[...])
    out = _layer_norm_rows(ALPHA * x_ref[...] + y, lnw_ref[...], lnb_ref[...])
    xo_ref[...] = out
    xb_ref[...] = out.astype(BF16)


def _merge(ro, sg, at, gates, x, wr, ws, wa, wo, lnw, lnb, tm):
    T = x.shape[0]
    row = pl.BlockSpec((tm, D_MODEL), lambda i: (i, 0))
    wspec = _const_spec((D_MODEL, D_MODEL))
    vec = _const_spec((1, D_MODEL))
    return pl.pallas_call(
        _merge_kernel,
        grid=(T // tm,),
        in_specs=[row, row, row, pl.BlockSpec((tm, N_BRANCH * D_MODEL), lambda i: (i, 0)), row,
                  wspec, wspec, wspec, wspec, vec, vec],
        out_specs=[row, row],
        out_shape=[jax.ShapeDtypeStruct((T, D_MODEL), F32), jax.ShapeDtypeStruct((T, D_MODEL), BF16)],
        compiler_params=_cparams(("parallel",)),
        name="merge_out_ln",
    )(ro, sg, at, gates, x, wr, ws, wa, wo, lnw, lnb)


def _cross_kernel(x_ref, xb_ref, kv_ref, wq_ref, wo_ref, lnw_ref, lnb_ref, xo_ref, xbo_ref, o_scr):
    q = _dot(xb_ref[0], wq_ref[...]).astype(BF16)
    scale = X_HD ** -0.5
    for h in range(X_HEADS):
        cols = slice(h * X_HD, (h + 1) * X_HD)
        k = kv_ref[0, :, cols]
        v = kv_ref[0, :, D_MODEL + h * X_HD:D_MODEL + (h + 1) * X_HD]
        s = lax.dot_general(q[:, cols], k, (((1,), (1,)), ((), ())), preferred_element_type=F32) * scale
        p = jnp.exp(s - jnp.max(s, axis=-1, keepdims=True))
        o = _dot(p.astype(BF16), v) / jnp.sum(p, axis=-1, keepdims=True)
        o_scr[:, cols] = o.astype(BF16)
    y = _dot(o_scr[...], wo_ref[...])
    out = _layer_norm_rows(ALPHA * x_ref[0] + y, lnw_ref[...], lnb_ref[...])
    xo_ref[0] = out
    xbo_ref[0] = out.astype(BF16)


def _cross(x3, xb3, kv3, wq, wo, lnw, lnb, tm):
    B, S, _ = x3.shape
    row = pl.BlockSpec((1, tm, D_MODEL), lambda b, i: (b, i, 0))
    wspec = _const_spec((D_MODEL, D_MODEL))
    vec = _const_spec((1, D_MODEL))
    return pl.pallas_call(
        _cross_kernel,
        grid=(B, S // tm),
        in_specs=[row, row, pl.BlockSpec((1, N_MEM, 2 * D_MODEL), lambda b, i: (b, 0, 0)), wspec, wspec, vec, vec],
        out_specs=[row, row],
        out_shape=[jax.ShapeDtypeStruct((B, S, D_MODEL), F32), jax.ShapeDtypeStruct((B, S, D_MODEL), BF16)],
        scratch_shapes=[pltpu.VMEM((tm, D_MODEL), BF16)],
        compiler_params=_cparams(("parallel", "parallel")),
        name="cross_attn_ln",
    )(x3, xb3, kv3, wq, wo, lnw, lnb)


def _ffn_kernel(x_ref, xb_ref, wa_ref, wb_ref, wo_ref, lnw_ref, lnb_ref, xo_ref, xbo_ref, h_scr):
    xb = xb_ref[...]
    for j in range(D_FF // FF_CHUNK):
        cols = slice(j * FF_CHUNK, (j + 1) * FF_CHUNK)
        a = _dot(xb, wa_ref[:, cols])
        b = _dot(xb, wb_ref[:, cols])
        h_scr[:, cols] = (a * _sigmoid(a) * b).astype(BF16)
    y = _dot(h_scr[...], wo_ref[...])
    out = _layer_norm_rows(ALPHA * x_ref[...] + y, lnw_ref[...], lnb_ref[...])
    xo_ref[...] = out
    xbo_ref[...] = out.astype(BF16)


def _ffn(x, xb, wa, wb, wo, lnw, lnb, tm):
    T = x.shape[0]
    row = pl.BlockSpec((tm, D_MODEL), lambda i: (i, 0))
    vec = _const_spec((1, D_MODEL))
    return pl.pallas_call(
        _ffn_kernel,
        grid=(T // tm,),
        in_specs=[row, row, _const_spec((D_MODEL, D_FF)), _const_spec((D_MODEL, D_FF)),
                  _const_spec((D_FF, D_MODEL)), vec, vec],
        out_specs=[row, row],
        out_shape=[jax.ShapeDtypeStruct((T, D_MODEL), F32), jax.ShapeDtypeStruct((T, D_MODEL), BF16)],
        scratch_shapes=[pltpu.VMEM((tm, D_FF), BF16)],
        compiler_params=_cparams(("parallel",)),
        name="swiglu_ln",
    )(x, xb, wa, wb, wo, lnw, lnb)


def _rope_tables(S):
    t = jnp.arange(S)

    def cos_sin(pos, dim):
        inv_freq = ROPE_BASE ** (-jnp.arange(0, dim, 2, dtype=F32) / dim)
        ang = pos.astype(F32)[:, None] * inv_freq[None, :]
        return jnp.cos(ang), jnp.sin(ang)

    ct, st = cos_sin(t, RET_DK)
    cr, sr = cos_sin(t // GRID_W, ATT_HD // 2)
    cc, sc = cos_sin(t % GRID_W, ATT_HD // 2)
    return (jnp.concatenate([ct, ct], -1), jnp.concatenate([-st, st], -1),
            jnp.concatenate([cr, cc, cr, cc], -1), jnp.concatenate([-sr, -sc, sr, sc], -1))


def _axial_dim_order():
    quarter = ATT_HD // 4
    blocks = [jnp.arange(i * quarter, (i + 1) * quarter) for i in (0, 2, 1, 3)]
    return jnp.concatenate(blocks)


def _decay_tables(decay_f, decay_b):
    lgf = jax.nn.log_sigmoid(decay_f.astype(F32))[:, None, None]
    lgb = jax.nn.log_sigmoid(decay_b.astype(F32))[:, None, None]
    idx = jnp.arange(CHUNK, dtype=F32)
    diff = idx[:, None] - idx[None, :]
    dtot = jnp.where(diff >= 0, jnp.exp(lgf * jnp.maximum(diff, 0.0)), jnp.exp(lgb * jnp.maximum(-diff, 0.0)))
    ones = jnp.ones((CHUNK, CHUNK), F32)
    row = idx[None, :, None] * ones
    lane = idx[None, None, :] * ones
    xif = jnp.exp(lgf * (row + 1.0))
    xib = jnp.exp(lgb * (CHUNK - row))
    zf = jnp.exp(lgf * (CHUNK - 1.0 - lane))
    zb = jnp.exp(lgb * lane)
    cdf = jnp.exp(lgf * CHUNK) * ones
    cdb = jnp.exp(lgb * CHUNK) * ones
    return jnp.stack([dtot, xif, xib, zf, zb, cdf, cdb], axis=1)


def _encoder(x, mem, p):
    B, S, _ = x.shape
    T = B * S
    tm = _tile(T, TOKEN_TILE)
    ts = _tile(S, TOKEN_TILE)
    rc, rs, ac, as_ = _rope_tables(S)
    xf, xb = _in_ln(x.reshape(T, D_MODEL), p["in_ln_w"], p["in_ln_b"], tm)
    memb = mem.astype(BF16).reshape(B * N_MEM, D_MODEL)
    flat = lambda a: a.reshape(T, D_MODEL)
    seq = lambda a: a.reshape(B, S, D_MODEL)
    for l in range(DEPTH):
        xb3 = seq(xb)
        rq, rkt = _proj_ret(xb3, p["w_ret"][l], rc, rs, ts)
        rv, rgs, sg = _proj_mid(xb3, p["w_mid"][l], p["sg_ln_w"][l], p["sg_ln_b"][l], p["sg_ws"][l], p["sg_bias"][l], ts)
        aq, akt, av = _proj_att(xb3, p["w_att"][l], ac, as_, p["att_qn_w"][l], p["att_kn_w"][l], ts, _tile(S, ATT_TQ))
        gates = _proj_gate(xb, p["w_gate"][l], p["b_gate"][l], tm)
        ro = _retention(rq, rkt, rv, rgs, p["dec"][l], p["ret_gn_w"][l])
        at = _attention(p["att_bounded"][l], aq, akt, av)
        xf, xb = _merge(flat(ro), flat(sg), flat(at), gates, xf, p["ret_wo"][l], p["sg_wo"][l], p["att_wo"][l],
                        p["w_out"][l], p["ln_w"][l, 0], p["ln_b"][l, 0], _tile(T, MERGE_TILE))
        kv = _matmul(memb, p["xa_wkv"][l], _tile(B * N_MEM, 1024), 1024, "proj_kv")
        xf3, xb3 = _cross(seq(xf), seq(xb), kv.reshape(B, N_MEM, 2 * D_MODEL), p["xa_wq"][l], p["xa_wo"][l],
                          p["ln_w"][l, 1], p["ln_b"][l, 1], ts)
        xf, xb = _ffn(flat(xf3), flat(xb3), p["ffn_wa"][l], p["ffn_wb"][l], p["ffn_w_out"][l],
                      p["ln_w"][l, 2], p["ln_b"][l, 2], tm)
    return seq(xf)


def _prepare_params(in_ln_w, in_ln_b, w_in, b_gate, ret_decay_f, ret_decay_b, ret_gn_w, ret_wo, sg_ln_w, sg_ln_b,
                    sg_ws, sg_b, sg_wo, att_qn_w, att_kn_w, att_wo, w_out, ln_w, ln_b, xa_wq, xa_wkv, xa_wo,
                    ffn_w_in, ffn_w_out):
    vec = lambda a: a.astype(F32).reshape(a.shape[:-1] + (1, a.shape[-1]))
    group = lambda c: w_in[:, :, c[0]:c[1]].astype(BF16)
    sg_bias = jnp.repeat(jnp.swapaxes(sg_b.astype(F32), 1, 2), SG_GW, axis=-1)
    score_bound = (SCORE_BOUND_COEF * jnp.max(jnp.abs(att_qn_w.astype(F32)), axis=-1)
                   * jnp.max(jnp.abs(att_kn_w.astype(F32)), axis=-1))
    order = _axial_dim_order()
    w_att = group(COLS_ATT)
    n_qk = ATT_W + ATT_KV_W
    w_qk = w_att[:, :, :n_qk].reshape(DEPTH, D_MODEL, n_qk // ATT_HD, ATT_HD)[..., order].reshape(DEPTH, D_MODEL, n_qk)
    w_att = jnp.concatenate([w_qk, w_att[:, :, n_qk:]], axis=-1)
    with_partner = lambda w: jnp.stack([w[:, order], jnp.roll(w[:, order], ATT_HD // 2, axis=-1)], axis=1).astype(F32)
    att_qn_w = with_partner(att_qn_w)
    att_kn_w = with_partner(att_kn_w)
    return dict(
        att_bounded=(score_bound <= SAFE_SCORE_BOUND).astype(jnp.int32).reshape(DEPTH, 1),
        in_ln_w=vec(in_ln_w), in_ln_b=vec(in_ln_b),
        w_ret=group(COLS_RET), w_mid=group(COLS_MID), w_att=w_att, w_gate=group(COLS_GATE),
        b_gate=vec(b_gate),
        dec=jnp.stack([_decay_tables(ret_decay_f[l], ret_decay_b[l]) for l in range(DEPTH)]),
        ret_gn_w=vec(ret_gn_w), ret_wo=ret_wo.astype(BF16),
        sg_ln_w=vec(sg_ln_w), sg_ln_b=vec(sg_ln_b), sg_ws=sg_ws.astype(BF16), sg_bias=sg_bias,
        sg_wo=sg_wo.astype(BF16),
        att_qn_w=att_qn_w, att_kn_w=att_kn_w, att_wo=att_wo.astype(BF16),
        w_out=w_out.astype(BF16), ln_w=vec(ln_w), ln_b=vec(ln_b),
        xa_wq=xa_wq.astype(BF16), xa_wkv=xa_wkv.astype(BF16), xa_wo=xa_wo.astype(BF16),
        ffn_wa=ffn_w_in[:, :, :D_FF].astype(BF16), ffn_wb=ffn_w_in[:, :, D_FF:].astype(BF16),
        ffn_w_out=ffn_w_out.astype(BF16),
    )


def kernel(x_prompt, x_sample, mem_prompt, mem_sample, in_ln_w, in_ln_b, w_in, b_gate, ret_decay_f, ret_decay_b,
           ret_gn_w, ret_wo, sg_ln_w, sg_ln_b, sg_ws, sg_b, sg_wo, att_qn_w, att_kn_w, att_wo, w_out, ln_w, ln_b,
           xa_wq, xa_wkv, xa_wo, ffn_w_in, ffn_w_out):
    p = _prepare_params(in_ln_w, in_ln_b, w_in, b_gate, ret_decay_f, ret_decay_b, ret_gn_w, ret_wo, sg_ln_w,
                        sg_ln_b, sg_ws, sg_b, sg_wo, att_qn_w, att_kn_w, att_wo, w_out, ln_w, ln_b, xa_wq, xa_wkv,
                        xa_wo, ffn_w_in, ffn_w_out)
    return (_encoder(x_prompt, mem_prompt, p), _encoder(x_sample, mem_sample, p))
```

```python
import functools

import jax
import jax.numpy as jnp
from jax import lax
from jax.experimental import pallas as pl
from jax.experimental.pallas import tpu as pltpu

F32 = jnp.float32
BF16 = jnp.bfloat16

D_MODEL = 1024
DEPTH = 4
N_MEM = 256
GRID_W = 64
CHUNK = 128
RET_HEADS = 8
RET_DK = 128
RET_W = RET_HEADS * RET_DK
SG_GROUPS = 4
SG_GW = 256
SG_W = SG_GROUPS * SG_GW
ATT_HEADS = 8
ATT_KV_HEADS = 2
ATT_GROUP = ATT_HEADS // ATT_KV_HEADS
ATT_HD = 128
ATT_W = ATT_HEADS * ATT_HD
ATT_KV_W = ATT_KV_HEADS * ATT_HD
X_HEADS = 4
X_HD = D_MODEL // X_HEADS
D_FF = 2816
N_BRANCH = 3
ALPHA = (2 * DEPTH) ** 0.25
ROPE_BASE = 10000.0
LN_EPS = 1e-5
RMS_EPS = 1e-6
LOG2E = 1.4426950408889634

COLS_RET = (0, 2 * RET_W)
COLS_MID = (COLS_RET[1], COLS_RET[1] + 2 * RET_W + 2 * SG_W)
COLS_ATT = (COLS_MID[1], COLS_MID[1] + ATT_W + 2 * ATT_KV_W)
COLS_GATE = (COLS_ATT[1], COLS_ATT[1] + N_BRANCH * D_MODEL)

MXU_W = 256
FF_CHUNK = MXU_W
SLAB = 512
TOKEN_TILE = 1024
MERGE_TILE = 512
ATT_TQ = 512
ATT_TK = 2048
ATT_SINGLE_KV_MAX = 2048
RET_UNROLL = 8
RET_DOUBLE_BUFFER_MAX_BYTES = 1024 * 1024
SCORE_BOUND_COEF = ATT_HD ** 0.5 * LOG2E
SAFE_SCORE_BOUND = 64.0
VMEM_LIMIT = 52 * 1024 * 1024


def _cparams(sem):
    return pltpu.CompilerParams(dimension_semantics=sem, vmem_limit_bytes=VMEM_LIMIT)


def _const_spec(shape):
    nd = len(shape)
    return pl.BlockSpec(shape, lambda *_: (0,) * nd, pipeline_mode=pl.Buffered(1))


def _tile(n, pref):
    t = min(n, pref)
    assert n % t == 0, (n, t)
    return t


def _layer_norm_rows(z, w, b):
    mu = jnp.mean(z, axis=-1, keepdims=True)
    d = z - mu
    var = jnp.mean(d * d, axis=-1, keepdims=True)
    return d * lax.rsqrt(var + LN_EPS) * w + b


def _gelu_tanh(x):
    return 0.5 * x * (1.0 + jnp.tanh(0.7978845608028654 * (x + 0.044715 * (x * x * x))))


def _sigmoid(x):
    return 1.0 / (1.0 + jnp.exp(-x))


def _dot(a, b):
    return jnp.dot(a, b, preferred_element_type=F32)


def _in_ln_kernel(x_ref, w_ref, b_ref, xo_ref, xb_ref):
    y = _layer_norm_rows(x_ref[...], w_ref[...], b_ref[...])
    xo_ref[...] = y
    xb_ref[...] = y.astype(BF16)


def _in_ln(x2, w, b, tm):
    T = x2.shape[0]
    row = pl.BlockSpec((tm, D_MODEL), lambda i: (i, 0))
    return pl.pallas_call(
        _in_ln_kernel,
        grid=(T // tm,),
        in_specs=[row, _const_spec((1, D_MODEL)), _const_spec((1, D_MODEL))],
        out_specs=[row, row],
        out_shape=[jax.ShapeDtypeStruct((T, D_MODEL), F32), jax.ShapeDtypeStruct((T, D_MODEL), BF16)],
        compiler_params=_cparams(("parallel",)),
        name="in_ln",
    )(x2, w, b)


def _mm_kernel(x_ref, w_ref, o_ref):
    o_ref[...] = _dot(x_ref[...], w_ref[...]).astype(o_ref.dtype)


def _matmul(x, w, tm, tn, name):
    M, K = x.shape
    N = w.shape[1]
    return pl.pallas_call(
        _mm_kernel,
        grid=(N // tn, M // tm),
        in_specs=[pl.BlockSpec((tm, K), lambda j, i: (i, 0)), pl.BlockSpec((K, tn), lambda j, i: (0, j))],
        out_specs=pl.BlockSpec((tm, tn), lambda j, i: (i, j)),
        out_shape=jax.ShapeDtypeStruct((M, N), BF16),
        compiler_params=_cparams(("parallel", "parallel")),
        name=name,
    )(x, w)


def _rope(x, cos, sin_signed, partner):
    return x * cos + partner * sin_signed


def _proj_ret_kernel(x_ref, w_ref, rc_ref, rs_ref, rq_ref, rkt_ref, *, tm):
    x = x_ref[0]
    rc = rc_ref[...]
    rs = rs_ref[...]
    k_scale = RET_DK ** -0.5

    def rope(t):
        return _rope(t, rc, rs, pltpu.roll(t, RET_DK // 2, axis=1))

    heads_per_slab = MXU_W // RET_DK
    for slab in range(RET_W // MXU_W):
        q2 = _dot(x, w_ref[:, slab * MXU_W:(slab + 1) * MXU_W])
        k2 = _dot(x, w_ref[:, RET_W + slab * MXU_W:RET_W + (slab + 1) * MXU_W])
        for j in range(heads_per_slab):
            h = slab * heads_per_slab + j
            sl = slice(j * RET_DK, (j + 1) * RET_DK)
            rq_ref[0, :, h * RET_DK:(h + 1) * RET_DK] = rope(q2[:, sl]).astype(BF16)
            kt = (rope(k2[:, sl]) * k_scale).T
            for c in range(tm // CHUNK):
                rkt_ref[0, h, c] = kt[:, c * CHUNK:(c + 1) * CHUNK].astype(BF16)


def _proj_ret(xb3, w, rc, rs, tm):
    B, S, _ = xb3.shape
    tab = pl.BlockSpec((tm, RET_DK), lambda b, s: (s, 0))
    return pl.pallas_call(
        functools.partial(_proj_ret_kernel, tm=tm),
        grid=(B, S // tm),
        in_specs=[pl.BlockSpec((1, tm, D_MODEL), lambda b, s: (b, s, 0)), _const_spec((D_MODEL, 2 * RET_W)), tab, tab],
        out_specs=[pl.BlockSpec((1, tm, RET_W), lambda b, s: (b, s, 0)),
                   pl.BlockSpec((1, RET_HEADS, tm // CHUNK, RET_DK, CHUNK), lambda b, s: (b, 0, s, 0, 0))],
        out_shape=[jax.ShapeDtypeStruct((B, S, RET_W), BF16),
                   jax.ShapeDtypeStruct((B, RET_HEADS, S // CHUNK, RET_DK, CHUNK), BF16)],
        compiler_params=_cparams(("parallel", "parallel")),
        name="proj_ret",
    )(xb3, w, rc, rs)


def _proj_mid_kernel(x_ref, w_ref, lnw_ref, lnb_ref, ws_ref, bias_ref, rv_ref, rgs_ref, sg_ref, vn_ref, *, tm):
    x = x_ref[0]
    for i in range(RET_W // SLAB):
        cols = slice(i * SLAB, (i + 1) * SLAB)
        rv_ref[0, :, cols] = _dot(x, w_ref[:, cols]).astype(BF16)
        g = _dot(x, w_ref[:, RET_W + i * SLAB:RET_W + (i + 1) * SLAB])
        rgs_ref[0, :, cols] = (g * _sigmoid(g)).astype(BF16)
    sv = _gelu_tanh(_dot(x, w_ref[:, 2 * RET_W + SG_W:2 * RET_W + 2 * SG_W]))
    vn_ref[...] = _layer_norm_rows(sv, lnw_ref[...], lnb_ref[...]).astype(BF16)
    for grp in range(SG_GROUPS):
        cols = slice(grp * SG_GW, (grp + 1) * SG_GW)
        u = _gelu_tanh(_dot(x, w_ref[:, 2 * RET_W + grp * SG_GW:2 * RET_W + (grp + 1) * SG_GW]))
        for c in range(tm // CHUNK):
            rows = slice(c * CHUNK, (c + 1) * CHUNK)
            mixed = _dot(ws_ref[grp], vn_ref[rows, cols]) + bias_ref[:, cols]
            sg_ref[0, rows, cols] = (u[rows, :] * mixed).astype(BF16)


def _proj_mid(xb3, w, lnw, lnb, ws, bias, tm):
    B, S, _ = xb3.shape
    row = pl.BlockSpec((1, tm, D_MODEL), lambda b, s: (b, s, 0))
    out = jax.ShapeDtypeStruct((B, S, D_MODEL), BF16)
    return pl.pallas_call(
        functools.partial(_proj_mid_kernel, tm=tm),
        grid=(B, S // tm),
        in_specs=[row, _const_spec((D_MODEL, 2 * RET_W + 2 * SG_W)), _const_spec((1, SG_W)), _const_spec((1, SG_W)),
                  _const_spec((SG_GROUPS, CHUNK, CHUNK)), _const_spec((CHUNK, SG_W))],
        out_specs=[row, row, row],
        out_shape=[out, out, out],
        scratch_shapes=[pltpu.VMEM((tm, SG_W), BF16)],
        compiler_params=_cparams(("parallel", "parallel")),
        name="proj_mid",
    )(xb3, w, lnw, lnb, ws, bias)


def _proj_att_gate_kernel(x_ref, w_ref, wg_ref, bg_ref, ac_ref, as_ref, qnw_ref, knw_ref,
                          aq_ref, akt_ref, av_ref, gate_ref, *, tm, tq):
    x = x_ref[0]

    n_gate_slabs = N_BRANCH * D_MODEL // MXU_W
    gate_slabs_done = []

    def gate_slab():
        i = len(gate_slabs_done)
        gate_slabs_done.append(i)
        cols = slice(i * MXU_W, (i + 1) * MXU_W)
        gate_ref[0, :, cols] = _sigmoid(_dot(x, wg_ref[:, cols]) + bg_ref[:, cols]).astype(BF16)

    n_q_slabs = ATT_W // MXU_W
    q_scale = ATT_HD ** -0.5 * LOG2E
    q_cos = ac_ref[...] * (qnw_ref[0:1, :] * q_scale)
    q_sin = as_ref[...] * (qnw_ref[1:2, :] * q_scale)
    k_cos = ac_ref[...] * knw_ref[0:1, :]
    k_sin = as_ref[...] * knw_ref[1:2, :]

    def norm_rope(t, cos, sin):
        r = lax.rsqrt(jnp.mean(t * t, axis=-1, keepdims=True) + RMS_EPS)
        return r * (t * cos + pltpu.roll(t, ATT_HD // 2, axis=1) * sin)

    heads_per_slab = MXU_W // ATT_HD
    for slab in range(n_q_slabs):
        q2 = _dot(x, w_ref[:, slab * MXU_W:(slab + 1) * MXU_W])
        for j in range(heads_per_slab):
            kv_head, g = divmod(slab * heads_per_slab + j, ATT_GROUP)
            for c in range(tm // CHUNK):
                rows = slice(c * CHUNK, (c + 1) * CHUNK)
                t = q2[rows, j * ATT_HD:(j + 1) * ATT_HD]
                q_tile, r0 = divmod(c * CHUNK, tq)
                aq_ref[0, kv_head, q_tile, g * tq + r0:g * tq + r0 + CHUNK, :] = (
                    norm_rope(t, q_cos[rows], q_sin[rows]).astype(BF16))
            gate_slab()
    k2 = _dot(x, w_ref[:, ATT_W:ATT_W + ATT_KV_W])
    for j in range(ATT_KV_HEADS):
        for c in range(tm // CHUNK):
            rows = slice(c * CHUNK, (c + 1) * CHUNK)
            t = k2[rows, j * ATT_HD:(j + 1) * ATT_HD]
            akt_ref[0, j, :, rows] = norm_rope(t, k_cos[rows], k_sin[rows]).T.astype(BF16)
        gate_slab()
    av_ref[0] = _dot(x, w_ref[:, ATT_W + ATT_KV_W:ATT_W + 2 * ATT_KV_W]).astype(BF16)
    while len(gate_slabs_done) < n_gate_slabs:
        gate_slab()


def _proj_att_gate(xb3, w, wg, bg, ac, as_, qnw, knw, tm, tq):
    B, S, _ = xb3.shape
    assert tm % tq == 0, (tm, tq)
    n_gate = N_BRANCH * D_MODEL
    tab = pl.BlockSpec((tm, ATT_HD), lambda b, s: (s, 0))
    return pl.pallas_call(
        functools.partial(_proj_att_gate_kernel, tm=tm, tq=tq),
        grid=(B, S // tm),
        in_specs=[pl.BlockSpec((1, tm, D_MODEL), lambda b, s: (b, s, 0)), _const_spec((D_MODEL, ATT_W + 2 * ATT_KV_W)),
                  _const_spec((D_MODEL, n_gate)), _const_spec((1, n_gate)),
                  tab, tab, _const_spec((2, ATT_HD)), _const_spec((2, ATT_HD))],
        out_specs=[pl.BlockSpec((1, ATT_KV_HEADS, tm // tq, ATT_GROUP * tq, ATT_HD), lambda b, s: (b, 0, s, 0, 0)),
                   pl.BlockSpec((1, ATT_KV_HEADS, ATT_HD, tm), lambda b, s: (b, 0, 0, s)),
                   pl.BlockSpec((1, tm, ATT_KV_W), lambda b, s: (b, s, 0)),
                   pl.BlockSpec((1, tm, n_gate), lambda b, s: (b, s, 0))],
        out_shape=[jax.ShapeDtypeStruct((B, ATT_KV_HEADS, S // tq, ATT_GROUP * tq, ATT_HD), BF16),
                   jax.ShapeDtypeStruct((B, ATT_KV_HEADS, ATT_HD, S), BF16),
                   jax.ShapeDtypeStruct((B, S, ATT_KV_W), BF16),
                   jax.ShapeDtypeStruct((B, S, n_gate), BF16)],
        compiler_params=_cparams(("parallel", "parallel")),
        name="proj_att_gate",
    )(xb3, w, wg, bg, ac, as_, qnw, knw)


def _ret_kernel(q_ref, kt_ref, v_ref, g_ref, dec_ref, gnw_ref, o_ref, st_ref, p_ref, tot_ref, *, n_chunks, unroll):
    dtot = dec_ref[0, 0]
    xif, xib = dec_ref[0, 1], dec_ref[0, 2]
    zf, zb = dec_ref[0, 3], dec_ref[0, 4]
    cdf, cdb = dec_ref[0, 5], dec_ref[0, 6]
    gnw = gnw_ref[...]

    def rows_of(c):
        return pl.ds(pl.multiple_of(c * CHUNK, CHUNK), CHUNK)

    def scan(i, carry):
        sf, sb = carry
        cf, cb = i, n_chunks - 1 - i
        st_ref[cf, :, :RET_DK] = sf.astype(BF16)
        st_ref[cb, :, RET_DK:] = sb.astype(BF16)
        kzf = (kt_ref[0, 0, cf].astype(F32) * zf).astype(BF16)
        kzb = (kt_ref[0, 0, cb].astype(F32) * zb).astype(BF16)
        sf = sf * cdf + _dot(kzf, v_ref[0, rows_of(cf), :])
        sb = sb * cdb + _dot(kzb, v_ref[0, rows_of(cb), :])
        return sf, sb

    zero = jnp.zeros((RET_DK, RET_DK), F32)
    lax.fori_loop(0, n_chunks, scan, (zero, zero), unroll=unroll)

    def decayed_scores(c, carry):
        s = _dot(q_ref[0, rows_of(c), :], kt_ref[0, 0, c])
        p_ref[c] = (s * dtot).astype(BF16)
        return carry

    lax.fori_loop(0, n_chunks, decayed_scores, 0, unroll=unroll)

    def mix(c, carry):
        rows = rows_of(c)
        inter = _dot(q_ref[0, rows, :], st_ref[c])
        tot_ref[rows, :] = (_dot(p_ref[c], v_ref[0, rows, :]) + inter[:, :RET_DK] * xif + inter[:, RET_DK:] * xib)
        return carry

    lax.fori_loop(0, n_chunks, mix, 0, unroll=unroll)

    def norm_gate(c, carry):
        rows = rows_of(c)
        tot = tot_ref[rows, :]
        mu = jnp.mean(tot, axis=-1, keepdims=True)
        var = jnp.maximum(jnp.mean(tot * tot, axis=-1, keepdims=True) - mu * mu, 0.0)
        y = (tot - mu) * lax.rsqrt(var + LN_EPS) * gnw
        o_ref[0, rows, :] = (y * g_ref[0, rows, :].astype(F32)).astype(BF16)
        return carry

    lax.fori_loop(0, n_chunks, norm_gate, 0, unroll=unroll)


def _retention(rq, rkt, rv, rgs, dec, gnw):
    B, S, _ = rq.shape
    n = S // CHUNK
    mode = pl.Buffered(1) if S * RET_DK * 2 > RET_DOUBLE_BUFFER_MAX_BYTES else None
    head = lambda pm=None: pl.BlockSpec((1, S, RET_DK), lambda b, h: (b, 0, h), pipeline_mode=pm)
    return pl.pallas_call(
        functools.partial(_ret_kernel, n_chunks=n, unroll=min(RET_UNROLL, n)),
        grid=(B, RET_HEADS),
        in_specs=[head(mode),
                  pl.BlockSpec((1, 1, n, RET_DK, CHUNK), lambda b, h: (b, h, 0, 0, 0), pipeline_mode=mode),
                  head(mode), head(mode),
                  pl.BlockSpec((1, 7, CHUNK, CHUNK), lambda b, h: (h, 0, 0, 0)),
                  pl.BlockSpec((1, RET_DK), lambda b, h: (0, h))],
        out_specs=head(),
        out_shape=jax.ShapeDtypeStruct((B, S, RET_W), BF16),
        scratch_shapes=[pltpu.VMEM((n, RET_DK, 2 * RET_DK), BF16), pltpu.VMEM((n, CHUNK, CHUNK), BF16),
                        pltpu.VMEM((S, RET_DK), F32)],
        compiler_params=_cparams(("parallel", "parallel")),
        name="retention",
    )(rq, rkt, rv, rgs, dec, gnw)


def _attn_kernel(bounded_ref, q_ref, kt_ref, v_ref, o_ref, *scratch, tq, tk, single_kv):
    q = q_ref[0, 0, 0]
    kt = kt_ref[0, 0]
    v1 = jnp.concatenate([v_ref[0], jnp.ones((tk, ATT_HD), BF16)], axis=1)
    bounded = bounded_ref[0] == 1

    def write(pv):
        o = pv[:, :ATT_HD] / pv[:, ATT_HD:]
        for g in range(ATT_GROUP):
            o_ref[0, :, g * ATT_HD:(g + 1) * ATT_HD] = o[g * tq:(g + 1) * tq].astype(BF16)

    if single_kv:
        @pl.when(bounded)
        def _():
            write(_dot(jnp.exp2(_dot(q, kt)).astype(BF16), v1))

        @pl.when(jnp.logical_not(bounded))
        def _():
            s = _dot(q, kt)
            write(_dot(jnp.exp2(s - jnp.max(s, axis=1, keepdims=True)).astype(BF16), v1))

        return

    m_ref, acc_ref = scratch
    ki = pl.program_id(3)

    @pl.when(ki == 0)
    def _():
        m_ref[...] = jnp.full(m_ref.shape, -jnp.inf, F32)
        acc_ref[...] = jnp.zeros(acc_ref.shape, F32)

    @pl.when(bounded)
    def _():
        acc_ref[...] += _dot(jnp.exp2(_dot(q, kt)).astype(BF16), v1)

    @pl.when(jnp.logical_not(bounded))
    def _():
        s = _dot(q, kt)
        m_prev = m_ref[...]
        m_next = jnp.maximum(m_prev, jnp.max(s, axis=1, keepdims=True))
        p = jnp.exp2(s - jnp.tile(m_next, (1, tk // ATT_HD)))
        alpha = jnp.exp2(m_prev - m_next)
        acc_ref[...] = jnp.tile(alpha, (1, 2)) * acc_ref[...] + _dot(p.astype(BF16), v1)
        m_ref[...] = m_next

    @pl.when(ki == pl.num_programs(3) - 1)
    def _():
        write(acc_ref[...])


def _attention(bounded, aq, akt, av):
    B, S, _ = av.shape
    tq = aq.shape[3] // ATT_GROUP
    single_kv = S <= ATT_SINGLE_KV_MAX
    tk = S if single_kv else _tile(S, ATT_TK)
    rows = ATT_GROUP * tq
    scratch = [] if single_kv else [pltpu.VMEM((rows, ATT_HD), F32), pltpu.VMEM((rows, 2 * ATT_HD), F32)]
    return pl.pallas_call(
        functools.partial(_attn_kernel, tq=tq, tk=tk, single_kv=single_kv),
        grid=(B, ATT_KV_HEADS, S // tq, S // tk),
        in_specs=[pl.BlockSpec(memory_space=pltpu.SMEM),
                  pl.BlockSpec((1, 1, 1, ATT_GROUP * tq, ATT_HD), lambda b, k, i, j: (b, k, i, 0, 0)),
                  pl.BlockSpec((1, 1, ATT_HD, tk), lambda b, k, i, j: (b, k, 0, j)),
                  pl.BlockSpec((1, tk, ATT_HD), lambda b, k, i, j: (b, j, k))],
        out_specs=pl.BlockSpec((1, tq, ATT_GROUP * ATT_HD), lambda b, k, i, j: (b, i, k)),
        out_shape=jax.ShapeDtypeStruct((B, S, ATT_W), BF16),
        scratch_shapes=scratch,
        compiler_params=_cparams(("parallel", "parallel", "parallel", "arbitrary")),
        name="gqa_attention",
    )(bounded, aq, akt, av)


def _merge_kernel(ro_ref, sg_ref, at_ref, gate_ref, x_ref, wr_ref, ws_ref, wa_ref, wo_ref, lnw_ref, lnb_ref,
                  xo_ref, xb_ref):
    def branch(a_ref, w_ref, i):
        return gate_ref[:, i * D_MODEL:(i + 1) * D_MODEL].astype(F32) * _dot(a_ref[...], w_ref[...])

    merged = branch(ro_ref, wr_ref, 0) + branch(sg_ref, ws_ref, 1) + branch(at_ref, wa_ref, 2)
    y = _dot(merged.astype(BF16), wo_ref[...])
    out = _layer_norm_rows(ALPHA * x_ref[...] + y, lnw_ref[...], lnb_ref[...])
    xo_ref[...] = out
    xb_ref[...] = out.astype(BF16)


def _merge(ro, sg, at, gates, x, wr, ws, wa, wo, lnw, lnb, tm):
    T = x.shape[0]
    row = pl.BlockSpec((tm, D_MODEL), lambda i: (i, 0))
    wspec = _const_spec((D_MODEL, D_MODEL))
    vec = _const_spec((1, D_MODEL))
    return pl.pallas_call(
        _merge_kernel,
        grid=(T // tm,),
        in_specs=[row, row, row, pl.BlockSpec((tm, N_BRANCH * D_MODEL), lambda i: (i, 0)), row,
                  wspec, wspec, wspec, wspec, vec, vec],
        out_specs=[row, row],
        out_shape=[jax.ShapeDtypeStruct((T, D_MODEL), F32), jax.ShapeDtypeStruct((T, D_MODEL), BF16)],
        compiler_params=_cparams(("parallel",)),
        name="merge_out_ln",
    )(ro, sg, at, gates, x, wr, ws, wa, wo, lnw, lnb)


def _cross_kernel(x_ref, xb_ref, kv_ref, wq_ref, wo_ref, lnw_ref, lnb_ref, xo_ref, xbo_ref, o_scr):
    q = _dot(xb_ref[0], wq_ref[...]).astype(BF16)
    scale = X_HD ** -0.5
    for h in range(X_HEADS):
        cols = slice(h * X_HD, (h + 1) * X_HD)
        k = kv_ref[0, :, cols]
        v = kv_ref[0, :, D_MODEL + h * X_HD:D_MODEL + (h + 1) * X_HD]
        s = lax.dot_general(q[:, cols], k, (((1,), (1,)), ((), ())), preferred_element_type=F32) * scale
        p = jnp.exp(s - jnp.max(s, axis=-1, keepdims=True))
        o = _dot(p.astype(BF16), v) / jnp.sum(p, axis=-1, keepdims=True)
        o_scr[:, cols] = o.astype(BF16)
    y = _dot(o_scr[...], wo_ref[...])
    out = _layer_norm_rows(ALPHA * x_ref[0] + y, lnw_ref[...], lnb_ref[...])
    xo_ref[0] = out
    xbo_ref[0] = out.astype(BF16)


def _cross(x3, xb3, kv3, wq, wo, lnw, lnb, tm):
    B, S, _ = x3.shape
    row = pl.BlockSpec((1, tm, D_MODEL), lambda b, i: (b, i, 0))
    wspec = _const_spec((D_MODEL, D_MODEL))
    vec = _const_spec((1, D_MODEL))
    return pl.pallas_call(
        _cross_kernel,
        grid=(B, S // tm),
        in_specs=[row, row, pl.BlockSpec((1, N_MEM, 2 * D_MODEL), lambda b, i: (b, 0, 0)), wspec, wspec, vec, vec],
        out_specs=[row, row],
        out_shape=[jax.ShapeDtypeStruct((B, S, D_MODEL), F32), jax.ShapeDtypeStruct((B, S, D_MODEL), BF16)],
        scratch_shapes=[pltpu.VMEM((tm, D_MODEL), BF16)],
        compiler_params=_cparams(("parallel", "parallel")),
        name="cross_attn_ln",
    )(x3, xb3, kv3, wq, wo, lnw, lnb)


def _ffn_kernel(x_ref, xb_ref, wa_ref, wb_ref, wo_ref, lnw_ref, lnb_ref, xo_ref, xbo_ref, h_scr):
    xb = xb_ref[...]
    for j in range(D_FF // FF_CHUNK):
        cols = slice(j * FF_CHUNK, (j + 1) * FF_CHUNK)
        a = _dot(xb, wa_ref[:, cols])
        b = _dot(xb, wb_ref[:, cols])
        h_scr[:, cols] = (a * _sigmoid(a) * b).astype(BF16)
    y = _dot(h_scr[...], wo_ref[...])
    out = _layer_norm_rows(ALPHA * x_ref[...] + y, lnw_ref[...], lnb_ref[...])
    xo_ref[...] = out
    xbo_ref[...] = out.astype(BF16)


def _ffn(x, xb, wa, wb, wo, lnw, lnb, tm):
    T = x.shape[0]
    row = pl.BlockSpec((tm, D_MODEL), lambda i: (i, 0))
    vec = _const_spec((1, D_MODEL))
    return pl.pallas_call(
        _ffn_kernel,
        grid=(T // tm,),
        in_specs=[row, row, _const_spec((D_MODEL, D_FF)), _const_spec((D_MODEL, D_FF)),
                  _const_spec((D_FF, D_MODEL)), vec, vec],
        out_specs=[row, row],
        out_shape=[jax.ShapeDtypeStruct((T, D_MODEL), F32), jax.ShapeDtypeStruct((T, D_MODEL), BF16)],
        scratch_shapes=[pltpu.VMEM((tm, D_FF), BF16)],
        compiler_params=_cparams(("parallel",)),
        name="swiglu_ln",
    )(x, xb, wa, wb, wo, lnw, lnb)


def _rope_tables(S):
    t = jnp.arange(S)

    def cos_sin(pos, dim):
        inv_freq = ROPE_BASE ** (-jnp.arange(0, dim, 2, dtype=F32) / dim)
        ang = pos.astype(F32)[:, None] * inv_freq[None, :]
        return jnp.cos(ang), jnp.sin(ang)

    ct, st = cos_sin(t, RET_DK)
    cr, sr = cos_sin(t // GRID_W, ATT_HD // 2)
    cc, sc = cos_sin(t % GRID_W, ATT_HD // 2)
    return (jnp.concatenate([ct, ct], -1), jnp.concatenate([-st, st], -1),
            jnp.concatenate([cr, cc, cr, cc], -1), jnp.concatenate([-sr, -sc, sr, sc], -1))


def _axial_dim_order():
    quarter = ATT_HD // 4
    blocks = [jnp.arange(i * quarter, (i + 1) * quarter) for i in (0, 2, 1, 3)]
    return jnp.concatenate(blocks)


def _decay_tables(decay_f, decay_b):
    lgf = jax.nn.log_sigmoid(decay_f.astype(F32))[:, None, None]
    lgb = jax.nn.log_sigmoid(decay_b.astype(F32))[:, None, None]
    idx = jnp.arange(CHUNK, dtype=F32)
    diff = idx[:, None] - idx[None, :]
    dtot = jnp.where(diff >= 0, jnp.exp(lgf * jnp.maximum(diff, 0.0)), jnp.exp(lgb * jnp.maximum(-diff, 0.0)))
    ones = jnp.ones((CHUNK, CHUNK), F32)
    row = idx[None, :, None] * ones
    lane = idx[None, None, :] * ones
    xif = jnp.exp(lgf * (row + 1.0))
    xib = jnp.exp(lgb * (CHUNK - row))
    zf = jnp.exp(lgf * (CHUNK - 1.0 - lane))
    zb = jnp.exp(lgb * lane)
    cdf = jnp.exp(lgf * CHUNK) * ones
    cdb = jnp.exp(lgb * CHUNK) * ones
    return jnp.stack([dtot, xif, xib, zf, zb, cdf, cdb], axis=1)


def _encoder(x, mem, p):
    B, S, _ = x.shape
    T = B * S
    tm = _tile(T, TOKEN_TILE)
    ts = _tile(S, TOKEN_TILE)
    rc, rs, ac, as_ = _rope_tables(S)
    xf, xb = _in_ln(x.reshape(T, D_MODEL), p["in_ln_w"], p["in_ln_b"], tm)
    memb = mem.astype(BF16).reshape(B * N_MEM, D_MODEL)
    flat = lambda a: a.reshape(T, D_MODEL)
    seq = lambda a: a.reshape(B, S, D_MODEL)
    for l in range(DEPTH):
        xb3 = seq(xb)
        rq, rkt = _proj_ret(xb3, p["w_ret"][l], rc, rs, ts)
        rv, rgs, sg = _proj_mid(xb3, p["w_mid"][l], p["sg_ln_w"][l], p["sg_ln_b"][l], p["sg_ws"][l], p["sg_bias"][l], ts)
        aq, akt, av, gates = _proj_att_gate(xb3, p["w_att"][l], p["w_gate"][l], p["b_gate"][l], ac, as_,
                                            p["att_qn_w"][l], p["att_kn_w"][l], ts, _tile(S, ATT_TQ))
        gates = gates.reshape(T, N_BRANCH * D_MODEL)
        ro = _retention(rq, rkt, rv, rgs, p["dec"][l], p["ret_gn_w"][l])
        at = _attention(p["att_bounded"][l], aq, akt, av)
        xf, xb = _merge(flat(ro), flat(sg), flat(at), gates, xf, p["ret_wo"][l], p["sg_wo"][l], p["att_wo"][l],
                        p["w_out"][l], p["ln_w"][l, 0], p["ln_b"][l, 0], _tile(T, MERGE_TILE))
        kv = _matmul(memb, p["xa_wkv"][l], _tile(B * N_MEM, 1024), 1024, "proj_kv")
        xf3, xb3 = _cross(seq(xf), seq(xb), kv.reshape(B, N_MEM, 2 * D_MODEL), p["xa_wq"][l], p["xa_wo"][l],
                          p["ln_w"][l, 1], p["ln_b"][l, 1], ts)
        xf, xb = _ffn(flat(xf3), flat(xb3), p["ffn_wa"][l], p["ffn_wb"][l], p["ffn_w_out"][l],
                      p["ln_w"][l, 2], p["ln_b"][l, 2], tm)
    return seq(xf)


def _prepare_params(in_ln_w, in_ln_b, w_in, b_gate, ret_decay_f, ret_decay_b, ret_gn_w, ret_wo, sg_ln_w, sg_ln_b,
                    sg_ws, sg_b, sg_wo, att_qn_w, att_kn_w, att_wo, w_out, ln_w, ln_b, xa_wq, xa_wkv, xa_wo,
                    ffn_w_in, ffn_w_out):
    vec = lambda a: a.astype(F32).reshape(a.shape[:-1] + (1, a.shape[-1]))
    group = lambda c: w_in[:, :, c[0]:c[1]].astype(BF16)
    sg_bias = jnp.repeat(jnp.swapaxes(sg_b.astype(F32), 1, 2), SG_GW, axis=-1)
    score_bound = (SCORE_BOUND_COEF * jnp.max(jnp.abs(att_qn_w.astype(F32)), axis=-1)
                   * jnp.max(jnp.abs(att_kn_w.astype(F32)), axis=-1))
    order = _axial_dim_order()
    w_att = group(COLS_ATT)
    n_qk = ATT_W + ATT_KV_W
    w_qk = w_att[:, :, :n_qk].reshape(DEPTH, D_MODEL, n_qk // ATT_HD, ATT_HD)[..., order].reshape(DEPTH, D_MODEL, n_qk)
    w_att = jnp.concatenate([w_qk, w_att[:, :, n_qk:]], axis=-1)
    with_partner = lambda w: jnp.stack([w[:, order], jnp.roll(w[:, order], ATT_HD // 2, axis=-1)], axis=1).astype(F32)
    att_qn_w = with_partner(att_qn_w)
    att_kn_w = with_partner(att_kn_w)
    return dict(
        att_bounded=(score_bound <= SAFE_SCORE_BOUND).astype(jnp.int32).reshape(DEPTH, 1),
        in_ln_w=vec(in_ln_w), in_ln_b=vec(in_ln_b),
        w_ret=group(COLS_RET), w_mid=group(COLS_MID), w_att=w_att, w_gate=group(COLS_GATE),
        b_gate=vec(b_gate),
        dec=jnp.stack([_decay_tables(ret_decay_f[l], ret_decay_b[l]) for l in range(DEPTH)]),
        ret_gn_w=vec(ret_gn_w), ret_wo=ret_wo.astype(BF16),
        sg_ln_w=vec(sg_ln_w), sg_ln_b=vec(sg_ln_b), sg_ws=sg_ws.astype(BF16), sg_bias=sg_bias,
        sg_wo=sg_wo.astype(BF16),
        att_qn_w=att_qn_w, att_kn_w=att_kn_w, att_wo=att_wo.astype(BF16),
        w_out=w_out.astype(BF16), ln_w=vec(ln_w), ln_b=vec(ln_b),
        xa_wq=xa_wq.astype(BF16), xa_wkv=xa_wkv.astype(BF16), xa_wo=xa_wo.astype(BF16),
        ffn_wa=ffn_w_in[:, :, :D_FF].astype(BF16), ffn_wb=ffn_w_in[:, :, D_FF:].astype(BF16),
        ffn_w_out=ffn_w_out.astype(BF16),
    )


def kernel(x_prompt, x_sample, mem_prompt, mem_sample, in_ln_w, in_ln_b, w_in, b_gate, ret_decay_f, ret_decay_b,
           ret_gn_w, ret_wo, sg_ln_w, sg_ln_b, sg_ws, sg_b, sg_wo, att_qn_w, att_kn_w, att_wo, w_out, ln_w, ln_b,
           xa_wq, xa_wkv, xa_wo, ffn_w_in, ffn_w_out):
    p = _prepare_params(in_ln_w, in_ln_b, w_in, b_gate, ret_decay_f, ret_decay_b, ret_gn_w, ret_wo, sg_ln_w,
                        sg_ln_b, sg_ws, sg_b, sg_wo, att_qn_w, att_kn_w, att_wo, w_out, ln_w, ln_b, xa_wq, xa_wkv,
                        xa_wo, ffn_w_in, ffn_w_out)
    return (_encoder(x_prompt, mem_prompt, p), _encoder(x_sample, mem_sample, p))
```

```python
import functools

import jax
import jax.numpy as jnp
from jax import lax
from jax.experimental import pallas as pl
from jax.experimental.pallas import tpu as pltpu

F32 = jnp.float32
BF16 = jnp.bfloat16

D_MODEL = 1024
DEPTH = 4
N_MEM = 256
GRID_W = 64
CHUNK = 128
RET_HEADS = 8
RET_DK = 128
RET_W = RET_HEADS * RET_DK
SG_GROUPS = 4
SG_GW = 256
SG_W = SG_GROUPS * SG_GW
ATT_HEADS = 8
ATT_KV_HEADS = 2
ATT_GROUP = ATT_HEADS // ATT_KV_HEADS
ATT_HD = 128
ATT_W = ATT_HEADS * ATT_HD
ATT_KV_W = ATT_KV_HEADS * ATT_HD
X_HEADS = 4
X_HD = D_MODEL // X_HEADS
D_FF = 2816
N_BRANCH = 3
ALPHA = (2 * DEPTH) ** 0.25
ROPE_BASE = 10000.0
LN_EPS = 1e-5
RMS_EPS = 1e-6
LOG2E = 1.4426950408889634

COLS_RET = (0, 2 * RET_W)
COLS_MID = (COLS_RET[1], COLS_RET[1] + 2 * RET_W + 2 * SG_W)
COLS_ATT = (COLS_MID[1], COLS_MID[1] + ATT_W + 2 * ATT_KV_W)
COLS_GATE = (COLS_ATT[1], COLS_ATT[1] + N_BRANCH * D_MODEL)

MXU_W = 256
FF_CHUNK = MXU_W
SLAB = 512
TOKEN_TILE = 1024
MERGE_TILE = 512
ATT_TQ = 512
ATT_TK = 2048
ATT_SINGLE_KV_MAX = 2048
RET_UNROLL = 8
RET_DOUBLE_BUFFER_MAX_BYTES = 1024 * 1024
SCORE_BOUND_COEF = ATT_HD ** 0.5 * LOG2E
SAFE_SCORE_BOUND = 64.0
VMEM_LIMIT = 52 * 1024 * 1024


def _cparams(sem):
    return pltpu.CompilerParams(dimension_semantics=sem, vmem_limit_bytes=VMEM_LIMIT)


def _const_spec(shape):
    nd = len(shape)
    return pl.BlockSpec(shape, lambda *_: (0,) * nd, pipeline_mode=pl.Buffered(1))


def _tile(n, pref):
    t = min(n, pref)
    assert n % t == 0, (n, t)
    return t


def _layer_norm_rows(z, w, b):
    mu = jnp.mean(z, axis=-1, keepdims=True)
    d = z - mu
    var = jnp.mean(d * d, axis=-1, keepdims=True)
    return d * lax.rsqrt(var + LN_EPS) * w + b


def _gelu_tanh(x):
    return 0.5 * x * (1.0 + jnp.tanh(0.7978845608028654 * (x + 0.044715 * (x * x * x))))


def _sigmoid(x):
    return 1.0 / (1.0 + jnp.exp(-x))


def _dot(a, b):
    return jnp.dot(a, b, preferred_element_type=F32)


def _in_ln_kernel(x_ref, w_ref, b_ref, xo_ref, xb_ref):
    y = _layer_norm_rows(x_ref[...], w_ref[...], b_ref[...])
    xo_ref[...] = y
    xb_ref[...] = y.astype(BF16)


def _in_ln(x2, w, b, tm):
    T = x2.shape[0]
    row = pl.BlockSpec((tm, D_MODEL), lambda i: (i, 0))
    return pl.pallas_call(
        _in_ln_kernel,
        grid=(T // tm,),
        in_specs=[row, _const_spec((1, D_MODEL)), _const_spec((1, D_MODEL))],
        out_specs=[row, row],
        out_shape=[jax.ShapeDtypeStruct((T, D_MODEL), F32), jax.ShapeDtypeStruct((T, D_MODEL), BF16)],
        compiler_params=_cparams(("parallel",)),
        name="in_ln",
    )(x2, w, b)


def _mm_kernel(x_ref, w_ref, o_ref):
    o_ref[...] = _dot(x_ref[...], w_ref[...]).astype(o_ref.dtype)


def _matmul(x, w, tm, tn, name):
    M, K = x.shape
    N = w.shape[1]
    return pl.pallas_call(
        _mm_kernel,
        grid=(N // tn, M // tm),
        in_specs=[pl.BlockSpec((tm, K), lambda j, i: (i, 0)), pl.BlockSpec((K, tn), lambda j, i: (0, j))],
        out_specs=pl.BlockSpec((tm, tn), lambda j, i: (i, j)),
        out_shape=jax.ShapeDtypeStruct((M, N), BF16),
        compiler_params=_cparams(("parallel", "parallel")),
        name=name,
    )(x, w)


def _rope(x, cos, sin_signed, partner):
    return x * cos + partner * sin_signed


def _proj_ret_kernel(x_ref, w_ref, rc_ref, rs_ref, rq_ref, rkt_ref, *, tm):
    x = x_ref[0]
    rc = rc_ref[...]
    rs = rs_ref[...]
    k_scale = RET_DK ** -0.5

    def rope(t):
        return _rope(t, rc, rs, pltpu.roll(t, RET_DK // 2, axis=1))

    heads_per_slab = MXU_W // RET_DK
    for slab in range(RET_W // MXU_W):
        q2 = _dot(x, w_ref[:, slab * MXU_W:(slab + 1) * MXU_W])
        k2 = _dot(x, w_ref[:, RET_W + slab * MXU_W:RET_W + (slab + 1) * MXU_W])
        for j in range(heads_per_slab):
            h = slab * heads_per_slab + j
            sl = slice(j * RET_DK, (j + 1) * RET_DK)
            rq_ref[0, :, h * RET_DK:(h + 1) * RET_DK] = rope(q2[:, sl]).astype(BF16)
            kt = (rope(k2[:, sl]) * k_scale).T
            for c in range(tm // CHUNK):
                rkt_ref[0, h, c] = kt[:, c * CHUNK:(c + 1) * CHUNK].astype(BF16)


def _proj_ret(xb3, w, rc, rs, tm):
    B, S, _ = xb3.shape
    tab = pl.BlockSpec((tm, RET_DK), lambda b, s: (s, 0))
    return pl.pallas_call(
        functools.partial(_proj_ret_kernel, tm=tm),
        grid=(B, S // tm),
        in_specs=[pl.BlockSpec((1, tm, D_MODEL), lambda b, s: (b, s, 0)), _const_spec((D_MODEL, 2 * RET_W)), tab, tab],
        out_specs=[pl.BlockSpec((1, tm, RET_W), lambda b, s: (b, s, 0)),
                   pl.BlockSpec((1, RET_HEADS, tm // CHUNK, RET_DK, CHUNK), lambda b, s: (b, 0, s, 0, 0))],
        out_shape=[jax.ShapeDtypeStruct((B, S, RET_W), BF16),
                   jax.ShapeDtypeStruct((B, RET_HEADS, S // CHUNK, RET_DK, CHUNK), BF16)],
        compiler_params=_cparams(("parallel", "parallel")),
        name="proj_ret",
    )(xb3, w, rc, rs)


def _proj_mid_kernel(x_ref, w_ref, lnw_ref, lnb_ref, ws_ref, bias_ref, rv_ref, rgs_ref, sg_ref, vn_ref, *, tm):
    x = x_ref[0]
    for i in range(RET_W // SLAB):
        cols = slice(i * SLAB, (i + 1) * SLAB)
        rv_ref[0, :, cols] = _dot(x, w_ref[:, cols]).astype(BF16)
        g = _dot(x, w_ref[:, RET_W + i * SLAB:RET_W + (i + 1) * SLAB])
        rgs_ref[0, :, cols] = (g * _sigmoid(g)).astype(BF16)
    sv = _gelu_tanh(_dot(x, w_ref[:, 2 * RET_W + SG_W:2 * RET_W + 2 * SG_W]))
    vn_ref[...] = _layer_norm_rows(sv, lnw_ref[...], lnb_ref[...]).astype(BF16)
    for grp in range(SG_GROUPS):
        cols = slice(grp * SG_GW, (grp + 1) * SG_GW)
        u = _gelu_tanh(_dot(x, w_ref[:, 2 * RET_W + grp * SG_GW:2 * RET_W + (grp + 1) * SG_GW]))
        for c in range(tm // CHUNK):
            rows = slice(c * CHUNK, (c + 1) * CHUNK)
            mixed = _dot(ws_ref[grp], vn_ref[rows, cols]) + bias_ref[:, cols]
            sg_ref[0, rows, cols] = (u[rows, :] * mixed).astype(BF16)


def _proj_mid(xb3, w, lnw, lnb, ws, bias, tm):
    B, S, _ = xb3.shape
    row = pl.BlockSpec((1, tm, D_MODEL), lambda b, s: (b, s, 0))
    out = jax.ShapeDtypeStruct((B, S, D_MODEL), BF16)
    return pl.pallas_call(
        functools.partial(_proj_mid_kernel, tm=tm),
        grid=(B, S // tm),
        in_specs=[row, _const_spec((D_MODEL, 2 * RET_W + 2 * SG_W)), _const_spec((1, SG_W)), _const_spec((1, SG_W)),
                  _const_spec((SG_GROUPS, CHUNK, CHUNK)), _const_spec((CHUNK, SG_W))],
        out_specs=[row, row, row],
        out_shape=[out, out, out],
        scratch_shapes=[pltpu.VMEM((tm, SG_W), BF16)],
        compiler_params=_cparams(("parallel", "parallel")),
        name="proj_mid",
    )(xb3, w, lnw, lnb, ws, bias)


def _proj_att_gate_kernel(x_ref, w_ref, wg_ref, bg_ref, ac_ref, as_ref, qnw_ref, knw_ref,
                          aq_ref, akt_ref, av_ref, gate_ref, *, tm, tq):
    x = x_ref[0]

    n_gate_slabs = N_BRANCH * D_MODEL // MXU_W
    gate_slabs_done = []

    def gate_slab():
        i = len(gate_slabs_done)
        gate_slabs_done.append(i)
        cols = slice(i * MXU_W, (i + 1) * MXU_W)
        gate_ref[0, :, cols] = _sigmoid(_dot(x, wg_ref[:, cols]) + bg_ref[:, cols]).astype(BF16)

    n_q_slabs = ATT_W // MXU_W
    q_scale = ATT_HD ** -0.5 * LOG2E
    q_cos = ac_ref[...] * (qnw_ref[0:1, :] * q_scale)
    q_sin = as_ref[...] * (qnw_ref[1:2, :] * q_scale)
    k_cos = ac_ref[...] * knw_ref[0:1, :]
    k_sin = as_ref[...] * knw_ref[1:2, :]

    def norm_rope(t, cos, sin):
        r = lax.rsqrt(jnp.mean(t * t, axis=-1, keepdims=True) + RMS_EPS)
        return r * (t * cos + pltpu.roll(t, ATT_HD // 2, axis=1) * sin)

    heads_per_slab = MXU_W // ATT_HD
    for slab in range(n_q_slabs):
        q2 = _dot(x, w_ref[:, slab * MXU_W:(slab + 1) * MXU_W])
        for j in range(heads_per_slab):
            kv_head, g = divmod(slab * heads_per_slab + j, ATT_GROUP)
            for c in range(tm // CHUNK):
                rows = slice(c * CHUNK, (c + 1) * CHUNK)
                t = q2[rows, j * ATT_HD:(j + 1) * ATT_HD]
                q_tile, r0 = divmod(c * CHUNK, tq)
                aq_ref[0, kv_head, q_tile, g * tq + r0:g * tq + r0 + CHUNK, :] = (
                    norm_rope(t, q_cos[rows], q_sin[rows]).astype(BF16))
            gate_slab()
    k2 = _dot(x, w_ref[:, ATT_W:ATT_W + ATT_KV_W])
    for j in range(ATT_KV_HEADS):
        for c in range(tm // CHUNK):
            rows = slice(c * CHUNK, (c + 1) * CHUNK)
            t = k2[rows, j * ATT_HD:(j + 1) * ATT_HD]
            akt_ref[0, j, :, rows] = norm_rope(t, k_cos[rows], k_sin[rows]).T.astype(BF16)
        gate_slab()
    av_ref[0] = _dot(x, w_ref[:, ATT_W + ATT_KV_W:ATT_W + 2 * ATT_KV_W]).astype(BF16)
    while len(gate_slabs_done) < n_gate_slabs:
        gate_slab()


def _proj_att_gate(xb3, w, wg, bg, ac, as_, qnw, knw, tm, tq):
    B, S, _ = xb3.shape
    assert tm % tq == 0, (tm, tq)
    n_gate = N_BRANCH * D_MODEL
    tab = pl.BlockSpec((tm, ATT_HD), lambda b, s: (s, 0))
    return pl.pallas_call(
        functools.partial(_proj_att_gate_kernel, tm=tm, tq=tq),
        grid=(B, S // tm),
        in_specs=[pl.BlockSpec((1, tm, D_MODEL), lambda b, s: (b, s, 0)), _const_spec((D_MODEL, ATT_W + 2 * ATT_KV_W)),
                  _const_spec((D_MODEL, n_gate)), _const_spec((1, n_gate)),
                  tab, tab, _const_spec((2, ATT_HD)), _const_spec((2, ATT_HD))],
        out_specs=[pl.BlockSpec((1, ATT_KV_HEADS, tm // tq, ATT_GROUP * tq, ATT_HD), lambda b, s: (b, 0, s, 0, 0)),
                   pl.BlockSpec((1, ATT_KV_HEADS, ATT_HD, tm), lambda b, s: (b, 0, 0, s)),
                   pl.BlockSpec((1, tm, ATT_KV_W), lambda b, s: (b, s, 0)),
                   pl.BlockSpec((1, tm, n_gate), lambda b, s: (b, s, 0))],
        out_shape=[jax.ShapeDtypeStruct((B, ATT_KV_HEADS, S // tq, ATT_GROUP * tq, ATT_HD), BF16),
                   jax.ShapeDtypeStruct((B, ATT_KV_HEADS, ATT_HD, S), BF16),
                   jax.ShapeDtypeStruct((B, S, ATT_KV_W), BF16),
                   jax.ShapeDtypeStruct((B, S, n_gate), BF16)],
        compiler_params=_cparams(("parallel", "parallel")),
        name="proj_att_gate",
    )(xb3, w, wg, bg, ac, as_, qnw, knw)


def _ret_kernel(q_ref, kt_ref, v_ref, g_ref, dec_ref, gnw_ref, o_ref, st_ref, p_ref, *tot_refs, n_chunks, unroll):
    dtot = dec_ref[0, 0]
    xif, xib = dec_ref[0, 1], dec_ref[0, 2]
    zf, zb = dec_ref[0, 3], dec_ref[0, 4]
    cdf, cdb = dec_ref[0, 5], dec_ref[0, 6]
    gnw = gnw_ref[...]

    def rows_of(c):
        if isinstance(c, int):
            return pl.ds(c * CHUNK, CHUNK)
        return pl.ds(pl.multiple_of(c * CHUNK, CHUNK), CHUNK)

    def scan(i, carry):
        sf, sb = carry
        cf, cb = i, n_chunks - 1 - i
        st_ref[cf, :, :RET_DK] = sf.astype(BF16)
        st_ref[cb, :, RET_DK:] = sb.astype(BF16)
        kzf = (kt_ref[0, 0, cf].astype(F32) * zf).astype(BF16)
        kzb = (kt_ref[0, 0, cb].astype(F32) * zb).astype(BF16)
        sf = sf * cdf + _dot(kzf, v_ref[0, rows_of(cf), :])
        sb = sb * cdb + _dot(kzb, v_ref[0, rows_of(cb), :])
        s = _dot(q_ref[0, rows_of(i), :], kt_ref[0, 0, i])
        p_ref[i] = (s * dtot).astype(BF16)
        return sf, sb

    zero = jnp.zeros((RET_DK, RET_DK), F32)
    lax.fori_loop(0, n_chunks, scan, (zero, zero), unroll=unroll)

    group = len(tot_refs)

    def mix_group(j):
        for k, tot_ref in enumerate(tot_refs):
            c = j * group + k
            rows = rows_of(c)
            inter = _dot(q_ref[0, rows, :], st_ref[c])
            tot_ref[...] = _dot(p_ref[c], v_ref[0, rows, :]) + inter[:, :RET_DK] * xif + inter[:, RET_DK:] * xib

    def norm_group(j):
        for k, tot_ref in enumerate(tot_refs):
            rows = rows_of(j * group + k)
            tot = tot_ref[...]
            mu = jnp.mean(tot, axis=-1, keepdims=True)
            var = jnp.maximum(jnp.mean(tot * tot, axis=-1, keepdims=True) - mu * mu, 0.0)
            y = (tot - mu) * lax.rsqrt(var + LN_EPS) * gnw
            o_ref[0, rows, :] = (y * g_ref[0, rows, :].astype(F32)).astype(BF16)

    mix_group(0)

    def step(j, carry):
        norm_group(j - 1)
        mix_group(j)
        return carry

    n_groups = n_chunks // group
    lax.fori_loop(1, n_groups, step, 0)
    norm_group(n_groups - 1)


def _retention(rq, rkt, rv, rgs, dec, gnw):
    B, S, _ = rq.shape
    n = S // CHUNK
    group = min(RET_UNROLL, n)
    assert n % group == 0, (n, group)
    mode = pl.Buffered(1) if S * RET_DK * 2 > RET_DOUBLE_BUFFER_MAX_BYTES else None
    head = lambda pm=None: pl.BlockSpec((1, S, RET_DK), lambda b, h: (b, 0, h), pipeline_mode=pm)
    return pl.pallas_call(
        functools.partial(_ret_kernel, n_chunks=n, unroll=min(RET_UNROLL, n)),
        grid=(B, RET_HEADS),
        in_specs=[head(mode),
                  pl.BlockSpec((1, 1, n, RET_DK, CHUNK), lambda b, h: (b, h, 0, 0, 0), pipeline_mode=mode),
                  head(mode), head(mode),
                  pl.BlockSpec((1, 7, CHUNK, CHUNK), lambda b, h: (h, 0, 0, 0)),
                  pl.BlockSpec((1, RET_DK), lambda b, h: (0, h))],
        out_specs=head(),
        out_shape=jax.ShapeDtypeStruct((B, S, RET_W), BF16),
        scratch_shapes=([pltpu.VMEM((n, RET_DK, 2 * RET_DK), BF16), pltpu.VMEM((n, CHUNK, CHUNK), BF16)]
                        + [pltpu.VMEM((CHUNK, RET_DK), F32)] * group),
        compiler_params=_cparams(("parallel", "parallel")),
        name="retention",
    )(rq, rkt, rv, rgs, dec, gnw)


def _attn_kernel(bounded_ref, q_ref, kt_ref, v_ref, o_ref, *scratch, tq, tk, single_kv):
    q = q_ref[0, 0, 0]
    kt = kt_ref[0, 0]
    v1 = jnp.concatenate([v_ref[0], jnp.ones((tk, ATT_HD), BF16)], axis=1)
    bounded = bounded_ref[0] == 1

    def write(pv):
        o = pv[:, :ATT_HD] / pv[:, ATT_HD:]
        for g in range(ATT_GROUP):
            o_ref[0, :, g * ATT_HD:(g + 1) * ATT_HD] = o[g * tq:(g + 1) * tq].astype(BF16)

    if single_kv:
        @pl.when(bounded)
        def _():
            write(_dot(jnp.exp2(_dot(q, kt)).astype(BF16), v1))

        @pl.when(jnp.logical_not(bounded))
        def _():
            s = _dot(q, kt)
            write(_dot(jnp.exp2(s - jnp.max(s, axis=1, keepdims=True)).astype(BF16), v1))

        return

    m_ref, acc_ref = scratch
    ki = pl.program_id(3)

    @pl.when(ki == 0)
    def _():
        m_ref[...] = jnp.full(m_ref.shape, -jnp.inf, F32)
        acc_ref[...] = jnp.zeros(acc_ref.shape, F32)

    @pl.when(bounded)
    def _():
        acc_ref[...] += _dot(jnp.exp2(_dot(q, kt)).astype(BF16), v1)

    @pl.when(jnp.logical_not(bounded))
    def _():
        s = _dot(q, kt)
        m_prev = m_ref[...]
        m_next = jnp.maximum(m_prev, jnp.max(s, axis=1, keepdims=True))
        p = jnp.exp2(s - jnp.tile(m_next, (1, tk // ATT_HD)))
        alpha = jnp.exp2(m_prev - m_next)
        acc_ref[...] = jnp.tile(alpha, (1, 2)) * acc_ref[...] + _dot(p.astype(BF16), v1)
        m_ref[...] = m_next

    @pl.when(ki == pl.num_programs(3) - 1)
    def _():
        write(acc_ref[...])


def _attention(bounded, aq, akt, av):
    B, S, _ = av.shape
    tq = aq.shape[3] // ATT_GROUP
    single_kv = S <= ATT_SINGLE_KV_MAX
    tk = S if single_kv else _tile(S, ATT_TK)
    rows = ATT_GROUP * tq
    scratch = [] if single_kv else [pltpu.VMEM((rows, ATT_HD), F32), pltpu.VMEM((rows, 2 * ATT_HD), F32)]
    return pl.pallas_call(
        functools.partial(_attn_kernel, tq=tq, tk=tk, single_kv=single_kv),
        grid=(B, ATT_KV_HEADS, S // tq, S // tk),
        in_specs=[pl.BlockSpec(memory_space=pltpu.SMEM),
                  pl.BlockSpec((1, 1, 1, ATT_GROUP * tq, ATT_HD), lambda b, k, i, j: (b, k, i, 0, 0)),
                  pl.BlockSpec((1, 1, ATT_HD, tk), lambda b, k, i, j: (b, k, 0, j)),
                  pl.BlockSpec((1, tk, ATT_HD), lambda b, k, i, j: (b, j, k))],
        out_specs=pl.BlockSpec((1, tq, ATT_GROUP * ATT_HD), lambda b, k, i, j: (b, i, k)),
        out_shape=jax.ShapeDtypeStruct((B, S, ATT_W), BF16),
        scratch_shapes=scratch,
        compiler_params=_cparams(("parallel", "parallel", "parallel", "arbitrary")),
        name="gqa_attention",
    )(bounded, aq, akt, av)


def _merge_kernel(ro_ref, sg_ref, at_ref, gate_ref, x_ref, wr_ref, ws_ref, wa_ref, wo_ref, lnw_ref, lnb_ref,
                  xo_ref, xb_ref):
    def branch(a_ref, w_ref, i):
        return gate_ref[:, i * D_MODEL:(i + 1) * D_MODEL].astype(F32) * _dot(a_ref[...], w_ref[...])

    merged = branch(ro_ref, wr_ref, 0) + branch(sg_ref, ws_ref, 1) + branch(at_ref, wa_ref, 2)
    y = _dot(merged.astype(BF16), wo_ref[...])
    out = _layer_norm_rows(ALPHA * x_ref[...] + y, lnw_ref[...], lnb_ref[...])
    xo_ref[...] = out
    xb_ref[...] = out.astype(BF16)


def _merge(ro, sg, at, gates, x, wr, ws, wa, wo, lnw, lnb, tm):
    T = x.shape[0]
    row = pl.BlockSpec((tm, D_MODEL), lambda i: (i, 0))
    wspec = _const_spec((D_MODEL, D_MODEL))
    vec = _const_spec((1, D_MODEL))
    return pl.pallas_call(
        _merge_kernel,
        grid=(T // tm,),
        in_specs=[row, row, row, pl.BlockSpec((tm, N_BRANCH * D_MODEL), lambda i: (i, 0)), row,
                  wspec, wspec, wspec, wspec, vec, vec],
        out_specs=[row, row],
        out_shape=[jax.ShapeDtypeStruct((T, D_MODEL), F32), jax.ShapeDtypeStruct((T, D_MODEL), BF16)],
        compiler_params=_cparams(("parallel",)),
        name="merge_out_ln",
    )(ro, sg, at, gates, x, wr, ws, wa, wo, lnw, lnb)


def _cross_kernel(x_ref, xb_ref, kv_ref, wq_ref, wo_ref, lnw_ref, lnb_ref, xo_ref, xbo_ref, o_scr):
    q = _dot(xb_ref[0], wq_ref[...]).astype(BF16)
    scale = X_HD ** -0.5
    for h in range(X_HEADS):
        cols = slice(h * X_HD, (h + 1) * X_HD)
        k = kv_ref[0, :, cols]
        v = kv_ref[0, :, D_MODEL + h * X_HD:D_MODEL + (h + 1) * X_HD]
        s = lax.dot_general(q[:, cols], k, (((1,), (1,)), ((), ())), preferred_element_type=F32) * scale
        p = jnp.exp(s - jnp.max(s, axis=-1, keepdims=True))
        o = _dot(p.astype(BF16), v) / jnp.sum(p, axis=-1, keepdims=True)
        o_scr[:, cols] = o.astype(BF16)
    y = _dot(o_scr[...], wo_ref[...])
    out = _layer_norm_rows(ALPHA * x_ref[0] + y, lnw_ref[...], lnb_ref[...])
    xo_ref[0] = out
    xbo_ref[0] = out.astype(BF16)


def _cross(x3, xb3, kv3, wq, wo, lnw, lnb, tm):
    B, S, _ = x3.shape
    row = pl.BlockSpec((1, tm, D_MODEL), lambda b, i: (b, i, 0))
    wspec = _const_spec((D_MODEL, D_MODEL))
    vec = _const_spec((1, D_MODEL))
    return pl.pallas_call(
        _cross_kernel,
        grid=(B, S // tm),
        in_specs=[row, row, pl.BlockSpec((1, N_MEM, 2 * D_MODEL), lambda b, i: (b, 0, 0)), wspec, wspec, vec, vec],
        out_specs=[row, row],
        out_shape=[jax.ShapeDtypeStruct((B, S, D_MODEL), F32), jax.ShapeDtypeStruct((B, S, D_MODEL), BF16)],
        scratch_shapes=[pltpu.VMEM((tm, D_MODEL), BF16)],
        compiler_params=_cparams(("parallel", "parallel")),
        name="cross_attn_ln",
    )(x3, xb3, kv3, wq, wo, lnw, lnb)


def _ffn_kernel(x_ref, xb_ref, wa_ref, wb_ref, wo_ref, lnw_ref, lnb_ref, xo_ref, xbo_ref, h_scr):
    xb = xb_ref[...]
    for j in range(D_FF // FF_CHUNK):
        cols = slice(j * FF_CHUNK, (j + 1) * FF_CHUNK)
        a = _dot(xb, wa_ref[:, cols])
        b = _dot(xb, wb_ref[:, cols])
        h_scr[:, cols] = (a * _sigmoid(a) * b).astype(BF16)
    y = _dot(h_scr[...], wo_ref[...])
    out = _layer_norm_rows(ALPHA * x_ref[...] + y, lnw_ref[...], lnb_ref[...])
    xo_ref[...] = out
    xbo_ref[...] = out.astype(BF16)


def _ffn(x, xb, wa, wb, wo, lnw, lnb, tm):
    T = x.shape[0]
    row = pl.BlockSpec((tm, D_MODEL), lambda i: (i, 0))
    vec = _const_spec((1, D_MODEL))
    return pl.pallas_call(
        _ffn_kernel,
        grid=(T // tm,),
        in_specs=[row, row, _const_spec((D_MODEL, D_FF)), _const_spec((D_MODEL, D_FF)),
                  _const_spec((D_FF, D_MODEL)), vec, vec],
        out_specs=[row, row],
        out_shape=[jax.ShapeDtypeStruct((T, D_MODEL), F32), jax.ShapeDtypeStruct((T, D_MODEL), BF16)],
        scratch_shapes=[pltpu.VMEM((tm, D_FF), BF16)],
        compiler_params=_cparams(("parallel",)),
        name="swiglu_ln",
    )(x, xb, wa, wb, wo, lnw, lnb)


def _rope_tables(S):
    t = jnp.arange(S)

    def cos_sin(pos, dim):
        inv_freq = ROPE_BASE ** (-jnp.arange(0, dim, 2, dtype=F32) / dim)
        ang = pos.astype(F32)[:, None] * inv_freq[None, :]
        return jnp.cos(ang), jnp.sin(ang)

    ct, st = cos_sin(t, RET_DK)
    cr, sr = cos_sin(t // GRID_W, ATT_HD // 2)
    cc, sc = cos_sin(t % GRID_W, ATT_HD // 2)
    return (jnp.concatenate([ct, ct], -1), jnp.concatenate([-st, st], -1),
            jnp.concatenate([cr, cc, cr, cc], -1), jnp.concatenate([-sr, -sc, sr, sc], -1))


def _axial_dim_order():
    quarter = ATT_HD // 4
    blocks = [jnp.arange(i * quarter, (i + 1) * quarter) for i in (0, 2, 1, 3)]
    return jnp.concatenate(blocks)


def _decay_tables(decay_f, decay_b):
    lgf = jax.nn.log_sigmoid(decay_f.astype(F32))[:, None, None]
    lgb = jax.nn.log_sigmoid(decay_b.astype(F32))[:, None, None]
    idx = jnp.arange(CHUNK, dtype=F32)
    diff = idx[:, None] - idx[None, :]
    dtot = jnp.where(diff >= 0, jnp.exp(lgf * jnp.maximum(diff, 0.0)), jnp.exp(lgb * jnp.maximum(-diff, 0.0)))
    ones = jnp.ones((CHUNK, CHUNK), F32)
    row = idx[None, :, None] * ones
    lane = idx[None, None, :] * ones
    xif = jnp.exp(lgf * (row + 1.0))
    xib = jnp.exp(lgb * (CHUNK - row))
    zf = jnp.exp(lgf * (CHUNK - 1.0 - lane))
    zb = jnp.exp(lgb * lane)
    cdf = jnp.exp(lgf * CHUNK) * ones
    cdb = jnp.exp(lgb * CHUNK) * ones
    return jnp.stack([dtot, xif, xib, zf, zb, cdf, cdb], axis=1)


def _encoder(x, mem, p):
    B, S, _ = x.shape
    T = B * S
    tm = _tile(T, TOKEN_TILE)
    ts = _tile(S, TOKEN_TILE)
    rc, rs, ac, as_ = _rope_tables(S)
    xf, xb = _in_ln(x.reshape(T, D_MODEL), p["in_ln_w"], p["in_ln_b"], tm)
    memb = mem.astype(BF16).reshape(B * N_MEM, D_MODEL)
    flat = lambda a: a.reshape(T, D_MODEL)
    seq = lambda a: a.reshape(B, S, D_MODEL)
    for l in range(DEPTH):
        xb3 = seq(xb)
        rq, rkt = _proj_ret(xb3, p["w_ret"][l], rc, rs, ts)
        rv, rgs, sg = _proj_mid(xb3, p["w_mid"][l], p["sg_ln_w"][l], p["sg_ln_b"][l], p["sg_ws"][l], p["sg_bias"][l], ts)
        aq, akt, av, gates = _proj_att_gate(xb3, p["w_att"][l], p["w_gate"][l], p["b_gate"][l], ac, as_,
                                            p["att_qn_w"][l], p["att_kn_w"][l], ts, _tile(S, ATT_TQ))
        gates = gates.reshape(T, N_BRANCH * D_MODEL)
        ro = _retention(rq, rkt, rv, rgs, p["dec"][l], p["ret_gn_w"][l])
        at = _attention(p["att_bounded"][l], aq, akt, av)
        xf, xb = _merge(flat(ro), flat(sg), flat(at), gates, xf, p["ret_wo"][l], p["sg_wo"][l], p["att_wo"][l],
                        p["w_out"][l], p["ln_w"][l, 0], p["ln_b"][l, 0], _tile(T, MERGE_TILE))
        kv = _matmul(memb, p["xa_wkv"][l], _tile(B * N_MEM, 1024), 1024, "proj_kv")
        xf3, xb3 = _cross(seq(xf), seq(xb), kv.reshape(B, N_MEM, 2 * D_MODEL), p["xa_wq"][l], p["xa_wo"][l],
                          p["ln_w"][l, 1], p["ln_b"][l, 1], ts)
        xf, xb = _ffn(flat(xf3), flat(xb3), p["ffn_wa"][l], p["ffn_wb"][l], p["ffn_w_out"][l],
                      p["ln_w"][l, 2], p["ln_b"][l, 2], tm)
    return seq(xf)


def _prepare_params(in_ln_w, in_ln_b, w_in, b_gate, ret_decay_f, ret_decay_b, ret_gn_w, ret_wo, sg_ln_w, sg_ln_b,
                    sg_ws, sg_b, sg_wo, att_qn_w, att_kn_w, att_wo, w_out, ln_w, ln_b, xa_wq, xa_wkv, xa_wo,
                    ffn_w_in, ffn_w_out):
    vec = lambda a: a.astype(F32).reshape(a.shape[:-1] + (1, a.shape[-1]))
    group = lambda c: w_in[:, :, c[0]:c[1]].astype(BF16)
    sg_bias = jnp.repeat(jnp.swapaxes(sg_b.astype(F32), 1, 2), SG_GW, axis=-1)
    score_bound = (SCORE_BOUND_COEF * jnp.max(jnp.abs(att_qn_w.astype(F32)), axis=-1)
                   * jnp.max(jnp.abs(att_kn_w.astype(F32)), axis=-1))
    order = _axial_dim_order()
    w_att = group(COLS_ATT)
    n_qk = ATT_W + ATT_KV_W
    w_qk = w_att[:, :, :n_qk].reshape(DEPTH, D_MODEL, n_qk // ATT_HD, ATT_HD)[..., order].reshape(DEPTH, D_MODEL, n_qk)
    w_att = jnp.concatenate([w_qk, w_att[:, :, n_qk:]], axis=-1)
    with_partner = lambda w: jnp.stack([w[:, order], jnp.roll(w[:, order], ATT_HD // 2, axis=-1)], axis=1).astype(F32)
    att_qn_w = with_partner(att_qn_w)
    att_kn_w = with_partner(att_kn_w)
    return dict(
        att_bounded=(score_bound <= SAFE_SCORE_BOUND).astype(jnp.int32).reshape(DEPTH, 1),
        in_ln_w=vec(in_ln_w), in_ln_b=vec(in_ln_b),
        w_ret=group(COLS_RET), w_mid=group(COLS_MID), w_att=w_att, w_gate=group(COLS_GATE),
        b_gate=vec(b_gate),
        dec=jnp.stack([_decay_tables(ret_decay_f[l], ret_decay_b[l]) for l in range(DEPTH)]),
        ret_gn_w=vec(ret_gn_w), ret_wo=ret_wo.astype(BF16),
        sg_ln_w=vec(sg_ln_w), sg_ln_b=vec(sg_ln_b), sg_ws=sg_ws.astype(BF16), sg_bias=sg_bias,
        sg_wo=sg_wo.astype(BF16),
        att_qn_w=att_qn_w, att_kn_w=att_kn_w, att_wo=att_wo.astype(BF16),
        w_out=w_out.astype(BF16), ln_w=vec(ln_w), ln_b=vec(ln_b),
        xa_wq=xa_wq.astype(BF16), xa_wkv=xa_wkv.astype(BF16), xa_wo=xa_wo.astype(BF16),
        ffn_wa=ffn_w_in[:, :, :D_FF].astype(BF16), ffn_wb=ffn_w_in[:, :, D_FF:].astype(BF16),
        ffn_w_out=ffn_w_out.astype(BF16),
    )


def kernel(x_prompt, x_sample, mem_prompt, mem_sample, in_ln_w, in_ln_b, w_in, b_gate, ret_decay_f, ret_decay_b,
           ret_gn_w, ret_wo, sg_ln_w, sg_ln_b, sg_ws, sg_b, sg_wo, att_qn_w, att_kn_w, att_wo, w_out, ln_w, ln_b,
           xa_wq, xa_wkv, xa_wo, ffn_w_in, ffn_w_out):
    p = _prepare_params(in_ln_w, in_ln_b, w_in, b_gate, ret_decay_f, ret_decay_b, ret_gn_w, ret_wo, sg_ln_w,
                        sg_ln_b, sg_ws, sg_b, sg_wo, att_qn_w, att_kn_w, att_wo, w_out, ln_w, ln_b, xa_wq, xa_wkv,
                        xa_wo, ffn_w_in, ffn_w_out)
    return (_encoder(x_prompt, mem_prompt, p), _encoder(x_sample, mem_sample, p))
```

```python
import functools

import jax
import jax.numpy as jnp
from jax import lax
from jax.experimental import pallas as pl
from jax.experimental.pallas import tpu as pltpu

F32 = jnp.float32
BF16 = jnp.bfloat16

D_MODEL = 1024
DEPTH = 4
N_MEM = 256
GRID_W = 64
CHUNK = 128
RET_HEADS = 8
RET_DK = 128
RET_W = RET_HEADS * RET_DK
SG_GROUPS = 4
SG_GW = 256
SG_W = SG_GROUPS * SG_GW
ATT_HEADS = 8
ATT_KV_HEADS = 2
ATT_GROUP = ATT_HEADS // ATT_KV_HEADS
ATT_HD = 128
ATT_W = ATT_HEADS * ATT_HD
ATT_KV_W = ATT_KV_HEADS * ATT_HD
X_HEADS = 4
X_HD = D_MODEL // X_HEADS
D_FF = 2816
N_BRANCH = 3
ALPHA = (2 * DEPTH) ** 0.25
ROPE_BASE = 10000.0
LN_EPS = 1e-5
RMS_EPS = 1e-6
LOG2E = 1.4426950408889634

COLS_RET = (0, 2 * RET_W)
COLS_MID = (COLS_RET[1], COLS_RET[1] + 2 * RET_W + 2 * SG_W)
COLS_ATT = (COLS_MID[1], COLS_MID[1] + ATT_W + 2 * ATT_KV_W)
COLS_GATE = (COLS_ATT[1], COLS_ATT[1] + N_BRANCH * D_MODEL)

MXU_W = 256
FF_CHUNK = MXU_W
SLAB = 512
Q_PIECE = 32
TOKEN_TILE = 1024
MERGE_TILE = 512
ATT_TQ = 512
ATT_TK = 2048
ATT_SINGLE_KV_MAX = 2048
RET_UNROLL = 8
RET_DOUBLE_BUFFER_MAX_BYTES = 1024 * 1024
SCORE_BOUND_COEF = ATT_HD ** 0.5 * LOG2E
SAFE_SCORE_BOUND = 64.0
VMEM_LIMIT = 52 * 1024 * 1024


def _cparams(sem):
    return pltpu.CompilerParams(dimension_semantics=sem, vmem_limit_bytes=VMEM_LIMIT)


def _const_spec(shape):
    nd = len(shape)
    return pl.BlockSpec(shape, lambda *_: (0,) * nd, pipeline_mode=pl.Buffered(1))


def _tile(n, pref):
    t = min(n, pref)
    assert n % t == 0, (n, t)
    return t


def _layer_norm_rows(z, w, b):
    mu = jnp.mean(z, axis=-1, keepdims=True)
    d = z - mu
    var = jnp.mean(d * d, axis=-1, keepdims=True)
    return d * lax.rsqrt(var + LN_EPS) * w + b


def _gelu_tanh(x):
    return 0.5 * x * (1.0 + jnp.tanh(0.7978845608028654 * (x + 0.044715 * (x * x * x))))


def _sigmoid(x):
    return 1.0 / (1.0 + jnp.exp(-x))


def _dot(a, b):
    return jnp.dot(a, b, preferred_element_type=F32)


def _in_ln_kernel(x_ref, w_ref, b_ref, xo_ref, xb_ref):
    y = _layer_norm_rows(x_ref[...], w_ref[...], b_ref[...])
    xo_ref[...] = y
    xb_ref[...] = y.astype(BF16)


def _in_ln(x2, w, b, tm):
    T = x2.shape[0]
    row = pl.BlockSpec((tm, D_MODEL), lambda i: (i, 0))
    return pl.pallas_call(
        _in_ln_kernel,
        grid=(T // tm,),
        in_specs=[row, _const_spec((1, D_MODEL)), _const_spec((1, D_MODEL))],
        out_specs=[row, row],
        out_shape=[jax.ShapeDtypeStruct((T, D_MODEL), F32), jax.ShapeDtypeStruct((T, D_MODEL), BF16)],
        compiler_params=_cparams(("parallel",)),
        name="in_ln",
    )(x2, w, b)


def _mm_kernel(x_ref, w_ref, o_ref):
    o_ref[...] = _dot(x_ref[...], w_ref[...]).astype(o_ref.dtype)


def _matmul(x, w, tm, tn, name):
    M, K = x.shape
    N = w.shape[1]
    return pl.pallas_call(
        _mm_kernel,
        grid=(N // tn, M // tm),
        in_specs=[pl.BlockSpec((tm, K), lambda j, i: (i, 0)), pl.BlockSpec((K, tn), lambda j, i: (0, j))],
        out_specs=pl.BlockSpec((tm, tn), lambda j, i: (i, j)),
        out_shape=jax.ShapeDtypeStruct((M, N), BF16),
        compiler_params=_cparams(("parallel", "parallel")),
        name=name,
    )(x, w)


def _rope(x, cos, sin_signed, partner):
    return x * cos + partner * sin_signed


def _proj_ret_kernel(x_ref, w_ref, rc_ref, rs_ref, rq_ref, rkt_ref, *, tm):
    x = x_ref[0]
    rc = rc_ref[...]
    rs = rs_ref[...]
    k_scale = RET_DK ** -0.5

    def rope(t):
        return _rope(t, rc, rs, pltpu.roll(t, RET_DK // 2, axis=1))

    heads_per_slab = MXU_W // RET_DK
    for slab in range(RET_W // MXU_W):
        q2 = _dot(x, w_ref[:, slab * MXU_W:(slab + 1) * MXU_W])
        k2 = _dot(x, w_ref[:, RET_W + slab * MXU_W:RET_W + (slab + 1) * MXU_W])
        for j in range(heads_per_slab):
            h = slab * heads_per_slab + j
            sl = slice(j * RET_DK, (j + 1) * RET_DK)
            rq_ref[0, :, h * RET_DK:(h + 1) * RET_DK] = rope(q2[:, sl]).astype(BF16)
            kt = (rope(k2[:, sl]) * k_scale).T
            for c in range(tm // CHUNK):
                rkt_ref[0, h, c] = kt[:, c * CHUNK:(c + 1) * CHUNK].astype(BF16)


def _proj_ret(xb3, w, rc, rs, tm):
    B, S, _ = xb3.shape
    tab = pl.BlockSpec((tm, RET_DK), lambda b, s: (s, 0))
    return pl.pallas_call(
        functools.partial(_proj_ret_kernel, tm=tm),
        grid=(B, S // tm),
        in_specs=[pl.BlockSpec((1, tm, D_MODEL), lambda b, s: (b, s, 0)), _const_spec((D_MODEL, 2 * RET_W)), tab, tab],
        out_specs=[pl.BlockSpec((1, tm, RET_W), lambda b, s: (b, s, 0)),
                   pl.BlockSpec((1, RET_HEADS, tm // CHUNK, RET_DK, CHUNK), lambda b, s: (b, 0, s, 0, 0))],
        out_shape=[jax.ShapeDtypeStruct((B, S, RET_W), BF16),
                   jax.ShapeDtypeStruct((B, RET_HEADS, S // CHUNK, RET_DK, CHUNK), BF16)],
        compiler_params=_cparams(("parallel", "parallel")),
        name="proj_ret",
    )(xb3, w, rc, rs)


def _proj_mid_kernel(x_ref, w_ref, lnw_ref, lnb_ref, ws_ref, bias_ref, rv_ref, rgs_ref, sg_ref, vn_ref, *, tm):
    x = x_ref[0]
    for i in range(RET_W // SLAB):
        cols = slice(i * SLAB, (i + 1) * SLAB)
        rv_ref[0, :, cols] = _dot(x, w_ref[:, cols]).astype(BF16)
        g = _dot(x, w_ref[:, RET_W + i * SLAB:RET_W + (i + 1) * SLAB])
        rgs_ref[0, :, cols] = (g * _sigmoid(g)).astype(BF16)
    sv = _gelu_tanh(_dot(x, w_ref[:, 2 * RET_W + SG_W:2 * RET_W + 2 * SG_W]))
    vn_ref[...] = _layer_norm_rows(sv, lnw_ref[...], lnb_ref[...]).astype(BF16)
    for grp in range(SG_GROUPS):
        cols = slice(grp * SG_GW, (grp + 1) * SG_GW)
        u = _gelu_tanh(_dot(x, w_ref[:, 2 * RET_W + grp * SG_GW:2 * RET_W + (grp + 1) * SG_GW]))
        for c in range(tm // CHUNK):
            rows = slice(c * CHUNK, (c + 1) * CHUNK)
            mixed = _dot(ws_ref[grp], vn_ref[rows, cols]) + bias_ref[:, cols]
            sg_ref[0, rows, cols] = (u[rows, :] * mixed).astype(BF16)


def _proj_mid(xb3, w, lnw, lnb, ws, bias, tm):
    B, S, _ = xb3.shape
    row = pl.BlockSpec((1, tm, D_MODEL), lambda b, s: (b, s, 0))
    out = jax.ShapeDtypeStruct((B, S, D_MODEL), BF16)
    return pl.pallas_call(
        functools.partial(_proj_mid_kernel, tm=tm),
        grid=(B, S // tm),
        in_specs=[row, _const_spec((D_MODEL, 2 * RET_W + 2 * SG_W)), _const_spec((1, SG_W)), _const_spec((1, SG_W)),
                  _const_spec((SG_GROUPS, CHUNK, CHUNK)), _const_spec((CHUNK, SG_W))],
        out_specs=[row, row, row],
        out_shape=[out, out, out],
        scratch_shapes=[pltpu.VMEM((tm, SG_W), BF16)],
        compiler_params=_cparams(("parallel", "parallel")),
        name="proj_mid",
    )(xb3, w, lnw, lnb, ws, bias)


def _proj_att_gate_kernel(x_ref, w_ref, wg_ref, bg_ref, ac_ref, as_ref, qnw_ref, knw_ref,
                          aq_ref, akt_ref, av_ref, gate_ref, *, tm, tq):
    x = x_ref[0]

    n_gate_slabs = N_BRANCH * D_MODEL // MXU_W
    gate_slabs_done = []

    def gate_slab():
        i = len(gate_slabs_done)
        gate_slabs_done.append(i)
        cols = slice(i * MXU_W, (i + 1) * MXU_W)
        gate_ref[0, :, cols] = _sigmoid(_dot(x, wg_ref[:, cols]) + bg_ref[:, cols]).astype(BF16)

    n_q_slabs = ATT_W // MXU_W
    q_scale = ATT_HD ** -0.5 * LOG2E
    q_cos = ac_ref[...] * (qnw_ref[0:1, :] * q_scale)
    q_sin = as_ref[...] * (qnw_ref[1:2, :] * q_scale)
    k_cos = ac_ref[...] * knw_ref[0:1, :]
    k_sin = as_ref[...] * knw_ref[1:2, :]

    def norm_rope(t, cos, sin):
        r = lax.rsqrt(jnp.mean(t * t, axis=-1, keepdims=True) + RMS_EPS)
        return r * (t * cos + pltpu.roll(t, ATT_HD // 2, axis=1) * sin)

    heads_per_slab = MXU_W // ATT_HD
    for slab in range(n_q_slabs):
        q2 = _dot(x, w_ref[:, slab * MXU_W:(slab + 1) * MXU_W])
        for j in range(heads_per_slab):
            kv_head, g = divmod(slab * heads_per_slab + j, ATT_GROUP)
            for c in range(tm // Q_PIECE):
                rows = slice(c * Q_PIECE, (c + 1) * Q_PIECE)
                t = q2[rows, j * ATT_HD:(j + 1) * ATT_HD]
                q_tile, r0 = divmod(c * Q_PIECE, tq)
                aq_ref[0, kv_head, q_tile, g * tq + r0:g * tq + r0 + Q_PIECE, :] = (
                    norm_rope(t, q_cos[rows], q_sin[rows]).astype(BF16))
            gate_slab()
    k2 = _dot(x, w_ref[:, ATT_W:ATT_W + ATT_KV_W])
    for j in range(ATT_KV_HEADS):
        for c in range(tm // CHUNK):
            rows = slice(c * CHUNK, (c + 1) * CHUNK)
            t = k2[rows, j * ATT_HD:(j + 1) * ATT_HD]
            akt_ref[0, j, :, rows] = norm_rope(t, k_cos[rows], k_sin[rows]).T.astype(BF16)
        gate_slab()
    av_ref[0] = _dot(x, w_ref[:, ATT_W + ATT_KV_W:ATT_W + 2 * ATT_KV_W]).astype(BF16)
    while len(gate_slabs_done) < n_gate_slabs:
        gate_slab()


def _proj_att_gate(xb3, w, wg, bg, ac, as_, qnw, knw, tm, tq):
    B, S, _ = xb3.shape
    assert tm % tq == 0, (tm, tq)
    n_gate = N_BRANCH * D_MODEL
    tab = pl.BlockSpec((tm, ATT_HD), lambda b, s: (s, 0))
    return pl.pallas_call(
        functools.partial(_proj_att_gate_kernel, tm=tm, tq=tq),
        grid=(B, S // tm),
        in_specs=[pl.BlockSpec((1, tm, D_MODEL), lambda b, s: (b, s, 0)), _const_spec((D_MODEL, ATT_W + 2 * ATT_KV_W)),
                  _const_spec((D_MODEL, n_gate)), _const_spec((1, n_gate)),
                  tab, tab, _const_spec((2, ATT_HD)), _const_spec((2, ATT_HD))],
        out_specs=[pl.BlockSpec((1, ATT_KV_HEADS, tm // tq, ATT_GROUP * tq, ATT_HD), lambda b, s: (b, 0, s, 0, 0)),
                   pl.BlockSpec((1, ATT_KV_HEADS, ATT_HD, tm), lambda b, s: (b, 0, 0, s)),
                   pl.BlockSpec((1, tm, ATT_KV_W), lambda b, s: (b, s, 0)),
                   pl.BlockSpec((1, tm, n_gate), lambda b, s: (b, s, 0))],
        out_shape=[jax.ShapeDtypeStruct((B, ATT_KV_HEADS, S // tq, ATT_GROUP * tq, ATT_HD), BF16),
                   jax.ShapeDtypeStruct((B, ATT_KV_HEADS, ATT_HD, S), BF16),
                   jax.ShapeDtypeStruct((B, S, ATT_KV_W), BF16),
                   jax.ShapeDtypeStruct((B, S, n_gate), BF16)],
        compiler_params=_cparams(("parallel", "parallel")),
        name="proj_att_gate",
    )(xb3, w, wg, bg, ac, as_, qnw, knw)


def _ret_kernel(q_ref, kt_ref, v_ref, g_ref, dec_ref, gnw_ref, o_ref, st_ref, p_ref, *tot_refs, n_chunks, unroll):
    dtot = dec_ref[0, 0]
    xif, xib = dec_ref[0, 1], dec_ref[0, 2]
    zf, zb = dec_ref[0, 3], dec_ref[0, 4]
    cdf, cdb = dec_ref[0, 5], dec_ref[0, 6]
    gnw = gnw_ref[...]

    def rows_of(c):
        if isinstance(c, int):
            return pl.ds(c * CHUNK, CHUNK)
        return pl.ds(pl.multiple_of(c * CHUNK, CHUNK), CHUNK)

    def scan(i, carry):
        sf, sb = carry
        cf, cb = i, n_chunks - 1 - i
        st_ref[cf, :, :RET_DK] = sf.astype(BF16)
        st_ref[cb, :, RET_DK:] = sb.astype(BF16)
        kzf = (kt_ref[0, 0, cf].astype(F32) * zf).astype(BF16)
        kzb = (kt_ref[0, 0, cb].astype(F32) * zb).astype(BF16)
        sf = sf * cdf + _dot(kzf, v_ref[0, rows_of(cf), :])
        sb = sb * cdb + _dot(kzb, v_ref[0, rows_of(cb), :])
        s = _dot(q_ref[0, rows_of(i), :], kt_ref[0, 0, i])
        p_ref[i] = (s * dtot).astype(BF16)
        return sf, sb

    zero = jnp.zeros((RET_DK, RET_DK), F32)
    lax.fori_loop(0, n_chunks, scan, (zero, zero), unroll=unroll)

    group = len(tot_refs)

    def mix_group(j):
        for k, tot_ref in enumerate(tot_refs):
            c = j * group + k
            rows = rows_of(c)
            inter = _dot(q_ref[0, rows, :], st_ref[c])
            tot_ref[...] = _dot(p_ref[c], v_ref[0, rows, :]) + inter[:, :RET_DK] * xif + inter[:, RET_DK:] * xib

    def norm_group(j):
        for k, tot_ref in enumerate(tot_refs):
            base = (j * group + k) * CHUNK
            for r in range(0, CHUNK, Q_PIECE):
                rows = pl.ds(base + r, Q_PIECE) if isinstance(base, int) else pl.ds(
                    pl.multiple_of(base + r, Q_PIECE), Q_PIECE)
                tot = tot_ref[r:r + Q_PIECE, :]
                mu = jnp.mean(tot, axis=-1, keepdims=True)
                var = jnp.maximum(jnp.mean(tot * tot, axis=-1, keepdims=True) - mu * mu, 0.0)
                y = (tot - mu) * lax.rsqrt(var + LN_EPS) * gnw
                o_ref[0, rows, :] = (y * g_ref[0, rows, :].astype(F32)).astype(BF16)

    mix_group(0)

    def step(j, carry):
        norm_group(j - 1)
        mix_group(j)
        return carry

    n_groups = n_chunks // group
    lax.fori_loop(1, n_groups, step, 0)
    norm_group(n_groups - 1)


def _retention(rq, rkt, rv, rgs, dec, gnw):
    B, S, _ = rq.shape
    n = S // CHUNK
    group = min(RET_UNROLL, n)
    assert n % group == 0, (n, group)
    mode = pl.Buffered(1) if S * RET_DK * 2 > RET_DOUBLE_BUFFER_MAX_BYTES else None
    head = lambda pm=None: pl.BlockSpec((1, S, RET_DK), lambda b, h: (b, 0, h), pipeline_mode=pm)
    return pl.pallas_call(
        functools.partial(_ret_kernel, n_chunks=n, unroll=min(RET_UNROLL, n)),
        grid=(B, RET_HEADS),
        in_specs=[head(mode),
                  pl.BlockSpec((1, 1, n, RET_DK, CHUNK), lambda b, h: (b, h, 0, 0, 0), pipeline_mode=mode),
                  head(mode), head(mode),
                  pl.BlockSpec((1, 7, CHUNK, CHUNK), lambda b, h: (h, 0, 0, 0)),
                  pl.BlockSpec((1, RET_DK), lambda b, h: (0, h))],
        out_specs=head(),
        out_shape=jax.ShapeDtypeStruct((B, S, RET_W), BF16),
        scratch_shapes=([pltpu.VMEM((n, RET_DK, 2 * RET_DK), BF16), pltpu.VMEM((n, CHUNK, CHUNK), BF16)]
                        + [pltpu.VMEM((CHUNK, RET_DK), F32)] * group),
        compiler_params=_cparams(("parallel", "parallel")),
        name="retention",
    )(rq, rkt, rv, rgs, dec, gnw)


def _attn_kernel(bounded_ref, q_ref, kt_ref, v_ref, o_ref, *scratch, tq, tk, single_kv):
    q = q_ref[0, 0, 0]
    kt = kt_ref[0, 0]
    v1 = jnp.concatenate([v_ref[0], jnp.ones((tk, ATT_HD), BF16)], axis=1)
    bounded = bounded_ref[0] == 1

    def write(pv):
        o = pv[:, :ATT_HD] / pv[:, ATT_HD:]
        for g in range(ATT_GROUP):
            o_ref[0, :, g * ATT_HD:(g + 1) * ATT_HD] = o[g * tq:(g + 1) * tq].astype(BF16)

    if single_kv:
        @pl.when(bounded)
        def _():
            write(_dot(jnp.exp2(_dot(q, kt)).astype(BF16), v1))

        @pl.when(jnp.logical_not(bounded))
        def _():
            s = _dot(q, kt)
            write(_dot(jnp.exp2(s - jnp.max(s, axis=1, keepdims=True)).astype(BF16), v1))

        return

    m_ref, acc_ref = scratch
    ki = pl.program_id(3)

    @pl.when(ki == 0)
    def _():
        m_ref[...] = jnp.full(m_ref.shape, -jnp.inf, F32)
        acc_ref[...] = jnp.zeros(acc_ref.shape, F32)

    @pl.when(bounded)
    def _():
        acc_ref[...] += _dot(jnp.exp2(_dot(q, kt)).astype(BF16), v1)

    @pl.when(jnp.logical_not(bounded))
    def _():
        s = _dot(q, kt)
        m_prev = m_ref[...]
        m_next = jnp.maximum(m_prev, jnp.max(s, axis=1, keepdims=True))
        p = jnp.exp2(s - jnp.tile(m_next, (1, tk // ATT_HD)))
        alpha = jnp.exp2(m_prev - m_next)
        acc_ref[...] = jnp.tile(alpha, (1, 2)) * acc_ref[...] + _dot(p.astype(BF16), v1)
        m_ref[...] = m_next

    @pl.when(ki == pl.num_programs(3) - 1)
    def _():
        write(acc_ref[...])


def _attention(bounded, aq, akt, av):
    B, S, _ = av.shape
    tq = aq.shape[3] // ATT_GROUP
    single_kv = S <= ATT_SINGLE_KV_MAX
    tk = S if single_kv else _tile(S, ATT_TK)
    rows = ATT_GROUP * tq
    scratch = [] if single_kv else [pltpu.VMEM((rows, ATT_HD), F32), pltpu.VMEM((rows, 2 * ATT_HD), F32)]
    return pl.pallas_call(
        functools.partial(_attn_kernel, tq=tq, tk=tk, single_kv=single_kv),
        grid=(B, ATT_KV_HEADS, S // tq, S // tk),
        in_specs=[pl.BlockSpec(memory_space=pltpu.SMEM),
                  pl.BlockSpec((1, 1, 1, ATT_GROUP * tq, ATT_HD), lambda b, k, i, j: (b, k, i, 0, 0)),
                  pl.BlockSpec((1, 1, ATT_HD, tk), lambda b, k, i, j: (b, k, 0, j)),
                  pl.BlockSpec((1, tk, ATT_HD), lambda b, k, i, j: (b, j, k))],
        out_specs=pl.BlockSpec((1, tq, ATT_GROUP * ATT_HD), lambda b, k, i, j: (b, i, k)),
        out_shape=jax.ShapeDtypeStruct((B, S, ATT_W), BF16),
        scratch_shapes=scratch,
        compiler_params=_cparams(("parallel", "parallel", "parallel", "arbitrary")),
        name="gqa_attention",
    )(bounded, aq, akt, av)


def _merge_kernel(ro_ref, sg_ref, at_ref, gate_ref, x_ref, wr_ref, ws_ref, wa_ref, wo_ref, lnw_ref, lnb_ref,
                  xo_ref, xb_ref):
    def branch(a_ref, w_ref, i):
        return gate_ref[:, i * D_MODEL:(i + 1) * D_MODEL].astype(F32) * _dot(a_ref[...], w_ref[...])

    merged = branch(ro_ref, wr_ref, 0) + branch(sg_ref, ws_ref, 1) + branch(at_ref, wa_ref, 2)
    y = _dot(merged.astype(BF16), wo_ref[...])
    out = _layer_norm_rows(ALPHA * x_ref[...] + y, lnw_ref[...], lnb_ref[...])
    xo_ref[...] = out
    xb_ref[...] = out.astype(BF16)


def _merge(ro, sg, at, gates, x, wr, ws, wa, wo, lnw, lnb, tm):
    T = x.shape[0]
    row = pl.BlockSpec((tm, D_MODEL), lambda i: (i, 0))
    wspec = _const_spec((D_MODEL, D_MODEL))
    vec = _const_spec((1, D_MODEL))
    return pl.pallas_call(
        _merge_kernel,
        grid=(T // tm,),
        in_specs=[row, row, row, pl.BlockSpec((tm, N_BRANCH * D_MODEL), lambda i: (i, 0)), row,
                  wspec, wspec, wspec, wspec, vec, vec],
        out_specs=[row, row],
        out_shape=[jax.ShapeDtypeStruct((T, D_MODEL), F32), jax.ShapeDtypeStruct((T, D_MODEL), BF16)],
        compiler_params=_cparams(("parallel",)),
        name="merge_out_ln",
    )(ro, sg, at, gates, x, wr, ws, wa, wo, lnw, lnb)


def _cross_kernel(x_ref, xb_ref, kv_ref, wq_ref, wo_ref, lnw_ref, lnb_ref, xo_ref, xbo_ref, o_scr):
    q = _dot(xb_ref[0], wq_ref[...]).astype(BF16)
    scale = X_HD ** -0.5
    for h in range(X_HEADS):
        cols = slice(h * X_HD, (h + 1) * X_HD)
        k = kv_ref[0, :, cols]
        v = kv_ref[0, :, D_MODEL + h * X_HD:D_MODEL + (h + 1) * X_HD]
        s = lax.dot_general(q[:, cols], k, (((1,), (1,)), ((), ())), preferred_element_type=F32) * scale
        p = jnp.exp(s - jnp.max(s, axis=-1, keepdims=True))
        o = _dot(p.astype(BF16), v) / jnp.sum(p, axis=-1, keepdims=True)
        o_scr[:, cols] = o.astype(BF16)
    y = _dot(o_scr[...], wo_ref[...])
    out = _layer_norm_rows(ALPHA * x_ref[0] + y, lnw_ref[...], lnb_ref[...])
    xo_ref[0] = out
    xbo_ref[0] = out.astype(BF16)


def _cross(x3, xb3, kv3, wq, wo, lnw, lnb, tm):
    B, S, _ = x3.shape
    row = pl.BlockSpec((1, tm, D_MODEL), lambda b, i: (b, i, 0))
    wspec = _const_spec((D_MODEL, D_MODEL))
    vec = _const_spec((1, D_MODEL))
    return pl.pallas_call(
        _cross_kernel,
        grid=(B, S // tm),
        in_specs=[row, row, pl.BlockSpec((1, N_MEM, 2 * D_MODEL), lambda b, i: (b, 0, 0)), wspec, wspec, vec, vec],
        out_specs=[row, row],
        out_shape=[jax.ShapeDtypeStruct((B, S, D_MODEL), F32), jax.ShapeDtypeStruct((B, S, D_MODEL), BF16)],
        scratch_shapes=[pltpu.VMEM((tm, D_MODEL), BF16)],
        compiler_params=_cparams(("parallel", "parallel")),
        name="cross_attn_ln",
    )(x3, xb3, kv3, wq, wo, lnw, lnb)


def _ffn_kernel(x_ref, xb_ref, wa_ref, wb_ref, wo_ref, lnw_ref, lnb_ref, xo_ref, xbo_ref, h_scr):
    xb = xb_ref[...]
    for j in range(D_FF // FF_CHUNK):
        cols = slice(j * FF_CHUNK, (j + 1) * FF_CHUNK)
        a = _dot(xb, wa_ref[:, cols])
        b = _dot(xb, wb_ref[:, cols])
        h_scr[:, cols] = (a * _sigmoid(a) * b).astype(BF16)
    y = _dot(h_scr[...], wo_ref[...])
    out = _layer_norm_rows(ALPHA * x_ref[...] + y, lnw_ref[...], lnb_ref[...])
    xo_ref[...] = out
    xbo_ref[...] = out.astype(BF16)


def _ffn(x, xb, wa, wb, wo, lnw, lnb, tm):
    T = x.shape[0]
    row = pl.BlockSpec((tm, D_MODEL), lambda i: (i, 0))
    vec = _const_spec((1, D_MODEL))
    return pl.pallas_call(
        _ffn_kernel,
        grid=(T // tm,),
        in_specs=[row, row, _const_spec((D_MODEL, D_FF)), _const_spec((D_MODEL, D_FF)),
                  _const_spec((D_FF, D_MODEL)), vec, vec],
        out_specs=[row, row],
        out_shape=[jax.ShapeDtypeStruct((T, D_MODEL), F32), jax.ShapeDtypeStruct((T, D_MODEL), BF16)],
        scratch_shapes=[pltpu.VMEM((tm, D_FF), BF16)],
        compiler_params=_cparams(("parallel",)),
        name="swiglu_ln",
    )(x, xb, wa, wb, wo, lnw, lnb)


def _rope_tables(S):
    t = jnp.arange(S)

    def cos_sin(pos, dim):
        inv_freq = ROPE_BASE ** (-jnp.arange(0, dim, 2, dtype=F32) / dim)
        ang = pos.astype(F32)[:, None] * inv_freq[None, :]
        return jnp.cos(ang), jnp.sin(ang)

    ct, st = cos_sin(t, RET_DK)
    cr, sr = cos_sin(t // GRID_W, ATT_HD // 2)
    cc, sc = cos_sin(t % GRID_W, ATT_HD // 2)
    return (jnp.concatenate([ct, ct], -1), jnp.concatenate([-st, st], -1),
            jnp.concatenate([cr, cc, cr, cc], -1), jnp.concatenate([-sr, -sc, sr, sc], -1))


def _axial_dim_order():
    quarter = ATT_HD // 4
    blocks = [jnp.arange(i * quarter, (i + 1) * quarter) for i in (0, 2, 1, 3)]
    return jnp.concatenate(blocks)


def _decay_tables(decay_f, decay_b):
    lgf = jax.nn.log_sigmoid(decay_f.astype(F32))[:, None, None]
    lgb = jax.nn.log_sigmoid(decay_b.astype(F32))[:, None, None]
    idx = jnp.arange(CHUNK, dtype=F32)
    diff = idx[:, None] - idx[None, :]
    dtot = jnp.where(diff >= 0, jnp.exp(lgf * jnp.maximum(diff, 0.0)), jnp.exp(lgb * jnp.maximum(-diff, 0.0)))
    ones = jnp.ones((CHUNK, CHUNK), F32)
    row = idx[None, :, None] * ones
    lane = idx[None, None, :] * ones
    xif = jnp.exp(lgf * (row + 1.0))
    xib = jnp.exp(lgb * (CHUNK - row))
    zf = jnp.exp(lgf * (CHUNK - 1.0 - lane))
    zb = jnp.exp(lgb * lane)
    cdf = jnp.exp(lgf * CHUNK) * ones
    cdb = jnp.exp(lgb * CHUNK) * ones
    return jnp.stack([dtot, xif, xib, zf, zb, cdf, cdb], axis=1)


def _encoder(x, mem, p):
    B, S, _ = x.shape
    T = B * S
    tm = _tile(T, TOKEN_TILE)
    ts = _tile(S, TOKEN_TILE)
    rc, rs, ac, as_ = _rope_tables(S)
    xf, xb = _in_ln(x.reshape(T, D_MODEL), p["in_ln_w"], p["in_ln_b"], tm)
    memb = mem.astype(BF16).reshape(B * N_MEM, D_MODEL)
    flat = lambda a: a.reshape(T, D_MODEL)
    seq = lambda a: a.reshape(B, S, D_MODEL)
    for l in range(DEPTH):
        xb3 = seq(xb)
        rq, rkt = _proj_ret(xb3, p["w_ret"][l], rc, rs, ts)
        rv, rgs, sg = _proj_mid(xb3, p["w_mid"][l], p["sg_ln_w"][l], p["sg_ln_b"][l], p["sg_ws"][l], p["sg_bias"][l], ts)
        aq, akt, av, gates = _proj_att_gate(xb3, p["w_att"][l], p["w_gate"][l], p["b_gate"][l], ac, as_,
                                            p["att_qn_w"][l], p["att_kn_w"][l], ts, _tile(S, ATT_TQ))
        gates = gates.reshape(T, N_BRANCH * D_MODEL)
        ro = _retention(rq, rkt, rv, rgs, p["dec"][l], p["ret_gn_w"][l])
        at = _attention(p["att_bounded"][l], aq, akt, av)
        xf, xb = _merge(flat(ro), flat(sg), flat(at), gates, xf, p["ret_wo"][l], p["sg_wo"][l], p["att_wo"][l],
                        p["w_out"][l], p["ln_w"][l, 0], p["ln_b"][l, 0], _tile(T, MERGE_TILE))
        kv = _matmul(memb, p["xa_wkv"][l], _tile(B * N_MEM, 1024), 1024, "proj_kv")
        xf3, xb3 = _cross(seq(xf), seq(xb), kv.reshape(B, N_MEM, 2 * D_MODEL), p["xa_wq"][l], p["xa_wo"][l],
                          p["ln_w"][l, 1], p["ln_b"][l, 1], ts)
        xf, xb = _ffn(flat(xf3), flat(xb3), p["ffn_wa"][l], p["ffn_wb"][l], p["ffn_w_out"][l],
                      p["ln_w"][l, 2], p["ln_b"][l, 2], tm)
    return seq(xf)


def _prepare_params(in_ln_w, in_ln_b, w_in, b_gate, ret_decay_f, ret_decay_b, ret_gn_w, ret_wo, sg_ln_w, sg_ln_b,
                    sg_ws, sg_b, sg_wo, att_qn_w, att_kn_w, att_wo, w_out, ln_w, ln_b, xa_wq, xa_wkv, xa_wo,
                    ffn_w_in, ffn_w_out):
    vec = lambda a: a.astype(F32).reshape(a.shape[:-1] + (1, a.shape[-1]))
    group = lambda c: w_in[:, :, c[0]:c[1]].astype(BF16)
    sg_bias = jnp.repeat(jnp.swapaxes(sg_b.astype(F32), 1, 2), SG_GW, axis=-1)
    score_bound = (SCORE_BOUND_COEF * jnp.max(jnp.abs(att_qn_w.astype(F32)), axis=-1)
                   * jnp.max(jnp.abs(att_kn_w.astype(F32)), axis=-1))
    order = _axial_dim_order()
    w_att = group(COLS_ATT)
    n_qk = ATT_W + ATT_KV_W
    w_qk = w_att[:, :, :n_qk].reshape(DEPTH, D_MODEL, n_qk // ATT_HD, ATT_HD)[..., order].reshape(DEPTH, D_MODEL, n_qk)
    w_att = jnp.concatenate([w_qk, w_att[:, :, n_qk:]], axis=-1)
    with_partner = lambda w: jnp.stack([w[:, order], jnp.roll(w[:, order], ATT_HD // 2, axis=-1)], axis=1).astype(F32)
    att_qn_w = with_partner(att_qn_w)
    att_kn_w = with_partner(att_kn_w)
    return dict(
        att_bounded=(score_bound <= SAFE_SCORE_BOUND).astype(jnp.int32).reshape(DEPTH, 1),
        in_ln_w=vec(in_ln_w), in_ln_b=vec(in_ln_b),
        w_ret=group(COLS_RET), w_mid=group(COLS_MID), w_att=w_att, w_gate=group(COLS_GATE),
        b_gate=vec(b_gate),
        dec=jnp.stack([_decay_tables(ret_decay_f[l], ret_decay_b[l]) for l in range(DEPTH)]),
        ret_gn_w=vec(ret_gn_w), ret_wo=ret_wo.astype(BF16),
        sg_ln_w=vec(sg_ln_w), sg_ln_b=vec(sg_ln_b), sg_ws=sg_ws.astype(BF16), sg_bias=sg_bias,
        sg_wo=sg_wo.astype(BF16),
        att_qn_w=att_qn_w, att_kn_w=att_kn_w, att_wo=att_wo.astype(BF16),
        w_out=w_out.astype(BF16), ln_w=vec(ln_w), ln_b=vec(ln_b),
        xa_wq=xa_wq.astype(BF16), xa_wkv=xa_wkv.astype(BF16), xa_wo=xa_wo.astype(BF16),
        ffn_wa=ffn_w_in[:, :, :D_FF].astype(BF16), ffn_wb=ffn_w_in[:, :, D_FF:].astype(BF16),
        ffn_w_out=ffn_w_out.astype(BF16),
    )


def kernel(x_prompt, x_sample, mem_prompt, mem_sample, in_ln_w, in_ln_b, w_in, b_gate, ret_decay_f, ret_decay_b,
           ret_gn_w, ret_wo, sg_ln_w, sg_ln_b, sg_ws, sg_b, sg_wo, att_qn_w, att_kn_w, att_wo, w_out, ln_w, ln_b,
           xa_wq, xa_wkv, xa_wo, ffn_w_in, ffn_w_out):
    p = _prepare_params(in_ln_w, in_ln_b, w_in, b_gate, ret_decay_f, ret_decay_b, ret_gn_w, ret_wo, sg_ln_w,
                        sg_ln_b, sg_ws, sg_b, sg_wo, att_qn_w, att_kn_w, att_wo, w_out, ln_w, ln_b, xa_wq, xa_wkv,
                        xa_wo, ffn_w_in, ffn_w_out)
    return (_encoder(x_prompt, mem_prompt, p), _encoder(x_sample, mem_sample, p))
```

```python
import functools

import jax
import jax.numpy as jnp
from jax import lax
from jax.experimental import pallas as pl
from jax.experimental.pallas import tpu as pltpu

F32 = jnp.float32
BF16 = jnp.bfloat16

D_MODEL = 1024
DEPTH = 4
N_MEM = 256
GRID_W = 64
CHUNK = 128
RET_HEADS = 8
RET_DK = 128
RET_W = RET_HEADS * RET_DK
SG_GROUPS = 4
SG_GW = 256
SG_W = SG_GROUPS * SG_GW
ATT_HEADS = 8
ATT_KV_HEADS = 2
ATT_GROUP = ATT_HEADS // ATT_KV_HEADS
ATT_HD = 128
ATT_W = ATT_HEADS * ATT_HD
ATT_KV_W = ATT_KV_HEADS * ATT_HD
X_HEADS = 4
X_HD = D_MODEL // X_HEADS
D_FF = 2816
N_BRANCH = 3
ALPHA = (2 * DEPTH) ** 0.25
ROPE_BASE = 10000.0
LN_EPS = 1e-5
RMS_EPS = 1e-6
LOG2E = 1.4426950408889634

COLS_RET = (0, 2 * RET_W)
COLS_MID = (COLS_RET[1], COLS_RET[1] + 2 * RET_W + 2 * SG_W)
COLS_ATT = (COLS_MID[1], COLS_MID[1] + ATT_W + 2 * ATT_KV_W)
COLS_GATE = (COLS_ATT[1], COLS_ATT[1] + N_BRANCH * D_MODEL)

MXU_W = 256
FF_CHUNK = MXU_W
SLAB = 512
Q_PIECE = 32
TOKEN_TILE = 1024
MERGE_TILE = 512
ATT_TQ = 512
ATT_TK = 2048
ATT_SINGLE_KV_MAX = 2048
RET_UNROLL = 8
SCORE_BOUND_COEF = ATT_HD ** 0.5 * LOG2E
SAFE_SCORE_BOUND = 64.0
VMEM_LIMIT = 52 * 1024 * 1024


def _cparams(sem):
    return pltpu.CompilerParams(dimension_semantics=sem, vmem_limit_bytes=VMEM_LIMIT)


def _const_spec(shape):
    nd = len(shape)
    return pl.BlockSpec(shape, lambda *_: (0,) * nd, pipeline_mode=pl.Buffered(1))


def _tile(n, pref):
    t = min(n, pref)
    assert n % t == 0, (n, t)
    return t


def _layer_norm_rows(z, w, b):
    mu = jnp.mean(z, axis=-1, keepdims=True)
    d = z - mu
    var = jnp.mean(d * d, axis=-1, keepdims=True)
    return d * lax.rsqrt(var + LN_EPS) * w + b


def _gelu_tanh(x):
    return 0.5 * x * (1.0 + jnp.tanh(0.7978845608028654 * (x + 0.044715 * (x * x * x))))


def _sigmoid(x):
    return 1.0 / (1.0 + jnp.exp(-x))


def _dot(a, b):
    return jnp.dot(a, b, preferred_element_type=F32)


def _in_ln_kernel(x_ref, w_ref, b_ref, xo_ref, xb_ref):
    y = _layer_norm_rows(x_ref[...], w_ref[...], b_ref[...])
    xo_ref[...] = y
    xb_ref[...] = y.astype(BF16)


def _in_ln(x2, w, b, tm):
    T = x2.shape[0]
    row = pl.BlockSpec((tm, D_MODEL), lambda i: (i, 0))
    return pl.pallas_call(
        _in_ln_kernel,
        grid=(T // tm,),
        in_specs=[row, _const_spec((1, D_MODEL)), _const_spec((1, D_MODEL))],
        out_specs=[row, row],
        out_shape=[jax.ShapeDtypeStruct((T, D_MODEL), F32), jax.ShapeDtypeStruct((T, D_MODEL), BF16)],
        compiler_params=_cparams(("parallel",)),
        name="in_ln",
    )(x2, w, b)


def _mm_kernel(x_ref, w_ref, o_ref):
    o_ref[...] = _dot(x_ref[...], w_ref[...]).astype(o_ref.dtype)


def _matmul(x, w, tm, tn, name):
    M, K = x.shape
    N = w.shape[1]
    return pl.pallas_call(
        _mm_kernel,
        grid=(N // tn, M // tm),
        in_specs=[pl.BlockSpec((tm, K), lambda j, i: (i, 0)), pl.BlockSpec((K, tn), lambda j, i: (0, j))],
        out_specs=pl.BlockSpec((tm, tn), lambda j, i: (i, j)),
        out_shape=jax.ShapeDtypeStruct((M, N), BF16),
        compiler_params=_cparams(("parallel", "parallel")),
        name=name,
    )(x, w)


def _rope(x, cos, sin_signed, partner):
    return x * cos + partner * sin_signed


def _proj_ret_kernel(x_ref, w_ref, rc_ref, rs_ref, rq_ref, rkt_ref, *, tm):
    x = x_ref[0]
    rc = rc_ref[...]
    rs = rs_ref[...]
    k_scale = RET_DK ** -0.5

    def rope(t):
        return _rope(t, rc, rs, pltpu.roll(t, RET_DK // 2, axis=1))

    heads_per_slab = MXU_W // RET_DK
    for slab in range(RET_W // MXU_W):
        q2 = _dot(x, w_ref[:, slab * MXU_W:(slab + 1) * MXU_W])
        k2 = _dot(x, w_ref[:, RET_W + slab * MXU_W:RET_W + (slab + 1) * MXU_W])
        for j in range(heads_per_slab):
            h = slab * heads_per_slab + j
            sl = slice(j * RET_DK, (j + 1) * RET_DK)
            rq_ref[0, h] = rope(q2[:, sl]).astype(BF16)
            kt = (rope(k2[:, sl]) * k_scale).T
            for c in range(tm // CHUNK):
                rkt_ref[0, h, c] = kt[:, c * CHUNK:(c + 1) * CHUNK].astype(BF16)


def _proj_ret(xb3, w, rc, rs, tm):
    B, S, _ = xb3.shape
    tab = pl.BlockSpec((tm, RET_DK), lambda b, s: (s, 0))
    return pl.pallas_call(
        functools.partial(_proj_ret_kernel, tm=tm),
        grid=(B, S // tm),
        in_specs=[pl.BlockSpec((1, tm, D_MODEL), lambda b, s: (b, s, 0)), _const_spec((D_MODEL, 2 * RET_W)), tab, tab],
        out_specs=[pl.BlockSpec((1, RET_HEADS, tm, RET_DK), lambda b, s: (b, 0, s, 0)),
                   pl.BlockSpec((1, RET_HEADS, tm // CHUNK, RET_DK, CHUNK), lambda b, s: (b, 0, s, 0, 0))],
        out_shape=[jax.ShapeDtypeStruct((B, RET_HEADS, S, RET_DK), BF16),
                   jax.ShapeDtypeStruct((B, RET_HEADS, S // CHUNK, RET_DK, CHUNK), BF16)],
        compiler_params=_cparams(("parallel", "parallel")),
        name="proj_ret",
    )(xb3, w, rc, rs)


def _proj_mid_kernel(x_ref, w_ref, lnw_ref, lnb_ref, ws_ref, bias_ref, rv_ref, rgs_ref, sg_ref, vn_ref, *, tm):
    x = x_ref[0]
    for i in range(RET_W // SLAB):
        cols = slice(i * SLAB, (i + 1) * SLAB)
        rv = _dot(x, w_ref[:, cols]).astype(BF16)
        for j in range(SLAB // RET_DK):
            rv_ref[0, i * (SLAB // RET_DK) + j] = rv[:, j * RET_DK:(j + 1) * RET_DK]
        g = _dot(x, w_ref[:, RET_W + i * SLAB:RET_W + (i + 1) * SLAB])
        rgs_ref[0, :, cols] = (g * _sigmoid(g)).astype(BF16)
    sv = _gelu_tanh(_dot(x, w_ref[:, 2 * RET_W + SG_W:2 * RET_W + 2 * SG_W]))
    vn_ref[...] = _layer_norm_rows(sv, lnw_ref[...], lnb_ref[...]).astype(BF16)
    for grp in range(SG_GROUPS):
        cols = slice(grp * SG_GW, (grp + 1) * SG_GW)
        u = _gelu_tanh(_dot(x, w_ref[:, 2 * RET_W + grp * SG_GW:2 * RET_W + (grp + 1) * SG_GW]))
        for c in range(tm // CHUNK):
            rows = slice(c * CHUNK, (c + 1) * CHUNK)
            mixed = _dot(ws_ref[grp], vn_ref[rows, cols]) + bias_ref[:, cols]
            sg_ref[0, rows, cols] = (u[rows, :] * mixed).astype(BF16)


def _proj_mid(xb3, w, lnw, lnb, ws, bias, tm):
    B, S, _ = xb3.shape
    row = pl.BlockSpec((1, tm, D_MODEL), lambda b, s: (b, s, 0))
    out = jax.ShapeDtypeStruct((B, S, D_MODEL), BF16)
    return pl.pallas_call(
        functools.partial(_proj_mid_kernel, tm=tm),
        grid=(B, S // tm),
        in_specs=[row, _const_spec((D_MODEL, 2 * RET_W + 2 * SG_W)), _const_spec((1, SG_W)), _const_spec((1, SG_W)),
                  _const_spec((SG_GROUPS, CHUNK, CHUNK)), _const_spec((CHUNK, SG_W))],
        out_specs=[pl.BlockSpec((1, RET_HEADS, tm, RET_DK), lambda b, s: (b, 0, s, 0)), row, row],
        out_shape=[jax.ShapeDtypeStruct((B, RET_HEADS, S, RET_DK), BF16), out, out],
        scratch_shapes=[pltpu.VMEM((tm, SG_W), BF16)],
        compiler_params=_cparams(("parallel", "parallel")),
        name="proj_mid",
    )(xb3, w, lnw, lnb, ws, bias)


def _proj_att_gate_kernel(x_ref, w_ref, wg_ref, bg_ref, ac_ref, as_ref, qnw_ref, knw_ref,
                          aq_ref, akt_ref, av_ref, gate_ref, *, tm, tq):
    x = x_ref[0]

    n_gate_slabs = N_BRANCH * D_MODEL // MXU_W
    gate_slabs_done = []

    def gate_slab():
        i = len(gate_slabs_done)
        gate_slabs_done.append(i)
        cols = slice(i * MXU_W, (i + 1) * MXU_W)
        gate_ref[0, :, cols] = _sigmoid(_dot(x, wg_ref[:, cols]) + bg_ref[:, cols]).astype(BF16)

    n_q_slabs = ATT_W // MXU_W
    q_scale = ATT_HD ** -0.5 * LOG2E
    q_cos = ac_ref[...] * (qnw_ref[0:1, :] * q_scale)
    q_sin = as_ref[...] * (qnw_ref[1:2, :] * q_scale)
    k_cos = ac_ref[...] * knw_ref[0:1, :]
    k_sin = as_ref[...] * knw_ref[1:2, :]

    def norm_rope(t, cos, sin):
        r = lax.rsqrt(jnp.mean(t * t, axis=-1, keepdims=True) + RMS_EPS)
        return r * (t * cos + pltpu.roll(t, ATT_HD // 2, axis=1) * sin)

    heads_per_slab = MXU_W // ATT_HD
    for slab in range(n_q_slabs):
        q2 = _dot(x, w_ref[:, slab * MXU_W:(slab + 1) * MXU_W])
        for j in range(heads_per_slab):
            kv_head, g = divmod(slab * heads_per_slab + j, ATT_GROUP)
            for c in range(tm // Q_PIECE):
                rows = slice(c * Q_PIECE, (c + 1) * Q_PIECE)
                t = q2[rows, j * ATT_HD:(j + 1) * ATT_HD]
                q_tile, r0 = divmod(c * Q_PIECE, tq)
                aq_ref[0, kv_head, q_tile, g * tq + r0:g * tq + r0 + Q_PIECE, :] = (
                    norm_rope(t, q_cos[rows], q_sin[rows]).astype(BF16))
            gate_slab()
    k2 = _dot(x, w_ref[:, ATT_W:ATT_W + ATT_KV_W])
    for j in range(ATT_KV_HEADS):
        for c in range(tm // CHUNK):
            rows = slice(c * CHUNK, (c + 1) * CHUNK)
            t = k2[rows, j * ATT_HD:(j + 1) * ATT_HD]
            akt_ref[0, j, :, rows] = norm_rope(t, k_cos[rows], k_sin[rows]).T.astype(BF16)
        gate_slab()
    av_ref[0] = _dot(x, w_ref[:, ATT_W + ATT_KV_W:ATT_W + 2 * ATT_KV_W]).astype(BF16)
    while len(gate_slabs_done) < n_gate_slabs:
        gate_slab()


def _proj_att_gate(xb3, w, wg, bg, ac, as_, qnw, knw, tm, tq):
    B, S, _ = xb3.shape
    assert tm % tq == 0, (tm, tq)
    n_gate = N_BRANCH * D_MODEL
    tab = pl.BlockSpec((tm, ATT_HD), lambda b, s: (s, 0))
    return pl.pallas_call(
        functools.partial(_proj_att_gate_kernel, tm=tm, tq=tq),
        grid=(B, S // tm),
        in_specs=[pl.BlockSpec((1, tm, D_MODEL), lambda b, s: (b, s, 0)), _const_spec((D_MODEL, ATT_W + 2 * ATT_KV_W)),
                  _const_spec((D_MODEL, n_gate)), _const_spec((1, n_gate)),
                  tab, tab, _const_spec((2, ATT_HD)), _const_spec((2, ATT_HD))],
        out_specs=[pl.BlockSpec((1, ATT_KV_HEADS, tm // tq, ATT_GROUP * tq, ATT_HD), lambda b, s: (b, 0, s, 0, 0)),
                   pl.BlockSpec((1, ATT_KV_HEADS, ATT_HD, tm), lambda b, s: (b, 0, 0, s)),
                   pl.BlockSpec((1, tm, ATT_KV_W), lambda b, s: (b, s, 0)),
                   pl.BlockSpec((1, tm, n_gate), lambda b, s: (b, s, 0))],
        out_shape=[jax.ShapeDtypeStruct((B, ATT_KV_HEADS, S // tq, ATT_GROUP * tq, ATT_HD), BF16),
                   jax.ShapeDtypeStruct((B, ATT_KV_HEADS, ATT_HD, S), BF16),
                   jax.ShapeDtypeStruct((B, S, ATT_KV_W), BF16),
                   jax.ShapeDtypeStruct((B, S, n_gate), BF16)],
        compiler_params=_cparams(("parallel", "parallel")),
        name="proj_att_gate",
    )(xb3, w, wg, bg, ac, as_, qnw, knw)


def _ret_kernel(q_ref, kt_ref, v_ref, dec_ref, gnw_ref, o_ref, st_ref, p_ref, *tot_refs, n_chunks, unroll):
    dtot = dec_ref[0, 0]
    xif, xib = dec_ref[0, 1], dec_ref[0, 2]
    zf, zb = dec_ref[0, 3], dec_ref[0, 4]
    cdf, cdb = dec_ref[0, 5], dec_ref[0, 6]
    gnw = gnw_ref[...]

    def rows_of(c):
        if isinstance(c, int):
            return pl.ds(c * CHUNK, CHUNK)
        return pl.ds(pl.multiple_of(c * CHUNK, CHUNK), CHUNK)

    def scan(i, carry):
        sf, sb = carry
        cf, cb = i, n_chunks - 1 - i
        st_ref[cf, :, :RET_DK] = sf.astype(BF16)
        st_ref[cb, :, RET_DK:] = sb.astype(BF16)
        kzf = (kt_ref[0, 0, cf].astype(F32) * zf).astype(BF16)
        kzb = (kt_ref[0, 0, cb].astype(F32) * zb).astype(BF16)
        sf = sf * cdf + _dot(kzf, v_ref[0, 0, rows_of(cf), :])
        sb = sb * cdb + _dot(kzb, v_ref[0, 0, rows_of(cb), :])
        s = _dot(q_ref[0, 0, rows_of(i), :], kt_ref[0, 0, i])
        p_ref[i] = (s * dtot).astype(BF16)
        return sf, sb

    zero = jnp.zeros((RET_DK, RET_DK), F32)
    lax.fori_loop(0, n_chunks, scan, (zero, zero), unroll=unroll)

    group = len(tot_refs)

    def mix_group(j):
        for k, tot_ref in enumerate(tot_refs):
            c = j * group + k
            rows = rows_of(c)
            inter = _dot(q_ref[0, 0, rows, :], st_ref[c])
            tot_ref[...] = _dot(p_ref[c], v_ref[0, 0, rows, :]) + inter[:, :RET_DK] * xif + inter[:, RET_DK:] * xib

    def norm_group(j):
        for k, tot_ref in enumerate(tot_refs):
            base = (j * group + k) * CHUNK
            for r in range(0, CHUNK, Q_PIECE):
                rows = pl.ds(base + r, Q_PIECE) if isinstance(base, int) else pl.ds(
                    pl.multiple_of(base + r, Q_PIECE), Q_PIECE)
                tot = tot_ref[r:r + Q_PIECE, :]
                mu = jnp.mean(tot, axis=-1, keepdims=True)
                var = jnp.maximum(jnp.mean(tot * tot, axis=-1, keepdims=True) - mu * mu, 0.0)
                y = (tot - mu) * lax.rsqrt(var + LN_EPS) * gnw
                o_ref[0, rows, :] = y.astype(BF16)

    mix_group(0)

    def step(j, carry):
        norm_group(j - 1)
        mix_group(j)
        return carry

    n_groups = n_chunks // group
    lax.fori_loop(1, n_groups, step, 0)
    norm_group(n_groups - 1)


def _retention(rq, rkt, rv, dec, gnw):
    B, _, S, _ = rq.shape
    n = S // CHUNK
    group = min(RET_UNROLL, n)
    assert n % group == 0, (n, group)
    head = pl.BlockSpec((1, 1, S, RET_DK), lambda b, h: (b, h, 0, 0))
    return pl.pallas_call(
        functools.partial(_ret_kernel, n_chunks=n, unroll=min(RET_UNROLL, n)),
        grid=(B, RET_HEADS),
        in_specs=[head,
                  pl.BlockSpec((1, 1, n, RET_DK, CHUNK), lambda b, h: (b, h, 0, 0, 0)),
                  head,
                  pl.BlockSpec((1, 7, CHUNK, CHUNK), lambda b, h: (h, 0, 0, 0)),
                  pl.BlockSpec((1, RET_DK), lambda b, h: (0, h))],
        out_specs=pl.BlockSpec((1, S, RET_DK), lambda b, h: (b, 0, h)),
        out_shape=jax.ShapeDtypeStruct((B, S, RET_W), BF16),
        scratch_shapes=([pltpu.VMEM((n, RET_DK, 2 * RET_DK), BF16), pltpu.VMEM((n, CHUNK, CHUNK), BF16)]
                        + [pltpu.VMEM((CHUNK, RET_DK), F32)] * group),
        compiler_params=_cparams(("parallel", "parallel")),
        name="retention",
    )(rq, rkt, rv, dec, gnw)


def _attn_kernel(bounded_ref, q_ref, kt_ref, v_ref, o_ref, *scratch, tq, tk, single_kv):
    q = q_ref[0, 0, 0]
    kt = kt_ref[0, 0]
    v1 = jnp.concatenate([v_ref[0], jnp.ones((tk, ATT_HD), BF16)], axis=1)
    bounded = bounded_ref[0] == 1

    def write(pv):
        o = pv[:, :ATT_HD] / pv[:, ATT_HD:]
        for g in range(ATT_GROUP):
            o_ref[0, :, g * ATT_HD:(g + 1) * ATT_HD] = o[g * tq:(g + 1) * tq].astype(BF16)

    if single_kv:
        @pl.when(bounded)
        def _():
            write(_dot(jnp.exp2(_dot(q, kt)).astype(BF16), v1))

        @pl.when(jnp.logical_not(bounded))
        def _():
            s = _dot(q, kt)
            write(_dot(jnp.exp2(s - jnp.max(s, axis=1, keepdims=True)).astype(BF16), v1))

        return

    m_ref, acc_ref = scratch
    ki = pl.program_id(3)

    @pl.when(ki == 0)
    def _():
        m_ref[...] = jnp.full(m_ref.shape, -jnp.inf, F32)
        acc_ref[...] = jnp.zeros(acc_ref.shape, F32)

    @pl.when(bounded)
    def _():
        acc_ref[...] += _dot(jnp.exp2(_dot(q, kt)).astype(BF16), v1)

    @pl.when(jnp.logical_not(bounded))
    def _():
        s = _dot(q, kt)
        m_prev = m_ref[...]
        m_next = jnp.maximum(m_prev, jnp.max(s, axis=1, keepdims=True))
        p = jnp.exp2(s - jnp.tile(m_next, (1, tk // ATT_HD)))
        alpha = jnp.exp2(m_prev - m_next)
        acc_ref[...] = jnp.tile(alpha, (1, 2)) * acc_ref[...] + _dot(p.astype(BF16), v1)
        m_ref[...] = m_next

    @pl.when(ki == pl.num_programs(3) - 1)
    def _():
        write(acc_ref[...])


def _attention(bounded, aq, akt, av):
    B, S, _ = av.shape
    tq = aq.shape[3] // ATT_GROUP
    single_kv = S <= ATT_SINGLE_KV_MAX
    tk = S if single_kv else _tile(S, ATT_TK)
    rows = ATT_GROUP * tq
    scratch = [] if single_kv else [pltpu.VMEM((rows, ATT_HD), F32), pltpu.VMEM((rows, 2 * ATT_HD), F32)]
    return pl.pallas_call(
        functools.partial(_attn_kernel, tq=tq, tk=tk, single_kv=single_kv),
        grid=(B, ATT_KV_HEADS, S // tq, S // tk),
        in_specs=[pl.BlockSpec(memory_space=pltpu.SMEM),
                  pl.BlockSpec((1, 1, 1, ATT_GROUP * tq, ATT_HD), lambda b, k, i, j: (b, k, i, 0, 0)),
                  pl.BlockSpec((1, 1, ATT_HD, tk), lambda b, k, i, j: (b, k, 0, j)),
                  pl.BlockSpec((1, tk, ATT_HD), lambda b, k, i, j: (b, j, k))],
        out_specs=pl.BlockSpec((1, tq, ATT_GROUP * ATT_HD), lambda b, k, i, j: (b, i, k)),
        out_shape=jax.ShapeDtypeStruct((B, S, ATT_W), BF16),
        scratch_shapes=scratch,
        compiler_params=_cparams(("parallel", "parallel", "parallel", "arbitrary")),
        name="gqa_attention",
    )(bounded, aq, akt, av)


def _merge_kernel(ro_ref, rgs_ref, sg_ref, at_ref, gate_ref, x_ref, wr_ref, ws_ref, wa_ref, wo_ref, lnw_ref, lnb_ref,
                  xo_ref, xb_ref):
    def branch(a, w_ref, i):
        return gate_ref[:, i * D_MODEL:(i + 1) * D_MODEL].astype(F32) * _dot(a, w_ref[...])

    ret_in = ro_ref[...] * rgs_ref[...]
    merged = branch(ret_in, wr_ref, 0) + branch(sg_ref[...], ws_ref, 1) + branch(at_ref[...], wa_ref, 2)
    y = _dot(merged.astype(BF16), wo_ref[...])
    out = _layer_norm_rows(ALPHA * x_ref[...] + y, lnw_ref[...], lnb_ref[...])
    xo_ref[...] = out
    xb_ref[...] = out.astype(BF16)


def _merge(ro, rgs, sg, at, gates, x, wr, ws, wa, wo, lnw, lnb, tm):
    T = x.shape[0]
    row = pl.BlockSpec((tm, D_MODEL), lambda i: (i, 0))
    wspec = _const_spec((D_MODEL, D_MODEL))
    vec = _const_spec((1, D_MODEL))
    return pl.pallas_call(
        _merge_kernel,
        grid=(T // tm,),
        in_specs=[row, row, row, row, pl.BlockSpec((tm, N_BRANCH * D_MODEL), lambda i: (i, 0)), row,
                  wspec, wspec, wspec, wspec, vec, vec],
        out_specs=[row, row],
        out_shape=[jax.ShapeDtypeStruct((T, D_MODEL), F32), jax.ShapeDtypeStruct((T, D_MODEL), BF16)],
        compiler_params=_cparams(("parallel",)),
        name="merge_out_ln",
    )(ro, rgs, sg, at, gates, x, wr, ws, wa, wo, lnw, lnb)


def _cross_kernel(x_ref, xb_ref, kv_ref, wq_ref, wo_ref, lnw_ref, lnb_ref, xo_ref, xbo_ref, o_scr):
    q = _dot(xb_ref[0], wq_ref[...]).astype(BF16)
    scale = X_HD ** -0.5
    for h in range(X_HEADS):
        cols = slice(h * X_HD, (h + 1) * X_HD)
        k = kv_ref[0, :, cols]
        v = kv_ref[0, :, D_MODEL + h * X_HD:D_MODEL + (h + 1) * X_HD]
        s = lax.dot_general(q[:, cols], k, (((1,), (1,)), ((), ())), preferred_element_type=F32) * scale
        p = jnp.exp(s - jnp.max(s, axis=-1, keepdims=True))
        o = _dot(p.astype(BF16), v) / jnp.sum(p, axis=-1, keepdims=True)
        o_scr[:, cols] = o.astype(BF16)
    y = _dot(o_scr[...], wo_ref[...])
    out = _layer_norm_rows(ALPHA * x_ref[0] + y, lnw_ref[...], lnb_ref[...])
    xo_ref[0] = out
    xbo_ref[0] = out.astype(BF16)


def _cross(x3, xb3, kv3, wq, wo, lnw, lnb, tm):
    B, S, _ = x3.shape
    row = pl.BlockSpec((1, tm, D_MODEL), lambda b, i: (b, i, 0))
    wspec = _const_spec((D_MODEL, D_MODEL))
    vec = _const_spec((1, D_MODEL))
    return pl.pallas_call(
        _cross_kernel,
        grid=(B, S // tm),
        in_specs=[row, row, pl.BlockSpec((1, N_MEM, 2 * D_MODEL), lambda b, i: (b, 0, 0)), wspec, wspec, vec, vec],
        out_specs=[row, row],
        out_shape=[jax.ShapeDtypeStruct((B, S, D_MODEL), F32), jax.ShapeDtypeStruct((B, S, D_MODEL), BF16)],
        scratch_shapes=[pltpu.VMEM((tm, D_MODEL), BF16)],
        compiler_params=_cparams(("parallel", "parallel")),
        name="cross_attn_ln",
    )(x3, xb3, kv3, wq, wo, lnw, lnb)


def _ffn_kernel(x_ref, xb_ref, wa_ref, wb_ref, wo_ref, lnw_ref, lnb_ref, xo_ref, xbo_ref, h_scr):
    xb = xb_ref[...]
    for j in range(D_FF // FF_CHUNK):
        cols = slice(j * FF_CHUNK, (j + 1) * FF_CHUNK)
        a = _dot(xb, wa_ref[:, cols])
        b = _dot(xb, wb_ref[:, cols])
        h_scr[:, cols] = (a * _sigmoid(a) * b).astype(BF16)
    y = _dot(h_scr[...], wo_ref[...])
    out = _layer_norm_rows(ALPHA * x_ref[...] + y, lnw_ref[...], lnb_ref[...])
    xo_ref[...] = out
    xbo_ref[...] = out.astype(BF16)


def _ffn(x, xb, wa, wb, wo, lnw, lnb, tm):
    T = x.shape[0]
    row = pl.BlockSpec((tm, D_MODEL), lambda i: (i, 0))
    vec = _const_spec((1, D_MODEL))
    return pl.pallas_call(
        _ffn_kernel,
        grid=(T // tm,),
        in_specs=[row, row, _const_spec((D_MODEL, D_FF)), _const_spec((D_MODEL, D_FF)),
                  _const_spec((D_FF, D_MODEL)), vec, vec],
        out_specs=[row, row],
        out_shape=[jax.ShapeDtypeStruct((T, D_MODEL), F32), jax.ShapeDtypeStruct((T, D_MODEL), BF16)],
        scratch_shapes=[pltpu.VMEM((tm, D_FF), BF16)],
        compiler_params=_cparams(("parallel",)),
        name="swiglu_ln",
    )(x, xb, wa, wb, wo, lnw, lnb)


def _rope_tables(S):
    t = jnp.arange(S)

    def cos_sin(pos, dim):
        inv_freq = ROPE_BASE ** (-jnp.arange(0, dim, 2, dtype=F32) / dim)
        ang = pos.astype(F32)[:, None] * inv_freq[None, :]
        return jnp.cos(ang), jnp.sin(ang)

    ct, st = cos_sin(t, RET_DK)
    cr, sr = cos_sin(t // GRID_W, ATT_HD // 2)
    cc, sc = cos_sin(t % GRID_W, ATT_HD // 2)
    return (jnp.concatenate([ct, ct], -1), jnp.concatenate([-st, st], -1),
            jnp.concatenate([cr, cc, cr, cc], -1), jnp.concatenate([-sr, -sc, sr, sc], -1))


def _axial_dim_order():
    quarter = ATT_HD // 4
    blocks = [jnp.arange(i * quarter, (i + 1) * quarter) for i in (0, 2, 1, 3)]
    return jnp.concatenate(blocks)


def _decay_tables(decay_f, decay_b):
    lgf = jax.nn.log_sigmoid(decay_f.astype(F32))[:, None, None]
    lgb = jax.nn.log_sigmoid(decay_b.astype(F32))[:, None, None]
    idx = jnp.arange(CHUNK, dtype=F32)
    diff = idx[:, None] - idx[None, :]
    dtot = jnp.where(diff >= 0, jnp.exp(lgf * jnp.maximum(diff, 0.0)), jnp.exp(lgb * jnp.maximum(-diff, 0.0)))
    ones = jnp.ones((CHUNK, CHUNK), F32)
    row = idx[None, :, None] * ones
    lane = idx[None, None, :] * ones
    xif = jnp.exp(lgf * (row + 1.0))
    xib = jnp.exp(lgb * (CHUNK - row))
    zf = jnp.exp(lgf * (CHUNK - 1.0 - lane))
    zb = jnp.exp(lgb * lane)
    cdf = jnp.exp(lgf * CHUNK) * ones
    cdb = jnp.exp(lgb * CHUNK) * ones
    return jnp.stack([dtot, xif, xib, zf, zb, cdf, cdb], axis=1)


def _encoder(x, mem, p):
    B, S, _ = x.shape
    T = B * S
    tm = _tile(T, TOKEN_TILE)
    ts = _tile(S, TOKEN_TILE)
    rc, rs, ac, as_ = _rope_tables(S)
    xf, xb = _in_ln(x.reshape(T, D_MODEL), p["in_ln_w"], p["in_ln_b"], tm)
    memb = mem.astype(BF16).reshape(B * N_MEM, D_MODEL)
    flat = lambda a: a.reshape(T, D_MODEL)
    seq = lambda a: a.reshape(B, S, D_MODEL)
    for l in range(DEPTH):
        xb3 = seq(xb)
        rq, rkt = _proj_ret(xb3, p["w_ret"][l], rc, rs, ts)
        rv, rgs, sg = _proj_mid(xb3, p["w_mid"][l], p["sg_ln_w"][l], p["sg_ln_b"][l], p["sg_ws"][l], p["sg_bias"][l], ts)
        aq, akt, av, gates = _proj_att_gate(xb3, p["w_att"][l], p["w_gate"][l], p["b_gate"][l], ac, as_,
                                            p["att_qn_w"][l], p["att_kn_w"][l], ts, _tile(S, ATT_TQ))
        gates = gates.reshape(T, N_BRANCH * D_MODEL)
        ro = _retention(rq, rkt, rv, p["dec"][l], p["ret_gn_w"][l])
        at = _attention(p["att_bounded"][l], aq, akt, av)
        xf, xb = _merge(flat(ro), flat(rgs), flat(sg), flat(at), gates, xf, p["ret_wo"][l], p["sg_wo"][l], p["att_wo"][l],
                        p["w_out"][l], p["ln_w"][l, 0], p["ln_b"][l, 0], _tile(T, MERGE_TILE))
        kv = _matmul(memb, p["xa_wkv"][l], _tile(B * N_MEM, 1024), 1024, "proj_kv")
        xf3, xb3 = _cross(seq(xf), seq(xb), kv.reshape(B, N_MEM, 2 * D_MODEL), p["xa_wq"][l], p["xa_wo"][l],
                          p["ln_w"][l, 1], p["ln_b"][l, 1], ts)
        xf, xb = _ffn(flat(xf3), flat(xb3), p["ffn_wa"][l], p["ffn_wb"][l], p["ffn_w_out"][l],
                      p["ln_w"][l, 2], p["ln_b"][l, 2], tm)
    return seq(xf)


def _prepare_params(in_ln_w, in_ln_b, w_in, b_gate, ret_decay_f, ret_decay_b, ret_gn_w, ret_wo, sg_ln_w, sg_ln_b,
                    sg_ws, sg_b, sg_wo, att_qn_w, att_kn_w, att_wo, w_out, ln_w, ln_b, xa_wq, xa_wkv, xa_wo,
                    ffn_w_in, ffn_w_out):
    vec = lambda a: a.astype(F32).reshape(a.shape[:-1] + (1, a.shape[-1]))
    group = lambda c: w_in[:, :, c[0]:c[1]].astype(BF16)
    sg_bias = jnp.repeat(jnp.swapaxes(sg_b.astype(F32), 1, 2), SG_GW, axis=-1)
    score_bound = (SCORE_BOUND_COEF * jnp.max(jnp.abs(att_qn_w.astype(F32)), axis=-1)
                   * jnp.max(jnp.abs(att_kn_w.astype(F32)), axis=-1))
    order = _axial_dim_order()
    w_att = group(COLS_ATT)
    n_qk = ATT_W + ATT_KV_W
    w_qk = w_att[:, :, :n_qk].reshape(DEPTH, D_MODEL, n_qk // ATT_HD, ATT_HD)[..., order].reshape(DEPTH, D_MODEL, n_qk)
    w_att = jnp.concatenate([w_qk, w_att[:, :, n_qk:]], axis=-1)
    with_partner = lambda w: jnp.stack([w[:, order], jnp.roll(w[:, order], ATT_HD // 2, axis=-1)], axis=1).astype(F32)
    att_qn_w = with_partner(att_qn_w)
    att_kn_w = with_partner(att_kn_w)
    return dict(
        att_bounded=(score_bound <= SAFE_SCORE_BOUND).astype(jnp.int32).reshape(DEPTH, 1),
        in_ln_w=vec(in_ln_w), in_ln_b=vec(in_ln_b),
        w_ret=group(COLS_RET), w_mid=group(COLS_MID), w_att=w_att, w_gate=group(COLS_GATE),
        b_gate=vec(b_gate),
        dec=jnp.stack([_decay_tables(ret_decay_f[l], ret_decay_b[l]) for l in range(DEPTH)]),
        ret_gn_w=vec(ret_gn_w), ret_wo=ret_wo.astype(BF16),
        sg_ln_w=vec(sg_ln_w), sg_ln_b=vec(sg_ln_b), sg_ws=sg_ws.astype(BF16), sg_bias=sg_bias,
        sg_wo=sg_wo.astype(BF16),
        att_qn_w=att_qn_w, att_kn_w=att_kn_w, att_wo=att_wo.astype(BF16),
        w_out=w_out.astype(BF16), ln_w=vec(ln_w), ln_b=vec(ln_b),
        xa_wq=xa_wq.astype(BF16), xa_wkv=xa_wkv.astype(BF16), xa_wo=xa_wo.astype(BF16),
        ffn_wa=ffn_w_in[:, :, :D_FF].astype(BF16), ffn_wb=ffn_w_in[:, :, D_FF:].astype(BF16),
        ffn_w_out=ffn_w_out.astype(BF16),
    )


def kernel(x_prompt, x_sample, mem_prompt, mem_sample, in_ln_w, in_ln_b, w_in, b_gate, ret_decay_f, ret_decay_b,
           ret_gn_w, ret_wo, sg_ln_w, sg_ln_b, sg_ws, sg_b, sg_wo, att_qn_w, att_kn_w, att_wo, w_out, ln_w, ln_b,
           xa_wq, xa_wkv, xa_wo, ffn_w_in, ffn_w_out):
    p = _prepare_params(in_ln_w, in_ln_b, w_in, b_gate, ret_decay_f, ret_decay_b, ret_gn_w, ret_wo, sg_ln_w,
                        sg_ln_b, sg_ws, sg_b, sg_wo, att_qn_w, att_kn_w, att_wo, w_out, ln_w, ln_b, xa_wq, xa_wkv,
                        xa_wo, ffn_w_in, ffn_w_out)
    return (_encoder(x_prompt, mem_prompt, p), _encoder(x_sample, mem_sample, p))
```

```python
import functools

import jax
import jax.numpy as jnp
from jax import lax
from jax.experimental import pallas as pl
from jax.experimental.pallas import tpu as pltpu

F32 = jnp.float32
BF16 = jnp.bfloat16

D_MODEL = 1024
DEPTH = 4
N_MEM = 256
GRID_W = 64
CHUNK = 128
RET_HEADS = 8
RET_DK = 128
RET_W = RET_HEADS * RET_DK
SG_GROUPS = 4
SG_GW = 256
SG_W = SG_GROUPS * SG_GW
ATT_HEADS = 8
ATT_KV_HEADS = 2
ATT_GROUP = ATT_HEADS // ATT_KV_HEADS
ATT_HD = 128
ATT_W = ATT_HEADS * ATT_HD
ATT_KV_W = ATT_KV_HEADS * ATT_HD
X_HEADS = 4
X_HD = D_MODEL // X_HEADS
D_FF = 2816
N_BRANCH = 3
ALPHA = (2 * DEPTH) ** 0.25
ROPE_BASE = 10000.0
LN_EPS = 1e-5
RMS_EPS = 1e-6
LOG2E = 1.4426950408889634

COLS_RET = (0, 2 * RET_W)
COLS_MID = (COLS_RET[1], COLS_RET[1] + 2 * RET_W + 2 * SG_W)
COLS_ATT = (COLS_MID[1], COLS_MID[1] + ATT_W + 2 * ATT_KV_W)
COLS_GATE = (COLS_ATT[1], COLS_ATT[1] + N_BRANCH * D_MODEL)

MXU_W = 256
FF_CHUNK = MXU_W
SLAB = 512
Q_PIECE = 32
TOKEN_TILE = 1024
MERGE_TILE = 512
ATT_TQ = 512
ATT_TQ_STREAM = 256
ATT_TK = 4096
ATT_SINGLE_KV_MAX = 2048
RET_UNROLL = 16
SCORE_BOUND_COEF = ATT_HD ** 0.5 * LOG2E
SAFE_SCORE_BOUND = 64.0
VMEM_LIMIT = 52 * 1024 * 1024


def _cparams(sem):
    return pltpu.CompilerParams(dimension_semantics=sem, vmem_limit_bytes=VMEM_LIMIT)


def _const_spec(shape):
    nd = len(shape)
    return pl.BlockSpec(shape, lambda *_: (0,) * nd, pipeline_mode=pl.Buffered(1))


def _tile(n, pref):
    t = min(n, pref)
    assert n % t == 0, (n, t)
    return t


def _layer_norm_rows(z, w, b):
    mu = jnp.mean(z, axis=-1, keepdims=True)
    d = z - mu
    var = jnp.mean(d * d, axis=-1, keepdims=True)
    return d * lax.rsqrt(var + LN_EPS) * w + b


def _gelu_tanh(x):
    return 0.5 * x * (1.0 + jnp.tanh(0.7978845608028654 * (x + 0.044715 * (x * x * x))))


def _sigmoid(x):
    return 1.0 / (1.0 + jnp.exp(-x))


def _dot(a, b):
    return jnp.dot(a, b, preferred_element_type=F32)


def _in_ln_kernel(x_ref, w_ref, b_ref, xo_ref, xb_ref):
    y = _layer_norm_rows(x_ref[...], w_ref[...], b_ref[...])
    xo_ref[...] = y
    xb_ref[...] = y.astype(BF16)


def _in_ln(x2, w, b, tm):
    T = x2.shape[0]
    row = pl.BlockSpec((tm, D_MODEL), lambda i: (i, 0))
    return pl.pallas_call(
        _in_ln_kernel,
        grid=(T // tm,),
        in_specs=[row, _const_spec((1, D_MODEL)), _const_spec((1, D_MODEL))],
        out_specs=[row, row],
        out_shape=[jax.ShapeDtypeStruct((T, D_MODEL), F32), jax.ShapeDtypeStruct((T, D_MODEL), BF16)],
        compiler_params=_cparams(("parallel",)),
        name="in_ln",
    )(x2, w, b)


def _mm_kernel(x_ref, w_ref, o_ref):
    o_ref[...] = _dot(x_ref[...], w_ref[...]).astype(o_ref.dtype)


def _matmul(x, w, tm, tn, name):
    M, K = x.shape
    N = w.shape[1]
    return pl.pallas_call(
        _mm_kernel,
        grid=(N // tn, M // tm),
        in_specs=[pl.BlockSpec((tm, K), lambda j, i: (i, 0)), pl.BlockSpec((K, tn), lambda j, i: (0, j))],
        out_specs=pl.BlockSpec((tm, tn), lambda j, i: (i, j)),
        out_shape=jax.ShapeDtypeStruct((M, N), BF16),
        compiler_params=_cparams(("parallel", "parallel")),
        name=name,
    )(x, w)


def _rope(x, cos, sin_signed, partner):
    return x * cos + partner * sin_signed


def _proj_ret_kernel(x_ref, w_ref, rc_ref, rs_ref, rq_ref, rkt_ref, *, tm):
    x = x_ref[0]
    rc = rc_ref[...]
    rs = rs_ref[...]
    k_scale = RET_DK ** -0.5

    def rope(t):
        return _rope(t, rc, rs, pltpu.roll(t, RET_DK // 2, axis=1))

    heads_per_slab = MXU_W // RET_DK
    for slab in range(RET_W // MXU_W):
        q2 = _dot(x, w_ref[:, slab * MXU_W:(slab + 1) * MXU_W])
        k2 = _dot(x, w_ref[:, RET_W + slab * MXU_W:RET_W + (slab + 1) * MXU_W])
        for j in range(heads_per_slab):
            h = slab * heads_per_slab + j
            sl = slice(j * RET_DK, (j + 1) * RET_DK)
            rq_ref[0, h] = rope(q2[:, sl]).astype(BF16)
            kt = (rope(k2[:, sl]) * k_scale).T
            for c in range(tm // CHUNK):
                rkt_ref[0, h, c] = kt[:, c * CHUNK:(c + 1) * CHUNK].astype(BF16)


def _proj_ret(xb3, w, rc, rs, tm):
    B, S, _ = xb3.shape
    tab = pl.BlockSpec((tm, RET_DK), lambda b, s: (s, 0))
    return pl.pallas_call(
        functools.partial(_proj_ret_kernel, tm=tm),
        grid=(B, S // tm),
        in_specs=[pl.BlockSpec((1, tm, D_MODEL), lambda b, s: (b, s, 0)), _const_spec((D_MODEL, 2 * RET_W)), tab, tab],
        out_specs=[pl.BlockSpec((1, RET_HEADS, tm, RET_DK), lambda b, s: (b, 0, s, 0)),
                   pl.BlockSpec((1, RET_HEADS, tm // CHUNK, RET_DK, CHUNK), lambda b, s: (b, 0, s, 0, 0))],
        out_shape=[jax.ShapeDtypeStruct((B, RET_HEADS, S, RET_DK), BF16),
                   jax.ShapeDtypeStruct((B, RET_HEADS, S // CHUNK, RET_DK, CHUNK), BF16)],
        compiler_params=_cparams(("parallel", "parallel")),
        name="proj_ret",
    )(xb3, w, rc, rs)


def _proj_mid_kernel(x_ref, w_ref, lnw_ref, lnb_ref, ws_ref, bias_ref, rv_ref, rgs_ref, sg_ref, vn_ref, *, tm):
    x = x_ref[0]
    for i in range(RET_W // SLAB):
        cols = slice(i * SLAB, (i + 1) * SLAB)
        rv = _dot(x, w_ref[:, cols]).astype(BF16)
        for j in range(SLAB // RET_DK):
            rv_ref[0, i * (SLAB // RET_DK) + j] = rv[:, j * RET_DK:(j + 1) * RET_DK]
        g = _dot(x, w_ref[:, RET_W + i * SLAB:RET_W + (i + 1) * SLAB])
        rgs_ref[0, :, cols] = (g * _sigmoid(g)).astype(BF16)
    sv = _gelu_tanh(_dot(x, w_ref[:, 2 * RET_W + SG_W:2 * RET_W + 2 * SG_W]))
    vn_ref[...] = _layer_norm_rows(sv, lnw_ref[...], lnb_ref[...]).astype(BF16)
    for grp in range(SG_GROUPS):
        cols = slice(grp * SG_GW, (grp + 1) * SG_GW)
        u = _gelu_tanh(_dot(x, w_ref[:, 2 * RET_W + grp * SG_GW:2 * RET_W + (grp + 1) * SG_GW]))
        for c in range(tm // CHUNK):
            rows = slice(c * CHUNK, (c + 1) * CHUNK)
            mixed = _dot(ws_ref[grp], vn_ref[rows, cols]) + bias_ref[:, cols]
            sg_ref[0, rows, cols] = (u[rows, :] * mixed).astype(BF16)


def _proj_mid(xb3, w, lnw, lnb, ws, bias, tm):
    B, S, _ = xb3.shape
    row = pl.BlockSpec((1, tm, D_MODEL), lambda b, s: (b, s, 0))
    out = jax.ShapeDtypeStruct((B, S, D_MODEL), BF16)
    return pl.pallas_call(
        functools.partial(_proj_mid_kernel, tm=tm),
        grid=(B, S // tm),
        in_specs=[row, _const_spec((D_MODEL, 2 * RET_W + 2 * SG_W)), _const_spec((1, SG_W)), _const_spec((1, SG_W)),
                  _const_spec((SG_GROUPS, CHUNK, CHUNK)), _const_spec((CHUNK, SG_W))],
        out_specs=[pl.BlockSpec((1, RET_HEADS, tm, RET_DK), lambda b, s: (b, 0, s, 0)), row, row],
        out_shape=[jax.ShapeDtypeStruct((B, RET_HEADS, S, RET_DK), BF16), out, out],
        scratch_shapes=[pltpu.VMEM((tm, SG_W), BF16)],
        compiler_params=_cparams(("parallel", "parallel")),
        name="proj_mid",
    )(xb3, w, lnw, lnb, ws, bias)


def _proj_att_gate_kernel(x_ref, w_ref, wg_ref, bg_ref, ac_ref, as_ref, qnw_ref, knw_ref,
                          aq_ref, akt_ref, av_ref, gate_ref, *, tm, tq):
    x = x_ref[0]

    n_gate_slabs = N_BRANCH * D_MODEL // MXU_W
    gate_slabs_done = []

    def gate_slab():
        i = len(gate_slabs_done)
        gate_slabs_done.append(i)
        cols = slice(i * MXU_W, (i + 1) * MXU_W)
        gate_ref[0, :, cols] = _sigmoid(_dot(x, wg_ref[:, cols]) + bg_ref[:, cols]).astype(BF16)

    n_q_slabs = ATT_W // MXU_W
    q_scale = ATT_HD ** -0.5 * LOG2E
    q_cos = ac_ref[...] * (qnw_ref[0:1, :] * q_scale)
    q_sin = as_ref[...] * (qnw_ref[1:2, :] * q_scale)
    k_cos = ac_ref[...] * knw_ref[0:1, :]
    k_sin = as_ref[...] * knw_ref[1:2, :]

    def norm_rope(t, cos, sin):
        r = lax.rsqrt(jnp.mean(t * t, axis=-1, keepdims=True) + RMS_EPS)
        return r * (t * cos + pltpu.roll(t, ATT_HD // 2, axis=1) * sin)

    heads_per_slab = MXU_W // ATT_HD
    for slab in range(n_q_slabs):
        q2 = _dot(x, w_ref[:, slab * MXU_W:(slab + 1) * MXU_W])
        for j in range(heads_per_slab):
            kv_head, g = divmod(slab * heads_per_slab + j, ATT_GROUP)
            for c in range(tm // Q_PIECE):
                rows = slice(c * Q_PIECE, (c + 1) * Q_PIECE)
                t = q2[rows, j * ATT_HD:(j + 1) * ATT_HD]
                q_tile, r0 = divmod(c * Q_PIECE, tq)
                aq_ref[0, kv_head, q_tile, g * tq + r0:g * tq + r0 + Q_PIECE, :] = (
                    norm_rope(t, q_cos[rows], q_sin[rows]).astype(BF16))
            gate_slab()
    k2 = _dot(x, w_ref[:, ATT_W:ATT_W + ATT_KV_W])
    for j in range(ATT_KV_HEADS):
        for c in range(tm // CHUNK):
            rows = slice(c * CHUNK, (c + 1) * CHUNK)
            t = k2[rows, j * ATT_HD:(j + 1) * ATT_HD]
            akt_ref[0, j, :, rows] = norm_rope(t, k_cos[rows], k_sin[rows]).T.astype(BF16)
        gate_slab()
    av_ref[0] = _dot(x, w_ref[:, ATT_W + ATT_KV_W:ATT_W + 2 * ATT_KV_W]).astype(BF16)
    while len(gate_slabs_done) < n_gate_slabs:
        gate_slab()


def _proj_att_gate(xb3, w, wg, bg, ac, as_, qnw, knw, tm, tq):
    B, S, _ = xb3.shape
    assert tm % tq == 0, (tm, tq)
    n_gate = N_BRANCH * D_MODEL
    tab = pl.BlockSpec((tm, ATT_HD), lambda b, s: (s, 0))
    return pl.pallas_call(
        functools.partial(_proj_att_gate_kernel, tm=tm, tq=tq),
        grid=(B, S // tm),
        in_specs=[pl.BlockSpec((1, tm, D_MODEL), lambda b, s: (b, s, 0)), _const_spec((D_MODEL, ATT_W + 2 * ATT_KV_W)),
                  _const_spec((D_MODEL, n_gate)), _const_spec((1, n_gate)),
                  tab, tab, _const_spec((2, ATT_HD)), _const_spec((2, ATT_HD))],
        out_specs=[pl.BlockSpec((1, ATT_KV_HEADS, tm // tq, ATT_GROUP * tq, ATT_HD), lambda b, s: (b, 0, s, 0, 0)),
                   pl.BlockSpec((1, ATT_KV_HEADS, ATT_HD, tm), lambda b, s: (b, 0, 0, s)),
                   pl.BlockSpec((1, tm, ATT_KV_W), lambda b, s: (b, s, 0)),
                   pl.BlockSpec((1, tm, n_gate), lambda b, s: (b, s, 0))],
        out_shape=[jax.ShapeDtypeStruct((B, ATT_KV_HEADS, S // tq, ATT_GROUP * tq, ATT_HD), BF16),
                   jax.ShapeDtypeStruct((B, ATT_KV_HEADS, ATT_HD, S), BF16),
                   jax.ShapeDtypeStruct((B, S, ATT_KV_W), BF16),
                   jax.ShapeDtypeStruct((B, S, n_gate), BF16)],
        compiler_params=_cparams(("parallel", "parallel")),
        name="proj_att_gate",
    )(xb3, w, wg, bg, ac, as_, qnw, knw)


def _ret_kernel(q_ref, kt_ref, v_ref, dec_ref, gnw_ref, o_ref, st_ref, p_ref, *tot_refs, n_chunks, unroll):
    dtot = dec_ref[0, 0]
    xif, xib = dec_ref[0, 1], dec_ref[0, 2]
    zf, zb = dec_ref[0, 3], dec_ref[0, 4]
    cdf, cdb = dec_ref[0, 5], dec_ref[0, 6]
    gnw = gnw_ref[...]

    def rows_of(c):
        if isinstance(c, int):
            return pl.ds(c * CHUNK, CHUNK)
        return pl.ds(pl.multiple_of(c * CHUNK, CHUNK), CHUNK)

    def scan(i, carry):
        sf, sb = carry
        cf, cb = i, n_chunks - 1 - i
        st_ref[cf, :, :RET_DK] = sf.astype(BF16)
        st_ref[cb, :, RET_DK:] = sb.astype(BF16)
        kzf = (kt_ref[0, 0, cf].astype(F32) * zf).astype(BF16)
        kzb = (kt_ref[0, 0, cb].astype(F32) * zb).astype(BF16)
        sf = sf * cdf + _dot(kzf, v_ref[0, 0, rows_of(cf), :])
        sb = sb * cdb + _dot(kzb, v_ref[0, 0, rows_of(cb), :])
        s = _dot(q_ref[0, 0, rows_of(i), :], kt_ref[0, 0, i])
        p_ref[i] = (s * dtot).astype(BF16)
        return sf, sb

    zero = jnp.zeros((RET_DK, RET_DK), F32)
    lax.fori_loop(0, n_chunks, scan, (zero, zero), unroll=unroll)

    group = len(tot_refs)

    def mix_group(j):
        for k, tot_ref in enumerate(tot_refs):
            c = j * group + k
            rows = rows_of(c)
            inter = _dot(q_ref[0, 0, rows, :], st_ref[c])
            tot_ref[...] = _dot(p_ref[c], v_ref[0, 0, rows, :]) + inter[:, :RET_DK] * xif + inter[:, RET_DK:] * xib

    def norm_group(j):
        for k, tot_ref in enumerate(tot_refs):
            base = (j * group + k) * CHUNK
            for r in range(0, CHUNK, Q_PIECE):
                rows = pl.ds(base + r, Q_PIECE) if isinstance(base, int) else pl.ds(
                    pl.multiple_of(base + r, Q_PIECE), Q_PIECE)
                tot = tot_ref[r:r + Q_PIECE, :]
                mu = jnp.mean(tot, axis=-1, keepdims=True)
                var = jnp.maximum(jnp.mean(tot * tot, axis=-1, keepdims=True) - mu * mu, 0.0)
                y = (tot - mu) * lax.rsqrt(var + LN_EPS) * gnw
                o_ref[0, rows, :] = y.astype(BF16)

    mix_group(0)

    def step(j, carry):
        norm_group(j - 1)
        mix_group(j)
        return carry

    n_groups = n_chunks // group
    lax.fori_loop(1, n_groups, step, 0)
    norm_group(n_groups - 1)


def _retention(rq, rkt, rv, dec, gnw):
    B, _, S, _ = rq.shape
    n = S // CHUNK
    group = min(RET_UNROLL, n)
    assert n % group == 0, (n, group)
    head = pl.BlockSpec((1, 1, S, RET_DK), lambda b, h: (b, h, 0, 0))
    return pl.pallas_call(
        functools.partial(_ret_kernel, n_chunks=n, unroll=min(RET_UNROLL, n)),
        grid=(B, RET_HEADS),
        in_specs=[head,
                  pl.BlockSpec((1, 1, n, RET_DK, CHUNK), lambda b, h: (b, h, 0, 0, 0)),
                  head,
                  pl.BlockSpec((1, 7, CHUNK, CHUNK), lambda b, h: (h, 0, 0, 0)),
                  pl.BlockSpec((1, RET_DK), lambda b, h: (0, h))],
        out_specs=pl.BlockSpec((1, S, RET_DK), lambda b, h: (b, 0, h)),
        out_shape=jax.ShapeDtypeStruct((B, S, RET_W), BF16),
        scratch_shapes=([pltpu.VMEM((n, RET_DK, 2 * RET_DK), BF16), pltpu.VMEM((n, CHUNK, CHUNK), BF16)]
                        + [pltpu.VMEM((CHUNK, RET_DK), F32)] * group),
        compiler_params=_cparams(("parallel", "parallel")),
        name="retention",
    )(rq, rkt, rv, dec, gnw)


def _attn_kernel(bounded_ref, q_ref, kt_ref, v_ref, o_ref, *scratch, tq, tk, single_kv):
    q = q_ref[0, 0, 0]
    kt = kt_ref[0, 0]
    v1 = jnp.concatenate([v_ref[0], jnp.ones((tk, ATT_HD), BF16)], axis=1)
    bounded = bounded_ref[0] == 1

    def write(pv):
        o = pv[:, :ATT_HD] / pv[:, ATT_HD:]
        for g in range(ATT_GROUP):
            o_ref[0, :, g * ATT_HD:(g + 1) * ATT_HD] = o[g * tq:(g + 1) * tq].astype(BF16)

    if single_kv:
        @pl.when(bounded)
        def _():
            write(_dot(jnp.exp2(_dot(q, kt)).astype(BF16), v1))

        @pl.when(jnp.logical_not(bounded))
        def _():
            s = _dot(q, kt)
            write(_dot(jnp.exp2(s - jnp.max(s, axis=1, keepdims=True)).astype(BF16), v1))

        return

    m_ref, acc_ref = scratch
    ki = pl.program_id(3)

    @pl.when(ki == 0)
    def _():
        m_ref[...] = jnp.full(m_ref.shape, -jnp.inf, F32)
        acc_ref[...] = jnp.zeros(acc_ref.shape, F32)

    @pl.when(bounded)
    def _():
        acc_ref[...] += _dot(jnp.exp2(_dot(q, kt)).astype(BF16), v1)

    @pl.when(jnp.logical_not(bounded))
    def _():
        s = _dot(q, kt)
        m_prev = m_ref[...]
        m_next = jnp.maximum(m_prev, jnp.max(s, axis=1, keepdims=True))
        p = jnp.exp2(s - jnp.tile(m_next, (1, tk // ATT_HD)))
        alpha = jnp.exp2(m_prev - m_next)
        acc_ref[...] = jnp.tile(alpha, (1, 2)) * acc_ref[...] + _dot(p.astype(BF16), v1)
        m_ref[...] = m_next

    @pl.when(ki == pl.num_programs(3) - 1)
    def _():
        write(acc_ref[...])


def _att_tiles(S):
    if S <= ATT_SINGLE_KV_MAX:
        return _tile(S, ATT_TQ), S
    return _tile(S, ATT_TQ_STREAM), _tile(S, ATT_TK)


def _attention(bounded, aq, akt, av):
    B, S, _ = av.shape
    tq, tk = _att_tiles(S)
    assert aq.shape[3] == ATT_GROUP * tq, (aq.shape, tq)
    single_kv = tk == S
    rows = ATT_GROUP * tq
    scratch = [] if single_kv else [pltpu.VMEM((rows, ATT_HD), F32), pltpu.VMEM((rows, 2 * ATT_HD), F32)]
    return pl.pallas_call(
        functools.partial(_attn_kernel, tq=tq, tk=tk, single_kv=single_kv),
        grid=(B, ATT_KV_HEADS, S // tq, S // tk),
        in_specs=[pl.BlockSpec(memory_space=pltpu.SMEM),
                  pl.BlockSpec((1, 1, 1, ATT_GROUP * tq, ATT_HD), lambda b, k, i, j: (b, k, i, 0, 0)),
                  pl.BlockSpec((1, 1, ATT_HD, tk), lambda b, k, i, j: (b, k, 0, j)),
                  pl.BlockSpec((1, tk, ATT_HD), lambda b, k, i, j: (b, j, k))],
        out_specs=pl.BlockSpec((1, tq, ATT_GROUP * ATT_HD), lambda b, k, i, j: (b, i, k)),
        out_shape=jax.ShapeDtypeStruct((B, S, ATT_W), BF16),
        scratch_shapes=scratch,
        compiler_params=_cparams(("parallel", "parallel", "parallel", "arbitrary")),
        name="gqa_attention",
    )(bounded, aq, akt, av)


def _merge_kernel(ro_ref, rgs_ref, sg_ref, at_ref, gate_ref, x_ref, wr_ref, ws_ref, wa_ref, wo_ref, lnw_ref, lnb_ref,
                  xo_ref, xb_ref):
    def branch(a, w_ref, i):
        return gate_ref[:, i * D_MODEL:(i + 1) * D_MODEL].astype(F32) * _dot(a, w_ref[...])

    ret_in = ro_ref[...] * rgs_ref[...]
    merged = branch(ret_in, wr_ref, 0) + branch(sg_ref[...], ws_ref, 1) + branch(at_ref[...], wa_ref, 2)
    y = _dot(merged.astype(BF16), wo_ref[...])
    out = _layer_norm_rows(ALPHA * x_ref[...] + y, lnw_ref[...], lnb_ref[...])
    xo_ref[...] = out
    xb_ref[...] = out.astype(BF16)


def _merge(ro, rgs, sg, at, gates, x, wr, ws, wa, wo, lnw, lnb, tm):
    T = x.shape[0]
    row = pl.BlockSpec((tm, D_MODEL), lambda i: (i, 0))
    wspec = _const_spec((D_MODEL, D_MODEL))
    vec = _const_spec((1, D_MODEL))
    return pl.pallas_call(
        _merge_kernel,
        grid=(T // tm,),
        in_specs=[row, row, row, row, pl.BlockSpec((tm, N_BRANCH * D_MODEL), lambda i: (i, 0)), row,
                  wspec, wspec, wspec, wspec, vec, vec],
        out_specs=[row, row],
        out_shape=[jax.ShapeDtypeStruct((T, D_MODEL), F32), jax.ShapeDtypeStruct((T, D_MODEL), BF16)],
        compiler_params=_cparams(("parallel",)),
        name="merge_out_ln",
    )(ro, rgs, sg, at, gates, x, wr, ws, wa, wo, lnw, lnb)


def _cross_kernel(x_ref, xb_ref, kv_ref, wq_ref, wo_ref, lnw_ref, lnb_ref, xo_ref, xbo_ref, o_scr):
    q = _dot(xb_ref[0], wq_ref[...]).astype(BF16)
    scale = X_HD ** -0.5
    for h in range(X_HEADS):
        cols = slice(h * X_HD, (h + 1) * X_HD)
        k = kv_ref[0, :, cols]
        v = kv_ref[0, :, D_MODEL + h * X_HD:D_MODEL + (h + 1) * X_HD]
        s = lax.dot_general(q[:, cols], k, (((1,), (1,)), ((), ())), preferred_element_type=F32) * scale
        p = jnp.exp(s - jnp.max(s, axis=-1, keepdims=True))
        o = _dot(p.astype(BF16), v) / jnp.sum(p, axis=-1, keepdims=True)
        o_scr[:, cols] = o.astype(BF16)
    y = _dot(o_scr[...], wo_ref[...])
    out = _layer_norm_rows(ALPHA * x_ref[0] + y, lnw_ref[...], lnb_ref[...])
    xo_ref[0] = out
    xbo_ref[0] = out.astype(BF16)


def _cross(x3, xb3, kv3, wq, wo, lnw, lnb, tm):
    B, S, _ = x3.shape
    row = pl.BlockSpec((1, tm, D_MODEL), lambda b, i: (b, i, 0))
    wspec = _const_spec((D_MODEL, D_MODEL))
    vec = _const_spec((1, D_MODEL))
    return pl.pallas_call(
        _cross_kernel,
        grid=(B, S // tm),
        in_specs=[row, row, pl.BlockSpec((1, N_MEM, 2 * D_MODEL), lambda b, i: (b, 0, 0)), wspec, wspec, vec, vec],
        out_specs=[row, row],
        out_shape=[jax.ShapeDtypeStruct((B, S, D_MODEL), F32), jax.ShapeDtypeStruct((B, S, D_MODEL), BF16)],
        scratch_shapes=[pltpu.VMEM((tm, D_MODEL), BF16)],
        compiler_params=_cparams(("parallel", "parallel")),
        name="cross_attn_ln",
    )(x3, xb3, kv3, wq, wo, lnw, lnb)


def _ffn_kernel(x_ref, xb_ref, wa_ref, wb_ref, wo_ref, lnw_ref, lnb_ref, xo_ref, xbo_ref, h_scr):
    xb = xb_ref[...]
    for j in range(D_FF // FF_CHUNK):
        cols = slice(j * FF_CHUNK, (j + 1) * FF_CHUNK)
        a = _dot(xb, wa_ref[:, cols])
        b = _dot(xb, wb_ref[:, cols])
        h_scr[:, cols] = (a * _sigmoid(a) * b).astype(BF16)
    y = _dot(h_scr[...], wo_ref[...])
    out = _layer_norm_rows(ALPHA * x_ref[...] + y, lnw_ref[...], lnb_ref[...])
    xo_ref[...] = out
    xbo_ref[...] = out.astype(BF16)


def _ffn(x, xb, wa, wb, wo, lnw, lnb, tm):
    T = x.shape[0]
    row = pl.BlockSpec((tm, D_MODEL), lambda i: (i, 0))
    vec = _const_spec((1, D_MODEL))
    return pl.pallas_call(
        _ffn_kernel,
        grid=(T // tm,),
        in_specs=[row, row, _const_spec((D_MODEL, D_FF)), _const_spec((D_MODEL, D_FF)),
                  _const_spec((D_FF, D_MODEL)), vec, vec],
        out_specs=[row, row],
        out_shape=[jax.ShapeDtypeStruct((T, D_MODEL), F32), jax.ShapeDtypeStruct((T, D_MODEL), BF16)],
        scratch_shapes=[pltpu.VMEM((tm, D_FF), BF16)],
        compiler_params=_cparams(("parallel",)),
        name="swiglu_ln",
    )(x, xb, wa, wb, wo, lnw, lnb)


def _rope_tables(S):
    t = jnp.arange(S)

    def cos_sin(pos, dim):
        inv_freq = ROPE_BASE ** (-jnp.arange(0, dim, 2, dtype=F32) / dim)
        ang = pos.astype(F32)[:, None] * inv_freq[None, :]
        return jnp.cos(ang), jnp.sin(ang)

    ct, st = cos_sin(t, RET_DK)
    cr, sr = cos_sin(t // GRID_W, ATT_HD // 2)
    cc, sc = cos_sin(t % GRID_W, ATT_HD // 2)
    return (jnp.concatenate([ct, ct], -1), jnp.concatenate([-st, st], -1),
            jnp.concatenate([cr, cc, cr, cc], -1), jnp.concatenate([-sr, -sc, sr, sc], -1))


def _axial_dim_order():
    quarter = ATT_HD // 4
    blocks = [jnp.arange(i * quarter, (i + 1) * quarter) for i in (0, 2, 1, 3)]
    return jnp.concatenate(blocks)


def _decay_tables(decay_f, decay_b):
    lgf = jax.nn.log_sigmoid(decay_f.astype(F32))[:, None, None]
    lgb = jax.nn.log_sigmoid(decay_b.astype(F32))[:, None, None]
    idx = jnp.arange(CHUNK, dtype=F32)
    diff = idx[:, None] - idx[None, :]
    dtot = jnp.where(diff >= 0, jnp.exp(lgf * jnp.maximum(diff, 0.0)), jnp.exp(lgb * jnp.maximum(-diff, 0.0)))
    ones = jnp.ones((CHUNK, CHUNK), F32)
    row = idx[None, :, None] * ones
    lane = idx[None, None, :] * ones
    xif = jnp.exp(lgf * (row + 1.0))
    xib = jnp.exp(lgb * (CHUNK - row))
    zf = jnp.exp(lgf * (CHUNK - 1.0 - lane))
    zb = jnp.exp(lgb * lane)
    cdf = jnp.exp(lgf * CHUNK) * ones
    cdb = jnp.exp(lgb * CHUNK) * ones
    return jnp.stack([dtot, xif, xib, zf, zb, cdf, cdb], axis=1)


def _encoder(x, mem, p):
    B, S, _ = x.shape
    T = B * S
    tm = _tile(T, TOKEN_TILE)
    ts = _tile(S, TOKEN_TILE)
    rc, rs, ac, as_ = _rope_tables(S)
    xf, xb = _in_ln(x.reshape(T, D_MODEL), p["in_ln_w"], p["in_ln_b"], tm)
    memb = mem.astype(BF16).reshape(B * N_MEM, D_MODEL)
    flat = lambda a: a.reshape(T, D_MODEL)
    seq = lambda a: a.reshape(B, S, D_MODEL)
    for l in range(DEPTH):
        xb3 = seq(xb)
        rq, rkt = _proj_ret(xb3, p["w_ret"][l], rc, rs, ts)
        rv, rgs, sg = _proj_mid(xb3, p["w_mid"][l], p["sg_ln_w"][l], p["sg_ln_b"][l], p["sg_ws"][l], p["sg_bias"][l], ts)
        aq, akt, av, gates = _proj_att_gate(xb3, p["w_att"][l], p["w_gate"][l], p["b_gate"][l], ac, as_,
                                            p["att_qn_w"][l], p["att_kn_w"][l], ts, _att_tiles(S)[0])
        gates = gates.reshape(T, N_BRANCH * D_MODEL)
        ro = _retention(rq, rkt, rv, p["dec"][l], p["ret_gn_w"][l])
        at = _attention(p["att_bounded"][l], aq, akt, av)
        xf, xb = _merge(flat(ro), flat(rgs), flat(sg), flat(at), gates, xf, p["ret_wo"][l], p["sg_wo"][l], p["att_wo"][l],
                        p["w_out"][l], p["ln_w"][l, 0], p["ln_b"][l, 0], _tile(T, MERGE_TILE))
        kv = _matmul(memb, p["xa_wkv"][l], _tile(B * N_MEM, 1024), 1024, "proj_kv")
        xf3, xb3 = _cross(seq(xf), seq(xb), kv.reshape(B, N_MEM, 2 * D_MODEL), p["xa_wq"][l], p["xa_wo"][l],
                          p["ln_w"][l, 1], p["ln_b"][l, 1], ts)
        xf, xb = _ffn(flat(xf3), flat(xb3), p["ffn_wa"][l], p["ffn_wb"][l], p["ffn_w_out"][l],
                      p["ln_w"][l, 2], p["ln_b"][l, 2], tm)
    return seq(xf)


def _prepare_params(in_ln_w, in_ln_b, w_in, b_gate, ret_decay_f, ret_decay_b, ret_gn_w, ret_wo, sg_ln_w, sg_ln_b,
                    sg_ws, sg_b, sg_wo, att_qn_w, att_kn_w, att_wo, w_out, ln_w, ln_b, xa_wq, xa_wkv, xa_wo,
                    ffn_w_in, ffn_w_out):
    vec = lambda a: a.astype(F32).reshape(a.shape[:-1] + (1, a.shape[-1]))
    group = lambda c: w_in[:, :, c[0]:c[1]].astype(BF16)
    sg_bias = jnp.repeat(jnp.swapaxes(sg_b.astype(F32), 1, 2), SG_GW, axis=-1)
    score_bound = (SCORE_BOUND_COEF * jnp.max(jnp.abs(att_qn_w.astype(F32)), axis=-1)
                   * jnp.max(jnp.abs(att_kn_w.astype(F32)), axis=-1))
    order = _axial_dim_order()
    w_att = group(COLS_ATT)
    n_qk = ATT_W + ATT_KV_W
    w_qk = w_att[:, :, :n_qk].reshape(DEPTH, D_MODEL, n_qk // ATT_HD, ATT_HD)[..., order].reshape(DEPTH, D_MODEL, n_qk)
    w_att = jnp.concatenate([w_qk, w_att[:, :, n_qk:]], axis=-1)
    with_partner = lambda w: jnp.stack([w[:, order], jnp.roll(w[:, order], ATT_HD // 2, axis=-1)], axis=1).astype(F32)
    att_qn_w = with_partner(att_qn_w)
    att_kn_w = with_partner(att_kn_w)
    return dict(
        att_bounded=(score_bound <= SAFE_SCORE_BOUND).astype(jnp.int32).reshape(DEPTH, 1),
        in_ln_w=vec(in_ln_w), in_ln_b=vec(in_ln_b),
        w_ret=group(COLS_RET), w_mid=group(COLS_MID), w_att=w_att, w_gate=group(COLS_GATE),
        b_gate=vec(b_gate),
        dec=jnp.stack([_decay_tables(ret_decay_f[l], ret_decay_b[l]) for l in range(DEPTH)]),
        ret_gn_w=vec(ret_gn_w), ret_wo=ret_wo.astype(BF16),
        sg_ln_w=vec(sg_ln_w), sg_ln_b=vec(sg_ln_b), sg_ws=sg_ws.astype(BF16), sg_bias=sg_bias,
        sg_wo=sg_wo.astype(BF16),
        att_qn_w=att_qn_w, att_kn_w=att_kn_w, att_wo=att_wo.astype(BF16),
        w_out=w_out.astype(BF16), ln_w=vec(ln_w), ln_b=vec(ln_b),
        xa_wq=xa_wq.astype(BF16), xa_wkv=xa_wkv.astype(BF16), xa_wo=xa_wo.astype(BF16),
        ffn_wa=ffn_w_in[:, :, :D_FF].astype(BF16), ffn_wb=ffn_w_in[:, :, D_FF:].astype(BF16),
        ffn_w_out=ffn_w_out.astype(BF16),
    )


def kernel(x_prompt, x_sample, mem_prompt, mem_sample, in_ln_w, in_ln_b, w_in, b_gate, ret_decay_f, ret_decay_b,
           ret_gn_w, ret_wo, sg_ln_w, sg_ln_b, sg_ws, sg_b, sg_wo, att_qn_w, att_kn_w, att_wo, w_out, ln_w, ln_b,
           xa_wq, xa_wkv, xa_wo, ffn_w_in, ffn_w_out):
    p = _prepare_params(in_ln_w, in_ln_b, w_in, b_gate, ret_decay_f, ret_decay_b, ret_gn_w, ret_wo, sg_ln_w,
                        sg_ln_b, sg_ws, sg_b, sg_wo, att_qn_w, att_kn_w, att_wo, w_out, ln_w, ln_b, xa_wq, xa_wkv,
                        xa_wo, ffn_w_in, ffn_w_out)
    return (_encoder(x_prompt, mem_prompt, p), _encoder(x_sample, mem_sample, p))
```

```python
import functools

import jax
import jax.numpy as jnp
from jax import lax
from jax.experimental import pallas as pl
from jax.experimental.pallas import tpu as pltpu

F32 = jnp.float32
BF16 = jnp.bfloat16

D_MODEL = 1024
DEPTH = 4
N_MEM = 256
GRID_W = 64
CHUNK = 128
RET_HEADS = 8
RET_DK = 128
RET_W = RET_HEADS * RET_DK
SG_GROUPS = 4
SG_GW = 256
SG_W = SG_GROUPS * SG_GW
ATT_HEADS = 8
ATT_KV_HEADS = 2
ATT_GROUP = ATT_HEADS // ATT_KV_HEADS
ATT_HD = 128
ATT_W = ATT_HEADS * ATT_HD
ATT_KV_W = ATT_KV_HEADS * ATT_HD
X_HEADS = 4
X_HD = D_MODEL // X_HEADS
D_FF = 2816
N_BRANCH = 3
ALPHA = (2 * DEPTH) ** 0.25
ROPE_BASE = 10000.0
LN_EPS = 1e-5
RMS_EPS = 1e-6
LOG2E = 1.4426950408889634

COLS_RET = (0, 2 * RET_W)
COLS_MID = (COLS_RET[1], COLS_RET[1] + 2 * RET_W + 2 * SG_W)
COLS_ATT = (COLS_MID[1], COLS_MID[1] + ATT_W + 2 * ATT_KV_W)
COLS_GATE = (COLS_ATT[1], COLS_ATT[1] + N_BRANCH * D_MODEL)

MXU_W = 256
FF_CHUNK = MXU_W
SLAB = 512
Q_PIECE = 32
TAIL_ROWS = 256
TOKEN_TILE = 1024
MERGE_TILE = 512
ATT_TQ = 512
ATT_TQ_STREAM = 256
ATT_TK = 4096
ATT_SINGLE_KV_MAX = 2048
RET_UNROLL = 16
SCORE_BOUND_COEF = ATT_HD ** 0.5 * LOG2E
SAFE_SCORE_BOUND = 64.0
VMEM_LIMIT = 52 * 1024 * 1024


def _cparams(sem):
    return pltpu.CompilerParams(dimension_semantics=sem, vmem_limit_bytes=VMEM_LIMIT)


def _const_spec(shape):
    nd = len(shape)
    return pl.BlockSpec(shape, lambda *_: (0,) * nd, pipeline_mode=pl.Buffered(1))


def _tile(n, pref):
    t = min(n, pref)
    assert n % t == 0, (n, t)
    return t


def _layer_norm_rows(z, w, b):
    mu = jnp.mean(z, axis=-1, keepdims=True)
    d = z - mu
    var = jnp.mean(d * d, axis=-1, keepdims=True)
    return d * lax.rsqrt(var + LN_EPS) * w + b


def _gelu_tanh(x):
    return 0.5 * x * (1.0 + jnp.tanh(0.7978845608028654 * (x + 0.044715 * (x * x * x))))


def _sigmoid(x):
    return 1.0 / (1.0 + jnp.exp(-x))


def _dot(a, b):
    return jnp.dot(a, b, preferred_element_type=F32)


def _in_ln_kernel(x_ref, w_ref, b_ref, xo_ref, xb_ref):
    y = _layer_norm_rows(x_ref[...], w_ref[...], b_ref[...])
    xo_ref[...] = y
    xb_ref[...] = y.astype(BF16)


def _in_ln(x2, w, b, tm):
    T = x2.shape[0]
    row = pl.BlockSpec((tm, D_MODEL), lambda i: (i, 0))
    return pl.pallas_call(
        _in_ln_kernel,
        grid=(T // tm,),
        in_specs=[row, _const_spec((1, D_MODEL)), _const_spec((1, D_MODEL))],
        out_specs=[row, row],
        out_shape=[jax.ShapeDtypeStruct((T, D_MODEL), F32), jax.ShapeDtypeStruct((T, D_MODEL), BF16)],
        compiler_params=_cparams(("parallel",)),
        name="in_ln",
    )(x2, w, b)


def _mm_kernel(x_ref, w_ref, o_ref):
    o_ref[...] = _dot(x_ref[...], w_ref[...]).astype(o_ref.dtype)


def _matmul(x, w, tm, tn, name):
    M, K = x.shape
    N = w.shape[1]
    return pl.pallas_call(
        _mm_kernel,
        grid=(N // tn, M // tm),
        in_specs=[pl.BlockSpec((tm, K), lambda j, i: (i, 0)), pl.BlockSpec((K, tn), lambda j, i: (0, j))],
        out_specs=pl.BlockSpec((tm, tn), lambda j, i: (i, j)),
        out_shape=jax.ShapeDtypeStruct((M, N), BF16),
        compiler_params=_cparams(("parallel", "parallel")),
        name=name,
    )(x, w)


def _rope(x, cos, sin_signed, partner):
    return x * cos + partner * sin_signed


def _proj_ret_kernel(x_ref, w_ref, rc_ref, rs_ref, rq_ref, rkt_ref, *, tm):
    x = x_ref[0]
    rc = rc_ref[...]
    rs = rs_ref[...]
    k_scale = RET_DK ** -0.5

    def rope(t):
        return _rope(t, rc, rs, pltpu.roll(t, RET_DK // 2, axis=1))

    heads_per_slab = MXU_W // RET_DK
    for slab in range(RET_W // MXU_W):
        q2 = _dot(x, w_ref[:, slab * MXU_W:(slab + 1) * MXU_W])
        k2 = _dot(x, w_ref[:, RET_W + slab * MXU_W:RET_W + (slab + 1) * MXU_W])
        for j in range(heads_per_slab):
            h = slab * heads_per_slab + j
            sl = slice(j * RET_DK, (j + 1) * RET_DK)
            rq_ref[0, h] = rope(q2[:, sl]).astype(BF16)
            kt = (rope(k2[:, sl]) * k_scale).T
            for c in range(tm // CHUNK):
                rkt_ref[0, h, c] = kt[:, c * CHUNK:(c + 1) * CHUNK].astype(BF16)


def _proj_ret(xb3, w, rc, rs, tm):
    B, S, _ = xb3.shape
    tab = pl.BlockSpec((tm, RET_DK), lambda b, s: (s, 0))
    return pl.pallas_call(
        functools.partial(_proj_ret_kernel, tm=tm),
        grid=(B, S // tm),
        in_specs=[pl.BlockSpec((1, tm, D_MODEL), lambda b, s: (b, s, 0)), _const_spec((D_MODEL, 2 * RET_W)), tab, tab],
        out_specs=[pl.BlockSpec((1, RET_HEADS, tm, RET_DK), lambda b, s: (b, 0, s, 0)),
                   pl.BlockSpec((1, RET_HEADS, tm // CHUNK, RET_DK, CHUNK), lambda b, s: (b, 0, s, 0, 0))],
        out_shape=[jax.ShapeDtypeStruct((B, RET_HEADS, S, RET_DK), BF16),
                   jax.ShapeDtypeStruct((B, RET_HEADS, S // CHUNK, RET_DK, CHUNK), BF16)],
        compiler_params=_cparams(("parallel", "parallel")),
        name="proj_ret",
    )(xb3, w, rc, rs)


def _proj_mid_kernel(x_ref, w_ref, lnw_ref, lnb_ref, ws_ref, bias_ref, rv_ref, rgs_ref, sg_ref, vn_ref, *, tm):
    x = x_ref[0]
    sv = _gelu_tanh(_dot(x, w_ref[:, 2 * RET_W + SG_W:2 * RET_W + 2 * SG_W]))
    vn_ref[...] = _layer_norm_rows(sv, lnw_ref[...], lnb_ref[...]).astype(BF16)
    for grp in range(SG_GROUPS):
        cols = slice(grp * SG_GW, (grp + 1) * SG_GW)
        u = _gelu_tanh(_dot(x, w_ref[:, 2 * RET_W + grp * SG_GW:2 * RET_W + (grp + 1) * SG_GW]))
        for c in range(tm // CHUNK):
            rows = slice(c * CHUNK, (c + 1) * CHUNK)
            mixed = _dot(ws_ref[grp], vn_ref[rows, cols]) + bias_ref[:, cols]
            sg_ref[0, rows, cols] = (u[rows, :] * mixed).astype(BF16)
    for i in range(RET_W // SLAB):
        cols = slice(i * SLAB, (i + 1) * SLAB)
        g = _dot(x, w_ref[:, RET_W + i * SLAB:RET_W + (i + 1) * SLAB])
        rgs_ref[0, :, cols] = (g * _sigmoid(g)).astype(BF16)
        rv = _dot(x, w_ref[:, cols]).astype(BF16)
        for j in range(SLAB // RET_DK):
            rv_ref[0, i * (SLAB // RET_DK) + j] = rv[:, j * RET_DK:(j + 1) * RET_DK]


def _proj_mid(xb3, w, lnw, lnb, ws, bias, tm):
    B, S, _ = xb3.shape
    row = pl.BlockSpec((1, tm, D_MODEL), lambda b, s: (b, s, 0))
    out = jax.ShapeDtypeStruct((B, S, D_MODEL), BF16)
    return pl.pallas_call(
        functools.partial(_proj_mid_kernel, tm=tm),
        grid=(B, S // tm),
        in_specs=[row, _const_spec((D_MODEL, 2 * RET_W + 2 * SG_W)), _const_spec((1, SG_W)), _const_spec((1, SG_W)),
                  _const_spec((SG_GROUPS, CHUNK, CHUNK)), _const_spec((CHUNK, SG_W))],
        out_specs=[pl.BlockSpec((1, RET_HEADS, tm, RET_DK), lambda b, s: (b, 0, s, 0)), row, row],
        out_shape=[jax.ShapeDtypeStruct((B, RET_HEADS, S, RET_DK), BF16), out, out],
        scratch_shapes=[pltpu.VMEM((tm, SG_W), BF16)],
        compiler_params=_cparams(("parallel", "parallel")),
        name="proj_mid",
    )(xb3, w, lnw, lnb, ws, bias)


def _proj_att_gate_kernel(x_ref, w_ref, wg_ref, bg_ref, ac_ref, as_ref, qnw_ref, knw_ref,
                          aq_ref, akt_ref, av_ref, gate_ref, *, tm, tq):
    x = x_ref[0]

    n_gate_slabs = N_BRANCH * D_MODEL // MXU_W
    gate_slabs_done = []

    def gate_slab():
        i = len(gate_slabs_done)
        gate_slabs_done.append(i)
        cols = slice(i * MXU_W, (i + 1) * MXU_W)
        gate_ref[0, :, cols] = _sigmoid(_dot(x, wg_ref[:, cols]) + bg_ref[:, cols]).astype(BF16)

    n_q_slabs = ATT_W // MXU_W
    q_scale = ATT_HD ** -0.5 * LOG2E
    q_cos = ac_ref[...] * (qnw_ref[0:1, :] * q_scale)
    q_sin = as_ref[...] * (qnw_ref[1:2, :] * q_scale)
    k_cos = ac_ref[...] * knw_ref[0:1, :]
    k_sin = as_ref[...] * knw_ref[1:2, :]

    def norm_rope(t, cos, sin):
        r = lax.rsqrt(jnp.mean(t * t, axis=-1, keepdims=True) + RMS_EPS)
        return r * (t * cos + pltpu.roll(t, ATT_HD // 2, axis=1) * sin)

    heads_per_slab = MXU_W // ATT_HD
    for slab in range(n_q_slabs):
        q2 = _dot(x, w_ref[:, slab * MXU_W:(slab + 1) * MXU_W])
        for j in range(heads_per_slab):
            kv_head, g = divmod(slab * heads_per_slab + j, ATT_GROUP)
            for c in range(tm // Q_PIECE):
                rows = slice(c * Q_PIECE, (c + 1) * Q_PIECE)
                t = q2[rows, j * ATT_HD:(j + 1) * ATT_HD]
                q_tile, r0 = divmod(c * Q_PIECE, tq)
                aq_ref[0, kv_head, q_tile, g * tq + r0:g * tq + r0 + Q_PIECE, :] = (
                    norm_rope(t, q_cos[rows], q_sin[rows]).astype(BF16))
            gate_slab()
    k2 = _dot(x, w_ref[:, ATT_W:ATT_W + ATT_KV_W])
    for j in range(ATT_KV_HEADS):
        for c in range(tm // CHUNK):
            rows = slice(c * CHUNK, (c + 1) * CHUNK)
            t = k2[rows, j * ATT_HD:(j + 1) * ATT_HD]
            akt_ref[0, j, :, rows] = norm_rope(t, k_cos[rows], k_sin[rows]).T.astype(BF16)
        gate_slab()
    v2 = _dot(x, w_ref[:, ATT_W + ATT_KV_W:ATT_W + 2 * ATT_KV_W]).astype(BF16)
    ones = jnp.ones((tm, ATT_HD), BF16)
    for j in range(ATT_KV_HEADS):
        av_ref[0, j] = jnp.concatenate([v2[:, j * ATT_HD:(j + 1) * ATT_HD], ones], axis=1)
    while len(gate_slabs_done) < n_gate_slabs:
        gate_slab()


def _proj_att_gate(xb3, w, wg, bg, ac, as_, qnw, knw, tm, tq):
    B, S, _ = xb3.shape
    assert tm % tq == 0, (tm, tq)
    n_gate = N_BRANCH * D_MODEL
    tab = pl.BlockSpec((tm, ATT_HD), lambda b, s: (s, 0))
    return pl.pallas_call(
        functools.partial(_proj_att_gate_kernel, tm=tm, tq=tq),
        grid=(B, S // tm),
        in_specs=[pl.BlockSpec((1, tm, D_MODEL), lambda b, s: (b, s, 0)), _const_spec((D_MODEL, ATT_W + 2 * ATT_KV_W)),
                  _const_spec((D_MODEL, n_gate)), _const_spec((1, n_gate)),
                  tab, tab, _const_spec((2, ATT_HD)), _const_spec((2, ATT_HD))],
        out_specs=[pl.BlockSpec((1, ATT_KV_HEADS, tm // tq, ATT_GROUP * tq, ATT_HD), lambda b, s: (b, 0, s, 0, 0)),
                   pl.BlockSpec((1, ATT_KV_HEADS, ATT_HD, tm), lambda b, s: (b, 0, 0, s)),
                   pl.BlockSpec((1, ATT_KV_HEADS, tm, 2 * ATT_HD), lambda b, s: (b, 0, s, 0)),
                   pl.BlockSpec((1, tm, n_gate), lambda b, s: (b, s, 0))],
        out_shape=[jax.ShapeDtypeStruct((B, ATT_KV_HEADS, S // tq, ATT_GROUP * tq, ATT_HD), BF16),
                   jax.ShapeDtypeStruct((B, ATT_KV_HEADS, ATT_HD, S), BF16),
                   jax.ShapeDtypeStruct((B, ATT_KV_HEADS, S, 2 * ATT_HD), BF16),
                   jax.ShapeDtypeStruct((B, S, n_gate), BF16)],
        compiler_params=_cparams(("parallel", "parallel")),
        name="proj_att_gate",
    )(xb3, w, wg, bg, ac, as_, qnw, knw)


def _ret_kernel(q_ref, kt_ref, v_ref, dec_ref, gnw_ref, o_ref, st_ref, p_ref, *tot_refs, n_chunks, unroll):
    dtot = dec_ref[0, 0]
    xif, xib = dec_ref[0, 1], dec_ref[0, 2]
    zf, zb = dec_ref[0, 3], dec_ref[0, 4]
    cdf, cdb = dec_ref[0, 5], dec_ref[0, 6]
    gnw = gnw_ref[...]

    def rows_of(c):
        if isinstance(c, int):
            return pl.ds(c * CHUNK, CHUNK)
        return pl.ds(pl.multiple_of(c * CHUNK, CHUNK), CHUNK)

    def scan(i, carry):
        sf, sb = carry
        cf, cb = i, n_chunks - 1 - i
        st_ref[cf, :, :RET_DK] = sf.astype(BF16)
        st_ref[cb, :, RET_DK:] = sb.astype(BF16)
        kzf = (kt_ref[0, 0, cf].astype(F32) * zf).astype(BF16)
        kzb = (kt_ref[0, 0, cb].astype(F32) * zb).astype(BF16)
        sf = sf * cdf + _dot(kzf, v_ref[0, 0, rows_of(cf), :])
        sb = sb * cdb + _dot(kzb, v_ref[0, 0, rows_of(cb), :])
        s = _dot(q_ref[0, 0, rows_of(i), :], kt_ref[0, 0, i])
        p_ref[i] = (s * dtot).astype(BF16)
        return sf, sb

    zero = jnp.zeros((RET_DK, RET_DK), F32)
    lax.fori_loop(0, n_chunks, scan, (zero, zero), unroll=unroll)

    group = len(tot_refs)

    def mix_group(j):
        for k, tot_ref in enumerate(tot_refs):
            c = j * group + k
            rows = rows_of(c)
            inter = _dot(q_ref[0, 0, rows, :], st_ref[c])
            tot_ref[...] = _dot(p_ref[c], v_ref[0, 0, rows, :]) + inter[:, :RET_DK] * xif + inter[:, RET_DK:] * xib

    def norm_group(j):
        for k, tot_ref in enumerate(tot_refs):
            base = (j * group + k) * CHUNK
            for r in range(0, CHUNK, Q_PIECE):
                rows = pl.ds(base + r, Q_PIECE) if isinstance(base, int) else pl.ds(
                    pl.multiple_of(base + r, Q_PIECE), Q_PIECE)
                tot = tot_ref[r:r + Q_PIECE, :]
                mu = jnp.mean(tot, axis=-1, keepdims=True)
                var = jnp.maximum(jnp.mean(tot * tot, axis=-1, keepdims=True) - mu * mu, 0.0)
                y = (tot - mu) * lax.rsqrt(var + LN_EPS) * gnw
                o_ref[0, rows, :] = y.astype(BF16)

    mix_group(0)

    def step(j, carry):
        norm_group(j - 1)
        mix_group(j)
        return carry

    n_groups = n_chunks // group
    lax.fori_loop(1, n_groups, step, 0)
    norm_group(n_groups - 1)


def _retention(rq, rkt, rv, dec, gnw):
    B, _, S, _ = rq.shape
    n = S // CHUNK
    group = min(RET_UNROLL, n)
    assert n % group == 0, (n, group)
    head = pl.BlockSpec((1, 1, S, RET_DK), lambda b, h: (b, h, 0, 0))
    return pl.pallas_call(
        functools.partial(_ret_kernel, n_chunks=n, unroll=min(RET_UNROLL, n)),
        grid=(B, RET_HEADS),
        in_specs=[head,
                  pl.BlockSpec((1, 1, n, RET_DK, CHUNK), lambda b, h: (b, h, 0, 0, 0)),
                  head,
                  pl.BlockSpec((1, 7, CHUNK, CHUNK), lambda b, h: (h, 0, 0, 0)),
                  pl.BlockSpec((1, RET_DK), lambda b, h: (0, h))],
        out_specs=pl.BlockSpec((1, S, RET_DK), lambda b, h: (b, 0, h)),
        out_shape=jax.ShapeDtypeStruct((B, S, RET_W), BF16),
        scratch_shapes=([pltpu.VMEM((n, RET_DK, 2 * RET_DK), BF16), pltpu.VMEM((n, CHUNK, CHUNK), BF16)]
                        + [pltpu.VMEM((CHUNK, RET_DK), F32)] * group),
        compiler_params=_cparams(("parallel", "parallel")),
        name="retention",
    )(rq, rkt, rv, dec, gnw)


def _attn_kernel(bounded_ref, q_ref, kt_ref, v_ref, o_ref, *scratch, tq, tk, single_kv):
    bounded = bounded_ref[0] == 1

    def scores():
        return _dot(q_ref[0, 0, 0], kt_ref[0, 0])

    def weighted_values(p):
        return _dot(p.astype(BF16), v_ref[0, 0])

    def write(pv):
        o = pv[:, :ATT_HD] / pv[:, ATT_HD:]
        for g in range(ATT_GROUP):
            o_ref[0, :, g * ATT_HD:(g + 1) * ATT_HD] = o[g * tq:(g + 1) * tq].astype(BF16)

    if single_kv:
        @pl.when(bounded)
        def _():
            write(weighted_values(jnp.exp2(scores())))

        @pl.when(jnp.logical_not(bounded))
        def _():
            s = scores()
            write(weighted_values(jnp.exp2(s - jnp.max(s, axis=1, keepdims=True))))

        return

    m_ref, acc_ref = scratch
    ki = pl.program_id(3)

    @pl.when(ki == 0)
    def _():
        m_ref[...] = jnp.full(m_ref.shape, -jnp.inf, F32)
        acc_ref[...] = jnp.zeros(acc_ref.shape, F32)

    @pl.when(bounded)
    def _():
        acc_ref[...] += weighted_values(jnp.exp2(scores()))

    @pl.when(jnp.logical_not(bounded))
    def _():
        s = scores()
        m_prev = m_ref[...]
        m_next = jnp.maximum(m_prev, jnp.max(s, axis=1, keepdims=True))
        p = jnp.exp2(s - jnp.tile(m_next, (1, tk // ATT_HD)))
        alpha = jnp.exp2(m_prev - m_next)
        acc_ref[...] = jnp.tile(alpha, (1, 2)) * acc_ref[...] + weighted_values(p)
        m_ref[...] = m_next

    @pl.when(ki == pl.num_programs(3) - 1)
    def _():
        write(acc_ref[...])


def _att_tiles(S):
    if S <= ATT_SINGLE_KV_MAX:
        return _tile(S, ATT_TQ), S
    return _tile(S, ATT_TQ_STREAM), _tile(S, ATT_TK)


def _attention(bounded, aq, akt, av):
    B, _, S, _ = av.shape
    tq, tk = _att_tiles(S)
    assert aq.shape[3] == ATT_GROUP * tq, (aq.shape, tq)
    single_kv = tk == S
    rows = ATT_GROUP * tq
    scratch = [] if single_kv else [pltpu.VMEM((rows, ATT_HD), F32), pltpu.VMEM((rows, 2 * ATT_HD), F32)]
    return pl.pallas_call(
        functools.partial(_attn_kernel, tq=tq, tk=tk, single_kv=single_kv),
        grid=(B, ATT_KV_HEADS, S // tq, S // tk),
        in_specs=[pl.BlockSpec(memory_space=pltpu.SMEM),
                  pl.BlockSpec((1, 1, 1, ATT_GROUP * tq, ATT_HD), lambda b, k, i, j: (b, k, i, 0, 0)),
                  pl.BlockSpec((1, 1, ATT_HD, tk), lambda b, k, i, j: (b, k, 0, j)),
                  pl.BlockSpec((1, 1, tk, 2 * ATT_HD), lambda b, k, i, j: (b, k, j, 0))],
        out_specs=pl.BlockSpec((1, tq, ATT_GROUP * ATT_HD), lambda b, k, i, j: (b, i, k)),
        out_shape=jax.ShapeDtypeStruct((B, S, ATT_W), BF16),
        scratch_shapes=scratch,
        compiler_params=_cparams(("parallel", "parallel", "parallel", "arbitrary")),
        name="gqa_attention",
    )(bounded, aq, akt, av)


def _merge_kernel(ro_ref, rgs_ref, sg_ref, at_ref, gate_ref, x_ref, wr_ref, ws_ref, wa_ref, wo_ref, lnw_ref, lnb_ref,
                  xo_ref, xb_ref):
    def branch(a, w_ref, i):
        return gate_ref[:, i * D_MODEL:(i + 1) * D_MODEL].astype(F32) * _dot(a, w_ref[...])

    ret_in = ro_ref[...] * rgs_ref[...]
    merged = branch(ret_in, wr_ref, 0) + branch(sg_ref[...], ws_ref, 1) + branch(at_ref[...], wa_ref, 2)
    merged = merged.astype(BF16)
    part = min(TAIL_ROWS, merged.shape[0])
    for r in range(0, merged.shape[0], part):
        rows = slice(r, r + part)
        y = _dot(merged[rows, :], wo_ref[...])
        out = _layer_norm_rows(ALPHA * x_ref[rows, :] + y, lnw_ref[...], lnb_ref[...])
        xo_ref[rows, :] = out
        xb_ref[rows, :] = out.astype(BF16)


def _merge(ro, rgs, sg, at, gates, x, wr, ws, wa, wo, lnw, lnb, tm):
    T = x.shape[0]
    row = pl.BlockSpec((tm, D_MODEL), lambda i: (i, 0))
    wspec = _const_spec((D_MODEL, D_MODEL))
    vec = _const_spec((1, D_MODEL))
    return pl.pallas_call(
        _merge_kernel,
        grid=(T // tm,),
        in_specs=[row, row, row, row, pl.BlockSpec((tm, N_BRANCH * D_MODEL), lambda i: (i, 0)), row,
                  wspec, wspec, wspec, wspec, vec, vec],
        out_specs=[row, row],
        out_shape=[jax.ShapeDtypeStruct((T, D_MODEL), F32), jax.ShapeDtypeStruct((T, D_MODEL), BF16)],
        compiler_params=_cparams(("parallel",)),
        name="merge_out_ln",
    )(ro, rgs, sg, at, gates, x, wr, ws, wa, wo, lnw, lnb)


def _cross_kernel(x_ref, xb_ref, kv_ref, wq_ref, wo_ref, lnw_ref, lnb_ref, xo_ref, xbo_ref, o_scr):
    q = _dot(xb_ref[0], wq_ref[...]).astype(BF16)
    scale = X_HD ** -0.5
    for h in range(X_HEADS):
        cols = slice(h * X_HD, (h + 1) * X_HD)
        k = kv_ref[0, :, cols]
        v = kv_ref[0, :, D_MODEL + h * X_HD:D_MODEL + (h + 1) * X_HD]
        s = lax.dot_general(q[:, cols], k, (((1,), (1,)), ((), ())), preferred_element_type=F32) * scale
        p = jnp.exp(s - jnp.max(s, axis=-1, keepdims=True))
        o = _dot(p.astype(BF16), v) / jnp.sum(p, axis=-1, keepdims=True)
        o_scr[:, cols] = o.astype(BF16)
    part = min(TAIL_ROWS, o_scr.shape[0])
    for r in range(0, o_scr.shape[0], part):
        rows = slice(r, r + part)
        y = _dot(o_scr[rows, :], wo_ref[...])
        out = _layer_norm_rows(ALPHA * x_ref[0, rows, :] + y, lnw_ref[...], lnb_ref[...])
        xo_ref[0, rows, :] = out
        xbo_ref[0, rows, :] = out.astype(BF16)


def _cross(x3, xb3, kv3, wq, wo, lnw, lnb, tm):
    B, S, _ = x3.shape
    row = pl.BlockSpec((1, tm, D_MODEL), lambda b, i: (b, i, 0))
    wspec = _const_spec((D_MODEL, D_MODEL))
    vec = _const_spec((1, D_MODEL))
    return pl.pallas_call(
        _cross_kernel,
        grid=(B, S // tm),
        in_specs=[row, row, pl.BlockSpec((1, N_MEM, 2 * D_MODEL), lambda b, i: (b, 0, 0)), wspec, wspec, vec, vec],
        out_specs=[row, row],
        out_shape=[jax.ShapeDtypeStruct((B, S, D_MODEL), F32), jax.ShapeDtypeStruct((B, S, D_MODEL), BF16)],
        scratch_shapes=[pltpu.VMEM((tm, D_MODEL), BF16)],
        compiler_params=_cparams(("parallel", "parallel")),
        name="cross_attn_ln",
    )(x3, xb3, kv3, wq, wo, lnw, lnb)


def _ffn_kernel(x_ref, xb_ref, wa_ref, wb_ref, wo_ref, lnw_ref, lnb_ref, xo_ref, xbo_ref, h_scr):
    xb = xb_ref[...]
    for j in range(D_FF // FF_CHUNK):
        cols = slice(j * FF_CHUNK, (j + 1) * FF_CHUNK)
        a = _dot(xb, wa_ref[:, cols])
        b = _dot(xb, wb_ref[:, cols])
        h_scr[:, cols] = (a * _sigmoid(a) * b).astype(BF16)
    part = min(TAIL_ROWS, h_scr.shape[0])
    for r in range(0, h_scr.shape[0], part):
        rows = slice(r, r + part)
        y = _dot(h_scr[rows, :], wo_ref[...])
        out = _layer_norm_rows(ALPHA * x_ref[rows, :] + y, lnw_ref[...], lnb_ref[...])
        xo_ref[rows, :] = out
        xbo_ref[rows, :] = out.astype(BF16)


def _ffn(x, xb, wa, wb, wo, lnw, lnb, tm):
    T = x.shape[0]
    row = pl.BlockSpec((tm, D_MODEL), lambda i: (i, 0))
    vec = _const_spec((1, D_MODEL))
    return pl.pallas_call(
        _ffn_kernel,
        grid=(T // tm,),
        in_specs=[row, row, _const_spec((D_MODEL, D_FF)), _const_spec((D_MODEL, D_FF)),
                  _const_spec((D_FF, D_MODEL)), vec, vec],
        out_specs=[row, row],
        out_shape=[jax.ShapeDtypeStruct((T, D_MODEL), F32), jax.ShapeDtypeStruct((T, D_MODEL), BF16)],
        scratch_shapes=[pltpu.VMEM((tm, D_FF), BF16)],
        compiler_params=_cparams(("parallel",)),
        name="swiglu_ln",
    )(x, xb, wa, wb, wo, lnw, lnb)


def _rope_tables(S):
    t = jnp.arange(S)

    def cos_sin(pos, dim):
        inv_freq = ROPE_BASE ** (-jnp.arange(0, dim, 2, dtype=F32) / dim)
        ang = pos.astype(F32)[:, None] * inv_freq[None, :]
        return jnp.cos(ang), jnp.sin(ang)

    ct, st = cos_sin(t, RET_DK)
    cr, sr = cos_sin(t // GRID_W, ATT_HD // 2)
    cc, sc = cos_sin(t % GRID_W, ATT_HD // 2)
    return (jnp.concatenate([ct, ct], -1), jnp.concatenate([-st, st], -1),
            jnp.concatenate([cr, cc, cr, cc], -1), jnp.concatenate([-sr, -sc, sr, sc], -1))


def _axial_dim_order():
    quarter = ATT_HD // 4
    blocks = [jnp.arange(i * quarter, (i + 1) * quarter) for i in (0, 2, 1, 3)]
    return jnp.concatenate(blocks)


def _decay_tables(decay_f, decay_b):
    lgf = jax.nn.log_sigmoid(decay_f.astype(F32))[:, None, None]
    lgb = jax.nn.log_sigmoid(decay_b.astype(F32))[:, None, None]
    idx = jnp.arange(CHUNK, dtype=F32)
    diff = idx[:, None] - idx[None, :]
    dtot = jnp.where(diff >= 0, jnp.exp(lgf * jnp.maximum(diff, 0.0)), jnp.exp(lgb * jnp.maximum(-diff, 0.0)))
    ones = jnp.ones((CHUNK, CHUNK), F32)
    row = idx[None, :, None] * ones
    lane = idx[None, None, :] * ones
    xif = jnp.exp(lgf * (row + 1.0))
    xib = jnp.exp(lgb * (CHUNK - row))
    zf = jnp.exp(lgf * (CHUNK - 1.0 - lane))
    zb = jnp.exp(lgb * lane)
    cdf = jnp.exp(lgf * CHUNK) * ones
    cdb = jnp.exp(lgb * CHUNK) * ones
    return jnp.stack([dtot, xif, xib, zf, zb, cdf, cdb], axis=1)


def _encoder(x, mem, p):
    B, S, _ = x.shape
    T = B * S
    tm = _tile(T, TOKEN_TILE)
    ts = _tile(S, TOKEN_TILE)
    rc, rs, ac, as_ = _rope_tables(S)
    xf, xb = _in_ln(x.reshape(T, D_MODEL), p["in_ln_w"], p["in_ln_b"], tm)
    memb = mem.astype(BF16).reshape(B * N_MEM, D_MODEL)
    flat = lambda a: a.reshape(T, D_MODEL)
    seq = lambda a: a.reshape(B, S, D_MODEL)
    for l in range(DEPTH):
        xb3 = seq(xb)
        rq, rkt = _proj_ret(xb3, p["w_ret"][l], rc, rs, ts)
        rv, rgs, sg = _proj_mid(xb3, p["w_mid"][l], p["sg_ln_w"][l], p["sg_ln_b"][l], p["sg_ws"][l], p["sg_bias"][l], ts)
        aq, akt, av, gates = _proj_att_gate(xb3, p["w_att"][l], p["w_gate"][l], p["b_gate"][l], ac, as_,
                                            p["att_qn_w"][l], p["att_kn_w"][l], ts, _att_tiles(S)[0])
        gates = gates.reshape(T, N_BRANCH * D_MODEL)
        ro = _retention(rq, rkt, rv, p["dec"][l], p["ret_gn_w"][l])
        at = _attention(p["att_bounded"][l], aq, akt, av)
        xf, xb = _merge(flat(ro), flat(rgs), flat(sg), flat(at), gates, xf, p["ret_wo"][l], p["sg_wo"][l], p["att_wo"][l],
                        p["w_out"][l], p["ln_w"][l, 0], p["ln_b"][l, 0], _tile(T, MERGE_TILE))
        kv = _matmul(memb, p["xa_wkv"][l], _tile(B * N_MEM, 1024), 1024, "proj_kv")
        xf3, xb3 = _cross(seq(xf), seq(xb), kv.reshape(B, N_MEM, 2 * D_MODEL), p["xa_wq"][l], p["xa_wo"][l],
                          p["ln_w"][l, 1], p["ln_b"][l, 1], ts)
        xf, xb = _ffn(flat(xf3), flat(xb3), p["ffn_wa"][l], p["ffn_wb"][l], p["ffn_w_out"][l],
                      p["ln_w"][l, 2], p["ln_b"][l, 2], tm)
    return seq(xf)


def _prepare_params(in_ln_w, in_ln_b, w_in, b_gate, ret_decay_f, ret_decay_b, ret_gn_w, ret_wo, sg_ln_w, sg_ln_b,
                    sg_ws, sg_b, sg_wo, att_qn_w, att_kn_w, att_wo, w_out, ln_w, ln_b, xa_wq, xa_wkv, xa_wo,
                    ffn_w_in, ffn_w_out):
    vec = lambda a: a.astype(F32).reshape(a.shape[:-1] + (1, a.shape[-1]))
    group = lambda c: w_in[:, :, c[0]:c[1]].astype(BF16)
    sg_bias = jnp.repeat(jnp.swapaxes(sg_b.astype(F32), 1, 2), SG_GW, axis=-1)
    score_bound = (SCORE_BOUND_COEF * jnp.max(jnp.abs(att_qn_w.astype(F32)), axis=-1)
                   * jnp.max(jnp.abs(att_kn_w.astype(F32)), axis=-1))
    order = _axial_dim_order()
    w_att = group(COLS_ATT)
    n_qk = ATT_W + ATT_KV_W
    w_qk = w_att[:, :, :n_qk].reshape(DEPTH, D_MODEL, n_qk // ATT_HD, ATT_HD)[..., order].reshape(DEPTH, D_MODEL, n_qk)
    w_att = jnp.concatenate([w_qk, w_att[:, :, n_qk:]], axis=-1)
    with_partner = lambda w: jnp.stack([w[:, order], jnp.roll(w[:, order], ATT_HD // 2, axis=-1)], axis=1).astype(F32)
    att_qn_w = with_partner(att_qn_w)
    att_kn_w = with_partner(att_kn_w)
    return dict(
        att_bounded=(score_bound <= SAFE_SCORE_BOUND).astype(jnp.int32).reshape(DEPTH, 1),
        in_ln_w=vec(in_ln_w), in_ln_b=vec(in_ln_b),
        w_ret=group(COLS_RET), w_mid=group(COLS_MID), w_att=w_att, w_gate=group(COLS_GATE),
        b_gate=vec(b_gate),
        dec=jnp.stack([_decay_tables(ret_decay_f[l], ret_decay_b[l]) for l in range(DEPTH)]),
        ret_gn_w=vec(ret_gn_w), ret_wo=ret_wo.astype(BF16),
        sg_ln_w=vec(sg_ln_w), sg_ln_b=vec(sg_ln_b), sg_ws=sg_ws.astype(BF16), sg_bias=sg_bias,
        sg_wo=sg_wo.astype(BF16),
        att_qn_w=att_qn_w, att_kn_w=att_kn_w, att_wo=att_wo.astype(BF16),
        w_out=w_out.astype(BF16), ln_w=vec(ln_w), ln_b=vec(ln_b),
        xa_wq=xa_wq.astype(BF16), xa_wkv=xa_wkv.astype(BF16), xa_wo=xa_wo.astype(BF16),
        ffn_wa=ffn_w_in[:, :, :D_FF].astype(BF16), ffn_wb=ffn_w_in[:, :, D_FF:].astype(BF16),
        ffn_w_out=ffn_w_out.astype(BF16),
    )


def kernel(x_prompt, x_sample, mem_prompt, mem_sample, in_ln_w, in_ln_b, w_in, b_gate, ret_decay_f, ret_decay_b,
           ret_gn_w, ret_wo, sg_ln_w, sg_ln_b, sg_ws, sg_b, sg_wo, att_qn_w, att_kn_w, att_wo, w_out, ln_w, ln_b,
           xa_wq, xa_wkv, xa_wo, ffn_w_in, ffn_w_out):
    p = _prepare_params(in_ln_w, in_ln_b, w_in, b_gate, ret_decay_f, ret_decay_b, ret_gn_w, ret_wo, sg_ln_w,
                        sg_ln_b, sg_ws, sg_b, sg_wo, att_qn_w, att_kn_w, att_wo, w_out, ln_w, ln_b, xa_wq, xa_wkv,
                        xa_wo, ffn_w_in, ffn_w_out)
    return (_encoder(x_prompt, mem_prompt, p), _encoder(x_sample, mem_sample, p))
```

```python
import functools

import jax
import jax.numpy as jnp
from jax import lax
from jax.experimental import pallas as pl
from jax.experimental.pallas import tpu as pltpu

F32 = jnp.float32
BF16 = jnp.bfloat16

D_MODEL = 1024
DEPTH = 4
N_MEM = 256
GRID_W = 64
CHUNK = 128
RET_HEADS = 8
RET_DK = 128
RET_W = RET_HEADS * RET_DK
SG_GROUPS = 4
SG_GW = 256
SG_W = SG_GROUPS * SG_GW
ATT_HEADS = 8
ATT_KV_HEADS = 2
ATT_GROUP = ATT_HEADS // ATT_KV_HEADS
ATT_HD = 128
ATT_W = ATT_HEADS * ATT_HD
ATT_KV_W = ATT_KV_HEADS * ATT_HD
X_HEADS = 4
X_HD = D_MODEL // X_HEADS
D_FF = 2816
N_BRANCH = 3
ALPHA = (2 * DEPTH) ** 0.25
ROPE_BASE = 10000.0
LN_EPS = 1e-5
RMS_EPS = 1e-6
LOG2E = 1.4426950408889634

COLS_RET = (0, 2 * RET_W)
COLS_MID = (COLS_RET[1], COLS_RET[1] + 2 * RET_W + 2 * SG_W)
COLS_ATT = (COLS_MID[1], COLS_MID[1] + ATT_W + 2 * ATT_KV_W)
COLS_GATE = (COLS_ATT[1], COLS_ATT[1] + N_BRANCH * D_MODEL)

MXU_W = 256
FF_CHUNK = MXU_W
SLAB = 512
Q_PIECE = 32
TAIL_ROWS = 256
TOKEN_TILE = 1024
MERGE_TILE = 512
ATT_TQ = 512
ATT_TQ_STREAM = 256
ATT_TK = 4096
ATT_SINGLE_KV_MAX = 2048
RET_UNROLL = 16
SCORE_BOUND_COEF = ATT_HD ** 0.5 * LOG2E
SAFE_SCORE_BOUND = 64.0
VMEM_LIMIT = 52 * 1024 * 1024


def _cparams(sem):
    return pltpu.CompilerParams(dimension_semantics=sem, vmem_limit_bytes=VMEM_LIMIT)


def _const_spec(shape):
    nd = len(shape)
    return pl.BlockSpec(shape, lambda *_: (0,) * nd, pipeline_mode=pl.Buffered(1))


def _tile(n, pref):
    t = min(n, pref)
    assert n % t == 0, (n, t)
    return t


def _layer_norm_rows(z, w, b):
    mu = jnp.mean(z, axis=-1, keepdims=True)
    d = z - mu
    var = jnp.mean(d * d, axis=-1, keepdims=True)
    return d * lax.rsqrt(var + LN_EPS) * w + b


def _gelu_tanh(x):
    return 0.5 * x * (1.0 + jnp.tanh(0.7978845608028654 * (x + 0.044715 * (x * x * x))))


def _sigmoid(x):
    return 1.0 / (1.0 + jnp.exp(-x))


def _dot(a, b):
    return jnp.dot(a, b, preferred_element_type=F32)


def _in_ln_kernel(x_ref, w_ref, b_ref, xo_ref, xb_ref):
    y = _layer_norm_rows(x_ref[...], w_ref[...], b_ref[...])
    xo_ref[...] = y
    xb_ref[...] = y.astype(BF16)


def _in_ln(x2, w, b, tm):
    T = x2.shape[0]
    row = pl.BlockSpec((tm, D_MODEL), lambda i: (i, 0))
    return pl.pallas_call(
        _in_ln_kernel,
        grid=(T // tm,),
        in_specs=[row, _const_spec((1, D_MODEL)), _const_spec((1, D_MODEL))],
        out_specs=[row, row],
        out_shape=[jax.ShapeDtypeStruct((T, D_MODEL), F32), jax.ShapeDtypeStruct((T, D_MODEL), BF16)],
        compiler_params=_cparams(("parallel",)),
        name="in_ln",
    )(x2, w, b)


def _mm_kernel(x_ref, w_ref, o_ref):
    o_ref[...] = _dot(x_ref[...], w_ref[...]).astype(o_ref.dtype)


def _matmul(x, w, tm, tn, name):
    M, K = x.shape
    N = w.shape[1]
    return pl.pallas_call(
        _mm_kernel,
        grid=(N // tn, M // tm),
        in_specs=[pl.BlockSpec((tm, K), lambda j, i: (i, 0)), pl.BlockSpec((K, tn), lambda j, i: (0, j))],
        out_specs=pl.BlockSpec((tm, tn), lambda j, i: (i, j)),
        out_shape=jax.ShapeDtypeStruct((M, N), BF16),
        compiler_params=_cparams(("parallel", "parallel")),
        name=name,
    )(x, w)


def _rope(x, cos, sin_signed, partner):
    return x * cos + partner * sin_signed


def _proj_ret_kernel(x_ref, w_ref, rc_ref, rs_ref, rq_ref, rkt_ref, *, tm):
    x = x_ref[0]
    rc = rc_ref[...]
    rs = rs_ref[...]
    k_scale = RET_DK ** -0.5

    def rope(t):
        return _rope(t, rc, rs, pltpu.roll(t, RET_DK // 2, axis=1))

    heads_per_slab = MXU_W // RET_DK
    for slab in range(RET_W // MXU_W):
        q2 = _dot(x, w_ref[:, slab * MXU_W:(slab + 1) * MXU_W])
        k2 = _dot(x, w_ref[:, RET_W + slab * MXU_W:RET_W + (slab + 1) * MXU_W])
        for j in range(heads_per_slab):
            h = slab * heads_per_slab + j
            sl = slice(j * RET_DK, (j + 1) * RET_DK)
            rq_ref[0, h] = rope(q2[:, sl]).astype(BF16)
            kt = (rope(k2[:, sl]) * k_scale).T
            for c in range(tm // CHUNK):
                rkt_ref[0, h, c] = kt[:, c * CHUNK:(c + 1) * CHUNK].astype(BF16)


def _proj_ret(xb3, w, rc, rs, tm):
    B, S, _ = xb3.shape
    tab = pl.BlockSpec((tm, RET_DK), lambda b, s: (s, 0))
    return pl.pallas_call(
        functools.partial(_proj_ret_kernel, tm=tm),
        grid=(B, S // tm),
        in_specs=[pl.BlockSpec((1, tm, D_MODEL), lambda b, s: (b, s, 0)), _const_spec((D_MODEL, 2 * RET_W)), tab, tab],
        out_specs=[pl.BlockSpec((1, RET_HEADS, tm, RET_DK), lambda b, s: (b, 0, s, 0)),
                   pl.BlockSpec((1, RET_HEADS, tm // CHUNK, RET_DK, CHUNK), lambda b, s: (b, 0, s, 0, 0))],
        out_shape=[jax.ShapeDtypeStruct((B, RET_HEADS, S, RET_DK), BF16),
                   jax.ShapeDtypeStruct((B, RET_HEADS, S // CHUNK, RET_DK, CHUNK), BF16)],
        compiler_params=_cparams(("parallel", "parallel")),
        name="proj_ret",
    )(xb3, w, rc, rs)


def _proj_mid_kernel(x_ref, w_ref, lnw_ref, lnb_ref, ws_ref, bias_ref, rv_ref, rgs_ref, sg_ref, vn_ref, *, tm):
    x = x_ref[0]
    sv = _gelu_tanh(_dot(x, w_ref[:, 2 * RET_W + SG_W:2 * RET_W + 2 * SG_W]))
    vn_ref[...] = _layer_norm_rows(sv, lnw_ref[...], lnb_ref[...]).astype(BF16)
    for grp in range(SG_GROUPS):
        cols = slice(grp * SG_GW, (grp + 1) * SG_GW)
        u = _gelu_tanh(_dot(x, w_ref[:, 2 * RET_W + grp * SG_GW:2 * RET_W + (grp + 1) * SG_GW]))
        for c in range(tm // CHUNK):
            rows = slice(c * CHUNK, (c + 1) * CHUNK)
            mixed = _dot(ws_ref[grp], vn_ref[rows, cols]) + bias_ref[:, cols]
            sg_ref[0, rows, cols] = (u[rows, :] * mixed).astype(BF16)
    for i in range(RET_W // SLAB):
        cols = slice(i * SLAB, (i + 1) * SLAB)
        g = _dot(x, w_ref[:, RET_W + i * SLAB:RET_W + (i + 1) * SLAB])
        rgs_ref[0, :, cols] = (g * _sigmoid(g)).astype(BF16)
        rv = _dot(x, w_ref[:, cols]).astype(BF16)
        for j in range(SLAB // RET_DK):
            rv_ref[0, i * (SLAB // RET_DK) + j] = rv[:, j * RET_DK:(j + 1) * RET_DK]


def _proj_mid(xb3, w, lnw, lnb, ws, bias, tm):
    B, S, _ = xb3.shape
    row = pl.BlockSpec((1, tm, D_MODEL), lambda b, s: (b, s, 0))
    out = jax.ShapeDtypeStruct((B, S, D_MODEL), BF16)
    return pl.pallas_call(
        functools.partial(_proj_mid_kernel, tm=tm),
        grid=(B, S // tm),
        in_specs=[row, _const_spec((D_MODEL, 2 * RET_W + 2 * SG_W)), _const_spec((1, SG_W)), _const_spec((1, SG_W)),
                  _const_spec((SG_GROUPS, CHUNK, CHUNK)), _const_spec((CHUNK, SG_W))],
        out_specs=[pl.BlockSpec((1, RET_HEADS, tm, RET_DK), lambda b, s: (b, 0, s, 0)), row, row],
        out_shape=[jax.ShapeDtypeStruct((B, RET_HEADS, S, RET_DK), BF16), out, out],
        scratch_shapes=[pltpu.VMEM((tm, SG_W), BF16)],
        compiler_params=_cparams(("parallel", "parallel")),
        name="proj_mid",
    )(xb3, w, lnw, lnb, ws, bias)


def _proj_att_gate_kernel(x_ref, w_ref, wg_ref, bg_ref, ac_ref, as_ref, qnw_ref, knw_ref,
                          aq_ref, akt_ref, av_ref, gate_ref, *, tm, tq):
    x = x_ref[0]

    n_gate_slabs = N_BRANCH * D_MODEL // MXU_W
    gate_slabs_done = []

    def gate_slab():
        i = len(gate_slabs_done)
        gate_slabs_done.append(i)
        cols = slice(i * MXU_W, (i + 1) * MXU_W)
        gate_ref[0, :, cols] = _sigmoid(_dot(x, wg_ref[:, cols]) + bg_ref[:, cols]).astype(BF16)

    n_q_slabs = ATT_W // MXU_W
    q_scale = ATT_HD ** -0.5 * LOG2E
    q_cos = ac_ref[...] * (qnw_ref[0:1, :] * q_scale)
    q_sin = as_ref[...] * (qnw_ref[1:2, :] * q_scale)
    k_cos = ac_ref[...] * knw_ref[0:1, :]
    k_sin = as_ref[...] * knw_ref[1:2, :]

    def norm_rope(t, cos, sin):
        r = lax.rsqrt(jnp.mean(t * t, axis=-1, keepdims=True) + RMS_EPS)
        return r * (t * cos + pltpu.roll(t, ATT_HD // 2, axis=1) * sin)

    heads_per_slab = MXU_W // ATT_HD
    for slab in range(n_q_slabs):
        q2 = _dot(x, w_ref[:, slab * MXU_W:(slab + 1) * MXU_W])
        for j in range(heads_per_slab):
            kv_head, g = divmod(slab * heads_per_slab + j, ATT_GROUP)
            for c in range(tm // Q_PIECE):
                rows = slice(c * Q_PIECE, (c + 1) * Q_PIECE)
                t = q2[rows, j * ATT_HD:(j + 1) * ATT_HD]
                q_tile, r0 = divmod(c * Q_PIECE, tq)
                aq_ref[0, kv_head, q_tile, g * tq + r0:g * tq + r0 + Q_PIECE, :] = (
                    norm_rope(t, q_cos[rows], q_sin[rows]).astype(BF16))
            gate_slab()
    k2 = _dot(x, w_ref[:, ATT_W:ATT_W + ATT_KV_W])
    for j in range(ATT_KV_HEADS):
        for c in range(tm // CHUNK):
            rows = slice(c * CHUNK, (c + 1) * CHUNK)
            t = k2[rows, j * ATT_HD:(j + 1) * ATT_HD]
            akt_ref[0, j, :, rows] = norm_rope(t, k_cos[rows], k_sin[rows]).T.astype(BF16)
        gate_slab()
    v2 = _dot(x, w_ref[:, ATT_W + ATT_KV_W:ATT_W + 2 * ATT_KV_W]).astype(BF16)
    ones = jnp.ones((tm, ATT_HD), BF16)
    for j in range(ATT_KV_HEADS):
        av_ref[0, j] = jnp.concatenate([v2[:, j * ATT_HD:(j + 1) * ATT_HD], ones], axis=1)
    while len(gate_slabs_done) < n_gate_slabs:
        gate_slab()


def _proj_att_gate(xb3, w, wg, bg, ac, as_, qnw, knw, tm, tq):
    B, S, _ = xb3.shape
    assert tm % tq == 0, (tm, tq)
    n_gate = N_BRANCH * D_MODEL
    tab = pl.BlockSpec((tm, ATT_HD), lambda b, s: (s, 0))
    return pl.pallas_call(
        functools.partial(_proj_att_gate_kernel, tm=tm, tq=tq),
        grid=(B, S // tm),
        in_specs=[pl.BlockSpec((1, tm, D_MODEL), lambda b, s: (b, s, 0)), _const_spec((D_MODEL, ATT_W + 2 * ATT_KV_W)),
                  _const_spec((D_MODEL, n_gate)), _const_spec((1, n_gate)),
                  tab, tab, _const_spec((2, ATT_HD)), _const_spec((2, ATT_HD))],
        out_specs=[pl.BlockSpec((1, ATT_KV_HEADS, tm // tq, ATT_GROUP * tq, ATT_HD), lambda b, s: (b, 0, s, 0, 0)),
                   pl.BlockSpec((1, ATT_KV_HEADS, ATT_HD, tm), lambda b, s: (b, 0, 0, s)),
                   pl.BlockSpec((1, ATT_KV_HEADS, tm, 2 * ATT_HD), lambda b, s: (b, 0, s, 0)),
                   pl.BlockSpec((1, tm, n_gate), lambda b, s: (b, s, 0))],
        out_shape=[jax.ShapeDtypeStruct((B, ATT_KV_HEADS, S // tq, ATT_GROUP * tq, ATT_HD), BF16),
                   jax.ShapeDtypeStruct((B, ATT_KV_HEADS, ATT_HD, S), BF16),
                   jax.ShapeDtypeStruct((B, ATT_KV_HEADS, S, 2 * ATT_HD), BF16),
                   jax.ShapeDtypeStruct((B, S, n_gate), BF16)],
        compiler_params=_cparams(("parallel", "parallel")),
        name="proj_att_gate",
    )(xb3, w, wg, bg, ac, as_, qnw, knw)


def _ret_kernel(q_ref, kt_ref, v_ref, dec_ref, gnw_ref, o_ref, st_ref, p_ref, *tot_refs, n_chunks, unroll):
    dtot = dec_ref[0, 0]
    xif, xib = dec_ref[0, 1], dec_ref[0, 2]
    zf, zb = dec_ref[0, 3], dec_ref[0, 4]
    cdf, cdb = dec_ref[0, 5], dec_ref[0, 6]
    gnw = gnw_ref[...]

    def rows_of(c):
        if isinstance(c, int):
            return pl.ds(c * CHUNK, CHUNK)
        return pl.ds(pl.multiple_of(c * CHUNK, CHUNK), CHUNK)

    def scan(i, carry):
        sf, sb = carry
        cf, cb = i, n_chunks - 1 - i
        st_ref[cf, :, :RET_DK] = sf.astype(BF16)
        st_ref[cb, :, RET_DK:] = sb.astype(BF16)
        kzf = (kt_ref[0, 0, cf].astype(F32) * zf).astype(BF16)
        kzb = (kt_ref[0, 0, cb].astype(F32) * zb).astype(BF16)
        sf = sf * cdf + _dot(kzf, v_ref[0, 0, rows_of(cf), :])
        sb = sb * cdb + _dot(kzb, v_ref[0, 0, rows_of(cb), :])
        s = _dot(q_ref[0, 0, rows_of(i), :], kt_ref[0, 0, i])
        p_ref[i] = (s * dtot).astype(BF16)
        return sf, sb

    zero = jnp.zeros((RET_DK, RET_DK), F32)
    lax.fori_loop(0, n_chunks, scan, (zero, zero), unroll=unroll)

    group = len(tot_refs)

    def mix_group(j):
        for k, tot_ref in enumerate(tot_refs):
            c = j * group + k
            rows = rows_of(c)
            inter = _dot(q_ref[0, 0, rows, :], st_ref[c])
            tot_ref[...] = _dot(p_ref[c], v_ref[0, 0, rows, :]) + inter[:, :RET_DK] * xif + inter[:, RET_DK:] * xib

    def norm_group(j):
        for k, tot_ref in enumerate(tot_refs):
            base = (j * group + k) * CHUNK
            for r in range(0, CHUNK, Q_PIECE):
                rows = pl.ds(base + r, Q_PIECE) if isinstance(base, int) else pl.ds(
                    pl.multiple_of(base + r, Q_PIECE), Q_PIECE)
                tot = tot_ref[r:r + Q_PIECE, :]
                mu = jnp.mean(tot, axis=-1, keepdims=True)
                var = jnp.maximum(jnp.mean(tot * tot, axis=-1, keepdims=True) - mu * mu, 0.0)
                y = (tot - mu) * lax.rsqrt(var + LN_EPS) * gnw
                o_ref[0, rows, :] = y.astype(BF16)

    mix_group(0)

    def step(j, carry):
        norm_group(j - 1)
        mix_group(j)
        return carry

    n_groups = n_chunks // group
    lax.fori_loop(1, n_groups, step, 0)
    norm_group(n_groups - 1)


def _retention(rq, rkt, rv, dec, gnw):
    B, _, S, _ = rq.shape
    n = S // CHUNK
    group = min(RET_UNROLL, n)
    assert n % group == 0, (n, group)
    head = pl.BlockSpec((1, 1, S, RET_DK), lambda b, h: (b, h, 0, 0))
    return pl.pallas_call(
        functools.partial(_ret_kernel, n_chunks=n, unroll=min(RET_UNROLL, n)),
        grid=(B, RET_HEADS),
        in_specs=[head,
                  pl.BlockSpec((1, 1, n, RET_DK, CHUNK), lambda b, h: (b, h, 0, 0, 0)),
                  head,
                  pl.BlockSpec((1, 7, CHUNK, CHUNK), lambda b, h: (h, 0, 0, 0)),
                  pl.BlockSpec((1, RET_DK), lambda b, h: (0, h))],
        out_specs=pl.BlockSpec((1, S, RET_DK), lambda b, h: (b, 0, h)),
        out_shape=jax.ShapeDtypeStruct((B, S, RET_W), BF16),
        scratch_shapes=([pltpu.VMEM((n, RET_DK, 2 * RET_DK), BF16), pltpu.VMEM((n, CHUNK, CHUNK), BF16)]
                        + [pltpu.VMEM((CHUNK, RET_DK), F32)] * group),
        compiler_params=_cparams(("parallel", "parallel")),
        name="retention",
    )(rq, rkt, rv, dec, gnw)


def _attn_kernel(bounded_ref, q_ref, kt_ref, v_ref, o_ref, *scratch, tq, tk, single_kv):
    bounded = bounded_ref[0] == 1

    def scores():
        return _dot(q_ref[0, 0, 0], kt_ref[0, 0])

    def weighted_values(p):
        return _dot(p.astype(BF16), v_ref[0, 0])

    def write(pv):
        o = pv[:, :ATT_HD] / pv[:, ATT_HD:]
        for g in range(ATT_GROUP):
            o_ref[0, :, g * ATT_HD:(g + 1) * ATT_HD] = o[g * tq:(g + 1) * tq].astype(BF16)

    if single_kv:
        @pl.when(bounded)
        def _():
            p = jnp.exp2(scores()).astype(BF16)
            for g in range(ATT_GROUP):
                pv = _dot(p[g * tq:(g + 1) * tq], v_ref[0, 0])
                o_ref[0, :, g * ATT_HD:(g + 1) * ATT_HD] = (pv[:, :ATT_HD] / pv[:, ATT_HD:]).astype(BF16)

        @pl.when(jnp.logical_not(bounded))
        def _():
            s = scores()
            write(weighted_values(jnp.exp2(s - jnp.max(s, axis=1, keepdims=True))))

        return

    m_ref, acc_ref = scratch
    ki = pl.program_id(3)

    @pl.when(ki == 0)
    def _():
        m_ref[...] = jnp.full(m_ref.shape, -jnp.inf, F32)
        acc_ref[...] = jnp.zeros(acc_ref.shape, F32)

    @pl.when(bounded)
    def _():
        p = jnp.exp2(scores()).astype(BF16)
        for g in range(ATT_GROUP):
            rows = slice(g * tq, (g + 1) * tq)
            acc_ref[rows, :] += _dot(p[rows], v_ref[0, 0])

    @pl.when(jnp.logical_not(bounded))
    def _():
        s = scores()
        m_prev = m_ref[...]
        m_next = jnp.maximum(m_prev, jnp.max(s, axis=1, keepdims=True))
        p = jnp.exp2(s - jnp.tile(m_next, (1, tk // ATT_HD)))
        alpha = jnp.exp2(m_prev - m_next)
        acc_ref[...] = jnp.tile(alpha, (1, 2)) * acc_ref[...] + weighted_values(p)
        m_ref[...] = m_next

    @pl.when(ki == pl.num_programs(3) - 1)
    def _():
        write(acc_ref[...])


def _att_tiles(S):
    if S <= ATT_SINGLE_KV_MAX:
        return _tile(S, ATT_TQ), S
    return _tile(S, ATT_TQ_STREAM), _tile(S, ATT_TK)


def _attention(bounded, aq, akt, av):
    B, _, S, _ = av.shape
    tq, tk = _att_tiles(S)
    assert aq.shape[3] == ATT_GROUP * tq, (aq.shape, tq)
    single_kv = tk == S
    rows = ATT_GROUP * tq
    scratch = [] if single_kv else [pltpu.VMEM((rows, ATT_HD), F32), pltpu.VMEM((rows, 2 * ATT_HD), F32)]
    return pl.pallas_call(
        functools.partial(_attn_kernel, tq=tq, tk=tk, single_kv=single_kv),
        grid=(B, ATT_KV_HEADS, S // tq, S // tk),
        in_specs=[pl.BlockSpec(memory_space=pltpu.SMEM),
                  pl.BlockSpec((1, 1, 1, ATT_GROUP * tq, ATT_HD), lambda b, k, i, j: (b, k, i, 0, 0)),
                  pl.BlockSpec((1, 1, ATT_HD, tk), lambda b, k, i, j: (b, k, 0, j)),
                  pl.BlockSpec((1, 1, tk, 2 * ATT_HD), lambda b, k, i, j: (b, k, j, 0))],
        out_specs=pl.BlockSpec((1, tq, ATT_GROUP * ATT_HD), lambda b, k, i, j: (b, i, k)),
        out_shape=jax.ShapeDtypeStruct((B, S, ATT_W), BF16),
        scratch_shapes=scratch,
        compiler_params=_cparams(("parallel", "parallel", "parallel", "arbitrary")),
        name="gqa_attention",
    )(bounded, aq, akt, av)


def _merge_kernel(ro_ref, rgs_ref, sg_ref, at_ref, gate_ref, x_ref, wr_ref, ws_ref, wa_ref, wo_ref, lnw_ref, lnb_ref,
                  xo_ref, xb_ref):
    def branch(a, w_ref, i):
        return gate_ref[:, i * D_MODEL:(i + 1) * D_MODEL].astype(F32) * _dot(a, w_ref[...])

    ret_in = ro_ref[...] * rgs_ref[...]
    merged = branch(ret_in, wr_ref, 0) + branch(sg_ref[...], ws_ref, 1) + branch(at_ref[...], wa_ref, 2)
    merged = merged.astype(BF16)
    part = min(TAIL_ROWS, merged.shape[0])
    for r in range(0, merged.shape[0], part):
        rows = slice(r, r + part)
        y = _dot(merged[rows, :], wo_ref[...])
        out = _layer_norm_rows(ALPHA * x_ref[rows, :] + y, lnw_ref[...], lnb_ref[...])
        xo_ref[rows, :] = out
        xb_ref[rows, :] = out.astype(BF16)


def _merge(ro, rgs, sg, at, gates, x, wr, ws, wa, wo, lnw, lnb, tm):
    T = x.shape[0]
    row = pl.BlockSpec((tm, D_MODEL), lambda i: (i, 0))
    wspec = _const_spec((D_MODEL, D_MODEL))
    vec = _const_spec((1, D_MODEL))
    return pl.pallas_call(
        _merge_kernel,
        grid=(T // tm,),
        in_specs=[row, row, row, row, pl.BlockSpec((tm, N_BRANCH * D_MODEL), lambda i: (i, 0)), row,
                  wspec, wspec, wspec, wspec, vec, vec],
        out_specs=[row, row],
        out_shape=[jax.ShapeDtypeStruct((T, D_MODEL), F32), jax.ShapeDtypeStruct((T, D_MODEL), BF16)],
        compiler_params=_cparams(("parallel",)),
        name="merge_out_ln",
    )(ro, rgs, sg, at, gates, x, wr, ws, wa, wo, lnw, lnb)


def _cross_kernel(x_ref, xb_ref, kv_ref, wq_ref, wo_ref, lnw_ref, lnb_ref, xo_ref, xbo_ref, o_scr):
    q = _dot(xb_ref[0], wq_ref[...]).astype(BF16)
    scale = X_HD ** -0.5
    for h in range(X_HEADS):
        cols = slice(h * X_HD, (h + 1) * X_HD)
        k = kv_ref[0, :, cols]
        v = kv_ref[0, :, D_MODEL + h * X_HD:D_MODEL + (h + 1) * X_HD]
        s = lax.dot_general(q[:, cols], k, (((1,), (1,)), ((), ())), preferred_element_type=F32) * scale
        p = jnp.exp(s - jnp.max(s, axis=-1, keepdims=True))
        o = _dot(p.astype(BF16), v) / jnp.sum(p, axis=-1, keepdims=True)
        o_scr[:, cols] = o.astype(BF16)
    part = min(TAIL_ROWS, o_scr.shape[0])
    for r in range(0, o_scr.shape[0], part):
        rows = slice(r, r + part)
        y = _dot(o_scr[rows, :], wo_ref[...])
        out = _layer_norm_rows(ALPHA * x_ref[0, rows, :] + y, lnw_ref[...], lnb_ref[...])
        xo_ref[0, rows, :] = out
        xbo_ref[0, rows, :] = out.astype(BF16)


def _cross(x3, xb3, kv3, wq, wo, lnw, lnb, tm):
    B, S, _ = x3.shape
    row = pl.BlockSpec((1, tm, D_MODEL), lambda b, i: (b, i, 0))
    wspec = _const_spec((D_MODEL, D_MODEL))
    vec = _const_spec((1, D_MODEL))
    return pl.pallas_call(
        _cross_kernel,
        grid=(B, S // tm),
        in_specs=[row, row, pl.BlockSpec((1, N_MEM, 2 * D_MODEL), lambda b, i: (b, 0, 0)), wspec, wspec, vec, vec],
        out_specs=[row, row],
        out_shape=[jax.ShapeDtypeStruct((B, S, D_MODEL), F32), jax.ShapeDtypeStruct((B, S, D_MODEL), BF16)],
        scratch_shapes=[pltpu.VMEM((tm, D_MODEL), BF16)],
        compiler_params=_cparams(("parallel", "parallel")),
        name="cross_attn_ln",
    )(x3, xb3, kv3, wq, wo, lnw, lnb)


def _ffn_kernel(x_ref, xb_ref, wa_ref, wb_ref, wo_ref, lnw_ref, lnb_ref, xo_ref, xbo_ref, h_scr):
    xb = xb_ref[...]
    for j in range(D_FF // FF_CHUNK):
        cols = slice(j * FF_CHUNK, (j + 1) * FF_CHUNK)
        a = _dot(xb, wa_ref[:, cols])
        b = _dot(xb, wb_ref[:, cols])
        h_scr[:, cols] = (a * _sigmoid(a) * b).astype(BF16)
    part = min(TAIL_ROWS, h_scr.shape[0])
    for r in range(0, h_scr.shape[0], part):
        rows = slice(r, r + part)
        y = _dot(h_scr[rows, :], wo_ref[...])
        out = _layer_norm_rows(ALPHA * x_ref[rows, :] + y, lnw_ref[...], lnb_ref[...])
        xo_ref[rows, :] = out
        xbo_ref[rows, :] = out.astype(BF16)


def _ffn(x, xb, wa, wb, wo, lnw, lnb, tm):
    T = x.shape[0]
    row = pl.BlockSpec((tm, D_MODEL), lambda i: (i, 0))
    vec = _const_spec((1, D_MODEL))
    return pl.pallas_call(
        _ffn_kernel,
        grid=(T // tm,),
        in_specs=[row, row, _const_spec((D_MODEL, D_FF)), _const_spec((D_MODEL, D_FF)),
                  _const_spec((D_FF, D_MODEL)), vec, vec],
        out_specs=[row, row],
        out_shape=[jax.ShapeDtypeStruct((T, D_MODEL), F32), jax.ShapeDtypeStruct((T, D_MODEL), BF16)],
        scratch_shapes=[pltpu.VMEM((tm, D_FF), BF16)],
        compiler_params=_cparams(("parallel",)),
        name="swiglu_ln",
    )(x, xb, wa, wb, wo, lnw, lnb)


def _rope_tables(S):
    t = jnp.arange(S)

    def cos_sin(pos, dim):
        inv_freq = ROPE_BASE ** (-jnp.arange(0, dim, 2, dtype=F32) / dim)
        ang = pos.astype(F32)[:, None] * inv_freq[None, :]
        return jnp.cos(ang), jnp.sin(ang)

    ct, st = cos_sin(t, RET_DK)
    cr, sr = cos_sin(t // GRID_W, ATT_HD // 2)
    cc, sc = cos_sin(t % GRID_W, ATT_HD // 2)
    return (jnp.concatenate([ct, ct], -1), jnp.concatenate([-st, st], -1),
            jnp.concatenate([cr, cc, cr, cc], -1), jnp.concatenate([-sr, -sc, sr, sc], -1))


def _axial_dim_order():
    quarter = ATT_HD // 4
    blocks = [jnp.arange(i * quarter, (i + 1) * quarter) for i in (0, 2, 1, 3)]
    return jnp.concatenate(blocks)


def _decay_tables(decay_f, decay_b):
    lgf = jax.nn.log_sigmoid(decay_f.astype(F32))[:, None, None]
    lgb = jax.nn.log_sigmoid(decay_b.astype(F32))[:, None, None]
    idx = jnp.arange(CHUNK, dtype=F32)
    diff = idx[:, None] - idx[None, :]
    dtot = jnp.where(diff >= 0, jnp.exp(lgf * jnp.maximum(diff, 0.0)), jnp.exp(lgb * jnp.maximum(-diff, 0.0)))
    ones = jnp.ones((CHUNK, CHUNK), F32)
    row = idx[None, :, None] * ones
    lane = idx[None, None, :] * ones
    xif = jnp.exp(lgf * (row + 1.0))
    xib = jnp.exp(lgb * (CHUNK - row))
    zf = jnp.exp(lgf * (CHUNK - 1.0 - lane))
    zb = jnp.exp(lgb * lane)
    cdf = jnp.exp(lgf * CHUNK) * ones
    cdb = jnp.exp(lgb * CHUNK) * ones
    return jnp.stack([dtot, xif, xib, zf, zb, cdf, cdb], axis=1)


def _encoder(x, mem, p):
    B, S, _ = x.shape
    T = B * S
    tm = _tile(T, TOKEN_TILE)
    ts = _tile(S, TOKEN_TILE)
    rc, rs, ac, as_ = _rope_tables(S)
    xf, xb = _in_ln(x.reshape(T, D_MODEL), p["in_ln_w"], p["in_ln_b"], tm)
    memb = mem.astype(BF16).reshape(B * N_MEM, D_MODEL)
    flat = lambda a: a.reshape(T, D_MODEL)
    seq = lambda a: a.reshape(B, S, D_MODEL)
    for l in range(DEPTH):
        xb3 = seq(xb)
        rq, rkt = _proj_ret(xb3, p["w_ret"][l], rc, rs, ts)
        rv, rgs, sg = _proj_mid(xb3, p["w_mid"][l], p["sg_ln_w"][l], p["sg_ln_b"][l], p["sg_ws"][l], p["sg_bias"][l], ts)
        aq, akt, av, gates = _proj_att_gate(xb3, p["w_att"][l], p["w_gate"][l], p["b_gate"][l], ac, as_,
                                            p["att_qn_w"][l], p["att_kn_w"][l], ts, _att_tiles(S)[0])
        gates = gates.reshape(T, N_BRANCH * D_MODEL)
        ro = _retention(rq, rkt, rv, p["dec"][l], p["ret_gn_w"][l])
        at = _attention(p["att_bounded"][l], aq, akt, av)
        xf, xb = _merge(flat(ro), flat(rgs), flat(sg), flat(at), gates, xf, p["ret_wo"][l], p["sg_wo"][l], p["att_wo"][l],
                        p["w_out"][l], p["ln_w"][l, 0], p["ln_b"][l, 0], _tile(T, MERGE_TILE))
        kv = _matmul(memb, p["xa_wkv"][l], _tile(B * N_MEM, 1024), 1024, "proj_kv")
        xf3, xb3 = _cross(seq(xf), seq(xb), kv.reshape(B, N_MEM, 2 * D_MODEL), p["xa_wq"][l], p["xa_wo"][l],
                          p["ln_w"][l, 1], p["ln_b"][l, 1], ts)
        xf, xb = _ffn(flat(xf3), flat(xb3), p["ffn_wa"][l], p["ffn_wb"][l], p["ffn_w_out"][l],
                      p["ln_w"][l, 2], p["ln_b"][l, 2], tm)
    return seq(xf)


def _prepare_params(in_ln_w, in_ln_b, w_in, b_gate, ret_decay_f, ret_decay_b, ret_gn_w, ret_wo, sg_ln_w, sg_ln_b,
                    sg_ws, sg_b, sg_wo, att_qn_w, att_kn_w, att_wo, w_out, ln_w, ln_b, xa_wq, xa_wkv, xa_wo,
                    ffn_w_in, ffn_w_out):
    vec = lambda a: a.astype(F32).reshape(a.shape[:-1] + (1, a.shape[-1]))
    group = lambda c: w_in[:, :, c[0]:c[1]].astype(BF16)
    sg_bias = jnp.repeat(jnp.swapaxes(sg_b.astype(F32), 1, 2), SG_GW, axis=-1)
    score_bound = (SCORE_BOUND_COEF * jnp.max(jnp.abs(att_qn_w.astype(F32)), axis=-1)
                   * jnp.max(jnp.abs(att_kn_w.astype(F32)), axis=-1))
    order = _axial_dim_order()
    w_att = group(COLS_ATT)
    n_qk = ATT_W + ATT_KV_W
    w_qk = w_att[:, :, :n_qk].reshape(DEPTH, D_MODEL, n_qk // ATT_HD, ATT_HD)[..., order].reshape(DEPTH, D_MODEL, n_qk)
    w_att = jnp.concatenate([w_qk, w_att[:, :, n_qk:]], axis=-1)
    with_partner = lambda w: jnp.stack([w[:, order], jnp.roll(w[:, order], ATT_HD // 2, axis=-1)], axis=1).astype(F32)
    att_qn_w = with_partner(att_qn_w)
    att_kn_w = with_partner(att_kn_w)
    return dict(
        att_bounded=(score_bound <= SAFE_SCORE_BOUND).astype(jnp.int32).reshape(DEPTH, 1),
        in_ln_w=vec(in_ln_w), in_ln_b=vec(in_ln_b),
        w_ret=group(COLS_RET), w_mid=group(COLS_MID), w_att=w_att, w_gate=group(COLS_GATE),
        b_gate=vec(b_gate),
        dec=jnp.stack([_decay_tables(ret_decay_f[l], ret_decay_b[l]) for l in range(DEPTH)]),
        ret_gn_w=vec(ret_gn_w), ret_wo=ret_wo.astype(BF16),
        sg_ln_w=vec(sg_ln_w), sg_ln_b=vec(sg_ln_b), sg_ws=sg_ws.astype(BF16), sg_bias=sg_bias,
        sg_wo=sg_wo.astype(BF16),
        att_qn_w=att_qn_w, att_kn_w=att_kn_w, att_wo=att_wo.astype(BF16),
        w_out=w_out.astype(BF16), ln_w=vec(ln_w), ln_b=vec(ln_b),
        xa_wq=xa_wq.astype(BF16), xa_wkv=xa_wkv.astype(BF16), xa_wo=xa_wo.astype(BF16),
        ffn_wa=ffn_w_in[:, :, :D_FF].astype(BF16), ffn_wb=ffn_w_in[:, :, D_FF:].astype(BF16),
        ffn_w_out=ffn_w_out.astype(BF16),
    )


def kernel(x_prompt, x_sample, mem_prompt, mem_sample, in_ln_w, in_ln_b, w_in, b_gate, ret_decay_f, ret_decay_b,
           ret_gn_w, ret_wo, sg_ln_w, sg_ln_b, sg_ws, sg_b, sg_wo, att_qn_w, att_kn_w, att_wo, w_out, ln_w, ln_b,
           xa_wq, xa_wkv, xa_wo, ffn_w_in, ffn_w_out):
    p = _prepare_params(in_ln_w, in_ln_b, w_in, b_gate, ret_decay_f, ret_decay_b, ret_gn_w, ret_wo, sg_ln_w,
                        sg_ln_b, sg_ws, sg_b, sg_wo, att_qn_w, att_kn_w, att_wo, w_out, ln_w, ln_b, xa_wq, xa_wkv,
                        xa_wo, ffn_w_in, ffn_w_out)
    return (_encoder(x_prompt, mem_prompt, p), _encoder(x_sample, mem_sample, p))
```
